```python
import jax, jax.numpy as jnp
from jax import lax
import numpy as np

D_MODEL = 2048
BATCH = 4
SEQ = 2048
DEPTH = 1
DEC_BATCH = 128
DEC_SEQ = 4
PAST_LEN = 16384
PAGE_SIZE = 128

CONV_DIM = D_MODEL // 2
CONV_GROUPS = 8
CONV_WIDTH = 3
SG_DIM = D_MODEL // 2
SG_GROUPS = 8
SG_HEAD = SG_DIM // SG_GROUPS
CHUNK = 128
D_FF = -(-8 * D_MODEL // (3 * 256)) * 256
N_MOD = 6
EPS = 1e-6
IN_SPLITS = (CONV_DIM, CONV_DIM, CONV_DIM, SG_DIM, SG_DIM, D_MODEL, D_MODEL)
IN_COLS = sum(IN_SPLITS)

kernel_name = "hybrid_shortconv_gmlp_decoder_step"


def _rms(x, gain):
    xf = x.astype(jnp.float32)
    y = xf * lax.rsqrt(jnp.mean(xf * xf, axis=-1, keepdims=True) + EPS)
    return (y * gain.astype(jnp.float32)).astype(x.dtype)


def _layer(x, c, conv_prefix, chunk, g_mix, g_ffn, w_ada, b_ada, w_in, w_conv, g_v, w_sg, b_sg,
           w_pa, w_pb, w_out, w_ffn_in, w_ffn_out):
    n, s, _ = x.shape
    mod = (jax.nn.silu(c) @ w_ada + b_ada).reshape(n, N_MOD, 1, D_MODEL)
    sh_m, sc_m, gt_m, sh_f, sc_f, gt_f = [mod[:, i] for i in range(N_MOD)]

    h = _rms(x, g_mix) * (1 + sc_m) + sh_m
    proj = h @ w_in
    idx = np.cumsum(IN_SPLITS)[:-1].tolist()
    b_gate, c_gate, hc, u, v, ga, gb = jnp.split(proj, idx, axis=-1)

    z = c_gate * hc
    zfull = jnp.concatenate([conv_prefix.astype(z.dtype), z], axis=1)
    conv = sum(w_conv[k] * zfull[:, k:k + s] for k in range(CONV_WIDTH))
    ya = b_gate * conv

    u = jax.nn.gelu(u)
    v = _rms(jax.nn.gelu(v), g_v)
    vg = v.reshape(n, s // chunk, chunk, SG_GROUPS, SG_HEAD)
    w_causal = jnp.tril(w_sg[:, :chunk, :chunk])
    sp = jnp.einsum('gts,bnsgc->bntgc', w_causal, vg) + b_sg[:, :chunk].T[None, None, :, :, None]
    yb = u * sp.reshape(n, s, SG_DIM)

    merged = jax.nn.sigmoid(ga) * (ya @ w_pa) + jax.nn.sigmoid(gb) * (yb @ w_pb)
    x = x + gt_m * (merged @ w_out)

    h2 = _rms(x, g_ffn) * (1 + sc_f) + sh_f
    gate, up = jnp.split(h2 @ w_ffn_in, 2, axis=-1)
    x = x + gt_f * ((jax.nn.silu(gate) * up) @ w_ffn_out)

    conv_state = zfull[:, -(CONV_WIDTH - 1):]
    v_rows = vg[:, -1]
    return x, conv_state, v_rows


def setup_inputs(seed: int = 0) -> dict:
    key = jax.random.key(seed)
    ks = jax.random.split(key, 24)
    nrm = lambda k, shape, scale: jax.random.normal(k, shape, jnp.float32) * scale
    L = DEPTH
    return {
        "x_prompt": nrm(ks[0], (BATCH, SEQ, D_MODEL), 1.0),
        "x_sample": nrm(ks[1], (DEC_BATCH, DEC_SEQ, D_MODEL), 1.0),
        "state_conv": nrm(ks[2], (L, DEC_BATCH, CONV_WIDTH - 1, CONV_DIM), 1.0),
        "c_prompt": nrm(ks[3], (BATCH, D_MODEL), 1.0),
        "c_sample": nrm(ks[4], (DEC_BATCH, D_MODEL), 1.0),
        "g_mix": 1.0 + nrm(ks[5], (L, D_MODEL), 0.02),
        "g_ffn": 1.0 + nrm(ks[6], (L, D_MODEL), 0.02),
        "w_ada": nrm(ks[7], (L, D_MODEL, N_MOD * D_MODEL), 0.5 * D_MODEL ** -0.5),
        "b_ada": nrm(ks[8], (L, N_MOD * D_MODEL), 0.02),
        "w_in": nrm(ks[9], (L, D_MODEL, IN_COLS), D_MODEL ** -0.5),
        "w_conv": nrm(ks[10], (L, CONV_WIDTH, CONV_DIM), CONV_WIDTH ** -0.5),
        "g_v": 1.0 + nrm(ks[11], (L, SG_DIM), 0.02),
        "w_sg": nrm(ks[12], (L, SG_GROUPS, CHUNK, CHUNK), CHUNK ** -0.5),
        "b_sg": 1.0 + nrm(ks[13], (L, SG_GROUPS, CHUNK), 0.02),
        "w_pa": nrm(ks[14], (L, CONV_DIM, D_MODEL), CONV_DIM ** -0.5),
        "w_pb": nrm(ks[15], (L, SG_DIM, D_MODEL), SG_DIM ** -0.5),
        "w_out": nrm(ks[16], (L, D_MODEL, D_MODEL), D_MODEL ** -0.5),
        "w_ffn_in": nrm(ks[17], (L, D_MODEL, 2 * D_FF), D_MODEL ** -0.5),
        "w_ffn_out": nrm(ks[18], (L, D_FF, D_MODEL), D_FF ** -0.5),
        "g_final": 1.0 + nrm(ks[19], (D_MODEL,), 0.02),
    }


def reference(x_prompt, x_sample, state_conv, c_prompt, c_sample, g_mix, g_ffn, w_ada, b_ada,
              w_in, w_conv, g_v, w_sg, b_sg, w_pa, w_pb, w_out, w_ffn_in, w_ffn_out, g_final):
    xp, xs = x_prompt, x_sample
    conv_p, conv_s, sgv_p, sgv_s = [], [], [], []
    for l in range(DEPTH):
        params = (g_mix[l], g_ffn[l], w_ada[l], b_ada[l], w_in[l], w_conv[l], g_v[l], w_sg[l],
                  b_sg[l], w_pa[l], w_pb[l], w_out[l], w_ffn_in[l], w_ffn_out[l])
        zero_prefix = jnp.zeros((xp.shape[0], CONV_WIDTH - 1, CONV_DIM), xp.dtype)
        xp, cp, vp = _layer(xp, c_prompt, zero_prefix, CHUNK, *params)
        xs, cs, vs = _layer(xs, c_sample, state_conv[l], xs.shape[1], *params)
        conv_p.append(cp); conv_s.append(cs); sgv_p.append(vp); sgv_s.append(vs)
    y_prompt = _rms(xp, g_final)
    y_sample = _rms(xs, g_final)
    return (y_prompt, y_sample, jnp.stack(conv_p), jnp.stack(conv_s), jnp.stack(sgv_p), jnp.stack(sgv_s))
```

```python
import functools

import jax
import jax.numpy as jnp
from jax import lax
from jax.experimental import pallas as pl
from jax.experimental.pallas import tpu as pltpu

D_MODEL = 2048
CONV_DIM = D_MODEL // 2
CONV_WIDTH = 3
SG_DIM = D_MODEL // 2
SG_GROUPS = 8
SG_HEAD = SG_DIM // SG_GROUPS
CHUNK = 128
D_FF = 5632
N_MOD = 6
EPS = 1e-6
SEG = 1024
MIX_COLS = 3 * CONV_DIM + 2 * SG_DIM
SUBLANES = 8
VMEM_LIMIT_BYTES = 60 * 1024 * 1024

F32 = jnp.float32
BF16 = jnp.bfloat16


def _dot(a, b):
    return jnp.dot(a, b, preferred_element_type=F32)


def _rms(x, gain):
    return x * lax.rsqrt(jnp.mean(x * x, axis=-1, keepdims=True) + EPS) * gain


def _rowwise(fn, y, *mods):
    mr = mods[0].shape[0]
    if mr == 1 or mr == y.shape[0]:
        return fn(y, *mods)
    blocks = [fn(y[k * mr:(k + 1) * mr], *mods) for k in range(y.shape[0] // mr)]
    return jnp.concatenate(blocks, axis=0)


def _modulated_norm(x, gain, scale, shift):
    return _rowwise(lambda y, sc, sh: y * (1 + sc) + sh, _rms(x, gain), scale, shift)


def _gated(gate, y):
    return _rowwise(lambda yb, g: g * yb, y, gate)


def _ada_kernel(c_ref, w_ref, b_ref, mp_ref, ms_ref):
    a = jax.nn.silu(c_ref[...]).astype(BF16)
    r = _dot(a, w_ref[...].astype(BF16)) + b_ref[...]
    mp_ref[...] = r[:SUBLANES]
    ms_ref[...] = r[SUBLANES:]


def _ada(c_all, w_ada, b_ada, n_sample):
    tn = 1024
    per = D_MODEL // tn
    rows = c_all.shape[0]
    return pl.pallas_call(
        _ada_kernel,
        grid=(N_MOD * per,),
        in_specs=[
            pl.BlockSpec((rows, D_MODEL), lambda j: (0, 0)),
            pl.BlockSpec((D_MODEL, tn), lambda j: (0, j)),
            pl.BlockSpec((1, tn), lambda j: (0, j)),
        ],
        out_specs=[
            pl.BlockSpec((None, SUBLANES, tn), lambda j: (j // per, 0, j % per)),
            pl.BlockSpec((None, n_sample, tn), lambda j: (j // per, 0, j % per)),
        ],
        out_shape=[
            jax.ShapeDtypeStruct((N_MOD, SUBLANES, D_MODEL), F32),
            jax.ShapeDtypeStruct((N_MOD, n_sample, D_MODEL), F32),
        ],
        compiler_params=pltpu.CompilerParams(
            dimension_semantics=("arbitrary",), vmem_limit_bytes=VMEM_LIMIT_BYTES),
        name="ada",
    )(c_all, w_ada, b_ada)


def _in_proj(x_ref, gmix_ref, sc_ref, sh_ref, w_ref):
    h = _modulated_norm(x_ref[...], gmix_ref[...], sc_ref[...], sh_ref[...]).astype(BF16)
    return lambda k: _dot(h, w_ref[:, k * SEG:(k + 1) * SEG])


def _mix_prompt_kernel(x_ref, sc_ref, sh_ref, gmix_ref, w_ref, wconv_ref, gv_ref, wsg_ref, bsg_ref,
                       ya_ref, yb_ref, zst_ref, vst_ref, carry_ref, *, tiles_per_seq):
    tm = x_ref.shape[0]
    proj = _in_proj(x_ref, gmix_ref, sc_ref, sh_ref, w_ref)

    @pl.when(pl.program_id(0) % tiles_per_seq == 0)
    def _():
        carry_ref[...] = jnp.zeros_like(carry_ref)

    z = proj(1) * proj(2)
    carry = carry_ref[...]
    prev2, prev1 = carry[SUBLANES - 2:SUBLANES - 1], carry[SUBLANES - 1:SUBLANES]
    row = lax.broadcasted_iota(jnp.int32, (SUBLANES, CONV_DIM), 0)
    z1 = pltpu.roll(z, 1, 0)
    z2 = pltpu.roll(z, 2, 0)
    z1 = jnp.concatenate([jnp.where(row == 0, prev1, z1[:SUBLANES]), z1[SUBLANES:]], axis=0)
    z2_top = jnp.where(row == 0, prev2, jnp.where(row == 1, prev1, z2[:SUBLANES]))
    z2 = jnp.concatenate([z2_top, z2[SUBLANES:]], axis=0)
    wc = wconv_ref[...]
    conv = wc[0:1] * z2 + wc[1:2] * z1 + wc[2:3] * z
    ya_ref[...] = (proj(0) * conv).astype(BF16)
    carry_ref[...] = z[tm - SUBLANES:]
    zst_ref[...] = z[tm - SUBLANES:]

    gu = jax.nn.gelu(proj(3))
    vn = _rms(jax.nn.gelu(proj(4)), gv_ref[...])
    vst_ref[...] = vn[tm - CHUNK:]
    vb = vn.astype(BF16)
    causal = (lax.broadcasted_iota(jnp.int32, (CHUNK, CHUNK), 0)
              >= lax.broadcasted_iota(jnp.int32, (CHUNK, CHUNK), 1))
    wgs = [jnp.where(causal, wsg_ref[g], 0.0).astype(BF16) for g in range(SG_GROUPS)]
    bias = bsg_ref[...]
    chunks = []
    for c in range(tm // CHUNK):
        rows = slice(c * CHUNK, (c + 1) * CHUNK)
        parts = [_dot(wgs[g], vb[rows, g * SG_HEAD:(g + 1) * SG_HEAD]) for g in range(SG_GROUPS)]
        chunks.append(jnp.concatenate(parts, axis=1) + bias)
    yb_ref[...] = (gu * jnp.concatenate(chunks, axis=0)).astype(BF16)


def _mix_sample_kernel(x_ref, sc_ref, sh_ref, gmix_ref, w_ref, wconv_ref, gv_ref, wsg_ref, bsg_ref, st_ref,
                       ya_ref, yb_ref, zst_ref, vst_ref, *, steps):
    nb = x_ref.shape[0] // steps
    proj = _in_proj(x_ref, gmix_ref, sc_ref, sh_ref, w_ref)
    step_rows = lambda a, t: a[t * nb:(t + 1) * nb]

    z = proj(1) * proj(2)
    zf = [st_ref[k] for k in range(CONV_WIDTH - 1)] + [step_rows(z, t) for t in range(steps)]
    wc = wconv_ref[...]
    conv = jnp.concatenate(
        [wc[0:1] * zf[t] + wc[1:2] * zf[t + 1] + wc[2:3] * zf[t + 2] for t in range(steps)], axis=0)
    ya_ref[...] = (proj(0) * conv).astype(BF16)
    for k in range(CONV_WIDTH - 1):
        zst_ref[k] = zf[steps + k]

    gu = jax.nn.gelu(proj(3))
    vn = _rms(jax.nn.gelu(proj(4)), gv_ref[...])
    vst_ref[...] = vn
    sps = []
    for t in range(steps):
        acc = wsg_ref[t * steps:t * steps + 1] * step_rows(vn, 0)
        for s in range(1, t + 1):
            acc = acc + wsg_ref[t * steps + s:t * steps + s + 1] * step_rows(vn, s)
        sps.append(acc + bsg_ref[t:t + 1])
    yb_ref[...] = (gu * jnp.concatenate(sps, axis=0)).astype(BF16)


def _resident(shape):
    return pl.BlockSpec(shape, lambda i: (0,) * len(shape), pipeline_mode=pl.Buffered(1))


def _prompt_mod_spec(comp, tiles_per_seq):
    return pl.BlockSpec((None, None, 1, D_MODEL), lambda i: (comp, i // tiles_per_seq, 0, 0))


def _sample_mod_spec(comp, n_sample):
    return pl.BlockSpec((None, n_sample, D_MODEL), lambda i: (comp, 0, 0))


def _params():
    return pltpu.CompilerParams(dimension_semantics=("arbitrary",), vmem_limit_bytes=VMEM_LIMIT_BYTES)


def _mix_prompt(x, mod, g_mix, w_mix, w_conv, g_v, w_sg, bias, seq, tm):
    rows = x.shape[0]
    n_seq = rows // seq
    tiles_per_seq = seq // tm
    row_tile = lambda width: pl.BlockSpec((tm, width), lambda i: (i, 0))
    return pl.pallas_call(
        functools.partial(_mix_prompt_kernel, tiles_per_seq=tiles_per_seq),
        grid=(rows // tm,),
        in_specs=[
            row_tile(D_MODEL),
            _prompt_mod_spec(1, tiles_per_seq),
            _prompt_mod_spec(0, tiles_per_seq),
            _resident((1, D_MODEL)),
            _resident((D_MODEL, MIX_COLS)),
            _resident((CONV_WIDTH, CONV_DIM)),
            _resident((1, SG_DIM)),
            _resident((SG_GROUPS, CHUNK, CHUNK)),
            _resident((CHUNK, SG_DIM)),
        ],
        out_specs=[
            row_tile(CONV_DIM),
            row_tile(SG_DIM),
            pl.BlockSpec((None, SUBLANES, CONV_DIM), lambda i: (i // tiles_per_seq, 0, 0)),
            pl.BlockSpec((None, CHUNK, SG_DIM), lambda i: (i // tiles_per_seq, 0, 0)),
        ],
        out_shape=[
            jax.ShapeDtypeStruct((rows, CONV_DIM), BF16),
            jax.ShapeDtypeStruct((rows, SG_DIM), BF16),
            jax.ShapeDtypeStruct((n_seq, SUBLANES, CONV_DIM), F32),
            jax.ShapeDtypeStruct((n_seq, CHUNK, SG_DIM), F32),
        ],
        scratch_shapes=[pltpu.VMEM((SUBLANES, CONV_DIM), F32)],
        compiler_params=_params(),
        name="mix_prompt",
    )(x, mod, mod, g_mix, w_mix, w_conv, g_v, w_sg, bias)


def _mix_sample(x, mod, g_mix, w_mix, w_conv, g_v, w_sg_rows, bias_rows, state, steps):
    rows = x.shape[0]
    nb = rows // steps
    full = lambda shape: pl.BlockSpec(shape, lambda i: (0,) * len(shape))
    return pl.pallas_call(
        functools.partial(_mix_sample_kernel, steps=steps),
        grid=(1,),
        in_specs=[
            full((rows, D_MODEL)),
            _sample_mod_spec(1, nb),
            _sample_mod_spec(0, nb),
            full((1, D_MODEL)),
            _resident((D_MODEL, MIX_COLS)),
            full((CONV_WIDTH, CONV_DIM)),
            full((1, SG_DIM)),
            full((steps * steps, SG_DIM)),
            full((steps, SG_DIM)),
            full((CONV_WIDTH - 1, nb, CONV_DIM)),
        ],
        out_specs=[
            full((rows, CONV_DIM)),
            full((rows, SG_DIM)),
            full((CONV_WIDTH - 1, nb, CONV_DIM)),
            full((rows, SG_DIM)),
        ],
        out_shape=[
            jax.ShapeDtypeStruct((rows, CONV_DIM), BF16),
            jax.ShapeDtypeStruct((rows, SG_DIM), BF16),
            jax.ShapeDtypeStruct((CONV_WIDTH - 1, nb, CONV_DIM), F32),
            jax.ShapeDtypeStruct((rows, SG_DIM), F32),
        ],
        compiler_params=_params(),
        name="mix_sample",
    )(x, mod, mod, g_mix, w_mix, w_conv, g_v, w_sg_rows, bias_rows, state)


def _merge_kernel(x_ref, ya_ref, yb_ref, scm_ref, shm_ref, gtm_ref, scf_ref, shf_ref, gmix_ref, gffn_ref,
                  wg_ref, wpa_ref, wpb_ref, wout_ref, x1_ref, h2_ref):
    x = x_ref[...]
    h = _modulated_norm(x, gmix_ref[...], scm_ref[...], shm_ref[...]).astype(BF16)
    ya = ya_ref[...]
    yb = yb_ref[...]
    halves = []
    for k in range(D_MODEL // SEG):
        cols = slice(k * SEG, (k + 1) * SEG)
        ga = _dot(h, wg_ref[:, k * SEG:(k + 1) * SEG])
        gb = _dot(h, wg_ref[:, D_MODEL + k * SEG:D_MODEL + (k + 1) * SEG])
        branch_a = jax.nn.sigmoid(ga) * _dot(ya, wpa_ref[:, cols])
        branch_b = jax.nn.sigmoid(gb) * _dot(yb, wpb_ref[:, cols])
        halves.append((branch_a + branch_b).astype(BF16))
    merged = jnp.concatenate(halves, axis=1)
    x1 = x + _gated(gtm_ref[...], _dot(merged, wout_ref[...]))
    x1_ref[...] = x1
    h2_ref[...] = _modulated_norm(x1, gffn_ref[...], scf_ref[...], shf_ref[...]).astype(BF16)


def _merge(x, ya, yb, mod, mod_spec, g_mix, g_ffn, w_gate, w_pa, w_pb, w_out, tm):
    rows = x.shape[0]
    row_tile = lambda width: pl.BlockSpec((tm, width), lambda i: (i, 0))
    return pl.pallas_call(
        _merge_kernel,
        grid=(rows // tm,),
        in_specs=[
            row_tile(D_MODEL), row_tile(CONV_DIM), row_tile(SG_DIM),
            mod_spec(1), mod_spec(0), mod_spec(2), mod_spec(4), mod_spec(3),
            _resident((1, D_MODEL)), _resident((1, D_MODEL)),
            _resident((D_MODEL, 2 * D_MODEL)),
            _resident((CONV_DIM, D_MODEL)),
            _resident((SG_DIM, D_MODEL)),
            _resident((D_MODEL, D_MODEL)),
        ],
        out_specs=[row_tile(D_MODEL), row_tile(D_MODEL)],
        out_shape=[
            jax.ShapeDtypeStruct((rows, D_MODEL), F32),
            jax.ShapeDtypeStruct((rows, D_MODEL), BF16),
        ],
        compiler_params=_params(),
        name="merge",
    )(x, ya, yb, mod, mod, mod, mod, mod, g_mix, g_ffn, w_gate, w_pa, w_pb, w_out)


def _ffn_in_kernel(h_ref, wg_ref, wu_ref, o_ref):
    h = h_ref[...]
    o_ref[...] = (jax.nn.silu(_dot(h, wg_ref[...])) * _dot(h, wu_ref[...])).astype(BF16)


def _ffn_in(h2, w_ffn_in, tm, tn):
    rows = h2.shape[0]
    n_blocks = D_FF // tn
    return pl.pallas_call(
        _ffn_in_kernel,
        grid=(n_blocks, rows // tm),
        in_specs=[
            pl.BlockSpec((tm, D_MODEL), lambda j, i: (i, 0)),
            pl.BlockSpec((D_MODEL, tn), lambda j, i: (0, j)),
            pl.BlockSpec((D_MODEL, tn), lambda j, i: (0, j + n_blocks)),
        ],
        out_specs=pl.BlockSpec((tm, tn), lambda j, i: (i, j)),
        out_shape=jax.ShapeDtypeStruct((rows, D_FF), BF16),
        compiler_params=pltpu.CompilerParams(
            dimension_semantics=("arbitrary", "arbitrary"), vmem_limit_bytes=VMEM_LIMIT_BYTES),
        name="ffn_in",
    )(h2, w_ffn_in, w_ffn_in)


def _ffn_out_kernel(a_ref, x1_ref, gt_ref, gfin_ref, w_ref, y_ref):
    x2 = x1_ref[...] + _gated(gt_ref[...], _dot(a_ref[...], w_ref[...]))
    y_ref[...] = _rms(x2, gfin_ref[...])


def _ffn_out(act, x1, mod, mod_spec, g_final, w_ffn_out, tm):
    rows = x1.shape[0]
    row_tile = lambda width: pl.BlockSpec((tm, width), lambda i: (i, 0))
    return pl.pallas_call(
        _ffn_out_kernel,
        grid=(rows // tm,),
        in_specs=[
            row_tile(D_FF), row_tile(D_MODEL), mod_spec(5),
            _resident((1, D_MODEL)),
            _resident((D_FF, D_MODEL)),
        ],
        out_specs=row_tile(D_MODEL),
        out_shape=jax.ShapeDtypeStruct((rows, D_MODEL), F32),
        compiler_params=_params(),
        name="ffn_out",
    )(act, x1, mod, g_final, w_ffn_out)


def _lane_expand(per_group):
    return jnp.repeat(per_group, SG_HEAD, axis=1)


def kernel(x_prompt, x_sample, state_conv, c_prompt, c_sample, g_mix, g_ffn, w_ada, b_ada, w_in, w_conv, g_v,
           w_sg, b_sg, w_pa, w_pb, w_out, w_ffn_in, w_ffn_out, g_final):
    assert g_mix.shape[0] == 1, "one layer"
    batch, seq, _ = x_prompt.shape
    n_sample, steps, _ = x_sample.shape
    assert batch <= SUBLANES and seq % CHUNK == 0

    row = lambda v: v.reshape(1, -1)
    w_mix = w_in[0][:, :MIX_COLS].astype(BF16)
    w_gate = w_in[0][:, MIX_COLS:].astype(BF16)
    w_pa_b, w_pb_b, w_out_b = w_pa[0].astype(BF16), w_pb[0].astype(BF16), w_out[0].astype(BF16)
    w_ffn_in_b, w_ffn_out_b = w_ffn_in[0].astype(BF16), w_ffn_out[0].astype(BF16)

    c_all = jnp.concatenate([c_prompt, jnp.zeros((SUBLANES - batch, D_MODEL), F32), c_sample], axis=0)
    mod_p, mod_s = _ada(c_all, w_ada[0], row(b_ada[0]), n_sample)
    mod_p = mod_p.reshape(N_MOD, SUBLANES, 1, D_MODEL)

    xp = x_prompt.reshape(batch * seq, D_MODEL)
    xs = x_sample.transpose(1, 0, 2).reshape(steps * n_sample, D_MODEL)
    state = state_conv[0].transpose(1, 0, 2)

    tm = 256
    tiles_per_seq = seq // tm
    bias_p = _lane_expand(b_sg[0][:, :CHUNK].T)
    ya_p, yb_p, zst_p, vst_p = _mix_prompt(xp, mod_p, row(g_mix[0]), w_mix, w_conv[0], row(g_v[0]), w_sg[0],
                                           bias_p, seq, tm)
    w_sg_s = _lane_expand(w_sg[0][:, :steps, :steps].transpose(1, 2, 0).reshape(steps * steps, SG_GROUPS))
    bias_s = _lane_expand(b_sg[0][:, :steps].T)
    ya_s, yb_s, zst_s, vst_s = _mix_sample(xs, mod_s, row(g_mix[0]), w_mix, w_conv[0], row(g_v[0]), w_sg_s,
                                           bias_s, state, steps)

    prompt_spec = lambda comp: _prompt_mod_spec(comp, tiles_per_seq)
    sample_spec = lambda comp: _sample_mod_spec(comp, n_sample)
    merge_w = (row(g_mix[0]), row(g_ffn[0]), w_gate, w_pa_b, w_pb_b, w_out_b)
    x1_p, h2_p = _merge(xp, ya_p, yb_p, mod_p, prompt_spec, *merge_w, tm)
    x1_s, h2_s = _merge(xs, ya_s, yb_s, mod_s, sample_spec, *merge_w, tm)

    act_p = _ffn_in(h2_p, w_ffn_in_b, 1024, 512)
    act_s = _ffn_in(h2_s, w_ffn_in_b, 512, 512)

    y_p = _ffn_out(act_p, x1_p, mod_p, prompt_spec, row(g_final), w_ffn_out_b, tm)
    y_s = _ffn_out(act_s, x1_s, mod_s, sample_spec, row(g_final), w_ffn_out_b, tm)

    y_prompt = y_p.reshape(batch, seq, D_MODEL)
    y_sample = y_s.reshape(steps, n_sample, D_MODEL).transpose(1, 0, 2)
    conv_prompt = zst_p[:, SUBLANES - (CONV_WIDTH - 1):, :][None]
    conv_sample = zst_s.transpose(1, 0, 2)[None]
    sgv_prompt = vst_p.reshape(1, batch, CHUNK, SG_GROUPS, SG_HEAD)
    sgv_sample = vst_s.reshape(steps, n_sample, SG_GROUPS, SG_HEAD).transpose(1, 0, 2, 3)[None]
    return (y_prompt, y_sample, conv_prompt, conv_sample, sgv_prompt, sgv_sample)
```

```python
import functools

import jax
import jax.numpy as jnp
from jax import lax
from jax.experimental import pallas as pl
from jax.experimental.pallas import tpu as pltpu

D_MODEL = 2048
CONV_DIM = D_MODEL // 2
CONV_WIDTH = 3
SG_DIM = D_MODEL // 2
SG_GROUPS = 8
SG_HEAD = SG_DIM // SG_GROUPS
CHUNK = 128
D_FF = 5632
N_MOD = 6
EPS = 1e-6
SEG = 1024
MIX_COLS = 3 * CONV_DIM + 2 * SG_DIM
SUBLANES = 8
VMEM_LIMIT_BYTES = 60 * 1024 * 1024

F32 = jnp.float32
BF16 = jnp.bfloat16


def _dot(a, b):
    return jnp.dot(a, b, preferred_element_type=F32)


def _rms(x, gain):
    return x * lax.rsqrt(jnp.mean(x * x, axis=-1, keepdims=True) + EPS) * gain


def _modulated_norm(x, gain, scale, shift):
    return _rms(x, gain) * (1 + scale) + shift


def _rows(*parts):
    return jnp.concatenate(parts, axis=0)


def _load_bf16(w_hbm, w_vmem, stage_ref, sem_ref, *, col0=0):
    k, n = w_vmem.shape
    rows = stage_ref.shape[1]
    n_chunks = k // rows

    def chunk_copy(c, slot):
        src = w_hbm.at[pl.ds(c * rows, rows), pl.ds(col0, n)]
        return pltpu.make_async_copy(src, stage_ref.at[slot], sem_ref.at[slot])

    chunk_copy(0, 0).start()

    def body(c, carry):
        slot = c % 2

        @pl.when(c + 1 < n_chunks)
        def _():
            chunk_copy(c + 1, 1 - slot).start()

        chunk_copy(c, slot).wait()
        w_vmem[pl.ds(pl.multiple_of(c * rows, rows), rows), :] = stage_ref[slot].astype(BF16)
        return carry

    lax.fori_loop(0, n_chunks, body, 0)


def _ada_kernel(c_ref, w_ref, b_ref, mp_ref, ms_ref):
    a = jax.nn.silu(c_ref[...]).astype(BF16)
    r = _dot(a, w_ref[...].astype(BF16)) + b_ref[...]
    mp_ref[...] = r[:SUBLANES]
    ms_ref[...] = r[SUBLANES:]


def _ada(c_all, w_ada, b_ada, n_sample):
    tn = 1024
    per = D_MODEL // tn
    rows = c_all.shape[0]
    return pl.pallas_call(
        _ada_kernel,
        grid=(N_MOD * per,),
        in_specs=[
            pl.BlockSpec((rows, D_MODEL), lambda j: (0, 0)),
            pl.BlockSpec((D_MODEL, tn), lambda j: (0, j)),
            pl.BlockSpec((1, tn), lambda j: (0, j)),
        ],
        out_specs=[
            pl.BlockSpec((None, SUBLANES, tn), lambda j: (j // per, 0, j % per)),
            pl.BlockSpec((None, n_sample, tn), lambda j: (j // per, 0, j % per)),
        ],
        out_shape=[
            jax.ShapeDtypeStruct((N_MOD, SUBLANES, D_MODEL), F32),
            jax.ShapeDtypeStruct((N_MOD, n_sample, D_MODEL), F32),
        ],
        compiler_params=pltpu.CompilerParams(
            dimension_semantics=("arbitrary",), vmem_limit_bytes=VMEM_LIMIT_BYTES),
        name="ada",
    )(c_all, w_ada, b_ada)


class _Tiling:
    def __init__(self, rows_p, rows_s, seq, n_sample, tp):
        self.tp = tp
        self.steps = rows_p // tp
        self.ts = rows_s // self.steps
        self.tiles_per_seq = seq // tp
        self.sample_blocks = n_sample // self.ts
        assert rows_p % tp == 0 and seq % tp == 0 and rows_s % self.steps == 0
        assert self.ts % 16 == 0 and n_sample % self.ts == 0

    def prompt(self, width):
        return pl.BlockSpec((self.tp, width), lambda i: (i, 0))

    def sample(self, width):
        return pl.BlockSpec((self.ts, width), lambda i: (i, 0))

    def prompt_mod(self, comp):
        return pl.BlockSpec((None, None, 1, D_MODEL), lambda i: (comp, i // self.tiles_per_seq, 0, 0))

    def sample_mod(self, comp):
        return pl.BlockSpec((None, self.ts, D_MODEL), lambda i: (comp, i % self.sample_blocks, 0))

    def mods(self, *comps):
        return [spec(c) for c in comps for spec in (self.prompt_mod, self.sample_mod)]


def _resident(shape):
    return pl.BlockSpec(shape, lambda i: (0,) * len(shape), pipeline_mode=pl.Buffered(1))


_HBM = pl.BlockSpec(memory_space=pl.ANY)


def _params():
    return pltpu.CompilerParams(dimension_semantics=("arbitrary",), vmem_limit_bytes=VMEM_LIMIT_BYTES)


def _mix_kernel(xp_ref, xs_ref, scp_ref, scs_ref, shp_ref, shs_ref, gmix_ref, w_hbm, wconv_ref, gv_ref,
                wsg_ref, bsgp_ref, wsgs_ref, bsgs_ref, state_ref,
                yap_ref, yas_ref, ybp_ref, ybs_ref, zstp_ref, vstp_ref, zs_ref, vs_ref,
                w_ref, stage_ref, sem_ref, carry_ref, zhist_ref, vhist_ref, *, tiles_per_seq, n_sample, steps):
    i = pl.program_id(0)
    tp = xp_ref.shape[0]
    ts = xs_ref.shape[0]

    @pl.when(i == 0)
    def _():
        _load_bf16(w_hbm, w_ref, stage_ref, sem_ref)
        zhist_ref[0:(CONV_WIDTH - 1) * n_sample] = state_ref[...]
        vhist_ref[...] = jnp.zeros_like(vhist_ref)

    gmix = gmix_ref[...]
    h = _rows(_modulated_norm(xp_ref[...], gmix, scp_ref[...], shp_ref[...]),
              _modulated_norm(xs_ref[...], gmix, scs_ref[...], shs_ref[...])).astype(BF16)
    proj = lambda k: _dot(h, w_ref[:, k * SEG:(k + 1) * SEG])
    wc = wconv_ref[...]

    @pl.when(i % tiles_per_seq == 0)
    def _():
        carry_ref[...] = jnp.zeros_like(carry_ref)

    z_all = proj(1) * proj(2)
    b_gate = proj(0)
    z = z_all[:tp]
    carry = carry_ref[...]
    prev2, prev1 = carry[SUBLANES - 2:SUBLANES - 1], carry[SUBLANES - 1:SUBLANES]
    row = lax.broadcasted_iota(jnp.int32, (SUBLANES, CONV_DIM), 0)
    z1 = pltpu.roll(z, 1, 0)
    z2 = pltpu.roll(z, 2, 0)
    z1 = _rows(jnp.where(row == 0, prev1, z1[:SUBLANES]), z1[SUBLANES:])
    z2 = _rows(jnp.where(row == 0, prev2, jnp.where(row == 1, prev1, z2[:SUBLANES])), z2[SUBLANES:])
    conv = wc[0:1] * z2 + wc[1:2] * z1 + wc[2:3] * z
    yap_ref[...] = (b_gate[:tp] * conv).astype(BF16)
    carry_ref[...] = z[tp - SUBLANES:]
    zstp_ref[...] = z[tp - SUBLANES:]

    base = pl.multiple_of(i * ts, ts)
    hist = lambda ref, k: ref[pl.ds(base + k * n_sample, ts), :]
    zs = z_all[tp:]
    zhist_ref[pl.ds(base + (CONV_WIDTH - 1) * n_sample, ts), :] = zs
    conv_s = wc[0:1] * hist(zhist_ref, 0) + wc[1:2] * hist(zhist_ref, 1) + wc[2:3] * zs
    yas_ref[...] = (b_gate[tp:] * conv_s).astype(BF16)
    zs_ref[...] = zs

    gu = jax.nn.gelu(proj(3))
    vn = _rms(jax.nn.gelu(proj(4)), gv_ref[...])
    vstp_ref[...] = vn[tp - CHUNK:tp]
    vb = vn[:tp].astype(BF16)
    causal = (lax.broadcasted_iota(jnp.int32, (CHUNK, CHUNK), 0)
              >= lax.broadcasted_iota(jnp.int32, (CHUNK, CHUNK), 1))
    wgs = [jnp.where(causal, wsg_ref[g], 0.0).astype(BF16) for g in range(SG_GROUPS)]
    bias = bsgp_ref[...]
    chunks = []
    for c in range(tp // CHUNK):
        rows = slice(c * CHUNK, (c + 1) * CHUNK)
        parts = [_dot(wgs[g], vb[rows, g * SG_HEAD:(g + 1) * SG_HEAD]) for g in range(SG_GROUPS)]
        chunks.append(jnp.concatenate(parts, axis=1) + bias)
    ybp_ref[...] = (gu[:tp] * _rows(*chunks)).astype(BF16)

    vs = vn[tp:]
    vs_ref[...] = vs
    block = i % (n_sample // ts)
    t = i // (n_sample // ts)
    vhist_ref[pl.ds(base, ts), :] = vs
    sp = bsgs_ref[t]
    for s in range(steps):
        w_ts = jnp.where(s <= t, wsgs_ref[t * steps + s], 0.0)
        sp = sp + w_ts * vhist_ref[pl.ds(pl.multiple_of(block * ts, ts) + s * n_sample, ts), :]
    ybs_ref[...] = (gu[tp:] * sp).astype(BF16)


def _mix(xp, xs, mod_p, mod_s, g_mix, w_in, w_conv, g_v, w_sg, bias_p, w_sg_s, bias_s, state, til, n_sample,
         steps):
    rows_p, rows_s = xp.shape[0], xs.shape[0]
    n_seq = rows_p // (til.tiles_per_seq * til.tp)
    per_seq = lambda rows, width: pl.BlockSpec((None, rows, width), lambda i: (i // til.tiles_per_seq, 0, 0))
    stage_rows = 128
    return pl.pallas_call(
        functools.partial(_mix_kernel, tiles_per_seq=til.tiles_per_seq, n_sample=n_sample, steps=steps),
        grid=(til.steps,),
        in_specs=[
            til.prompt(D_MODEL), til.sample(D_MODEL), *til.mods(1, 0),
            _resident((1, D_MODEL)),
            _HBM,
            _resident((CONV_WIDTH, CONV_DIM)),
            _resident((1, SG_DIM)),
            _resident((SG_GROUPS, CHUNK, CHUNK)),
            _resident((CHUNK, SG_DIM)),
            _resident((steps * steps, 1, SG_DIM)),
            _resident((steps, 1, SG_DIM)),
            _resident(((CONV_WIDTH - 1) * n_sample, CONV_DIM)),
        ],
        out_specs=[
            til.prompt(CONV_DIM), til.sample(CONV_DIM), til.prompt(SG_DIM), til.sample(SG_DIM),
            per_seq(SUBLANES, CONV_DIM), per_seq(CHUNK, SG_DIM),
            til.sample(CONV_DIM), til.sample(SG_DIM),
        ],
        out_shape=[
            jax.ShapeDtypeStruct((rows_p, CONV_DIM), BF16), jax.ShapeDtypeStruct((rows_s, CONV_DIM), BF16),
            jax.ShapeDtypeStruct((rows_p, SG_DIM), BF16), jax.ShapeDtypeStruct((rows_s, SG_DIM), BF16),
            jax.ShapeDtypeStruct((n_seq, SUBLANES, CONV_DIM), F32),
            jax.ShapeDtypeStruct((n_seq, CHUNK, SG_DIM), F32),
            jax.ShapeDtypeStruct((rows_s, CONV_DIM), F32), jax.ShapeDtypeStruct((rows_s, SG_DIM), F32),
        ],
        scratch_shapes=[
            pltpu.VMEM((D_MODEL, MIX_COLS), BF16),
            pltpu.VMEM((2, stage_rows, MIX_COLS), F32),
            pltpu.SemaphoreType.DMA((2,)),
            pltpu.VMEM((SUBLANES, CONV_DIM), F32),
            pltpu.VMEM(((CONV_WIDTH - 1) * n_sample + rows_s, CONV_DIM), F32),
            pltpu.VMEM((rows_s, SG_DIM), F32),
        ],
        compiler_params=_params(),
        name="mix",
    )(xp, xs, mod_p, mod_s, mod_p, mod_s, g_mix, w_in, w_conv, g_v, w_sg, bias_p, w_sg_s, bias_s, state)


def _merge_kernel(xp_ref, xs_ref, yap_ref, yas_ref, ybp_ref, ybs_ref,
                  scmp_ref, scms_ref, shmp_ref, shms_ref, gtmp_ref, gtms_ref,
                  scfp_ref, scfs_ref, shfp_ref, shfs_ref, gmix_ref, gffn_ref,
                  win_hbm, wpa_hbm, wpb_hbm, wout_hbm,
                  x1p_ref, x1s_ref, h2p_ref, h2s_ref,
                  wga_ref, wgb_ref, wpa_ref, wpb_ref, wout_ref, stage_ref, sem_ref):
    tp = xp_ref.shape[0]

    @pl.when(pl.program_id(0) == 0)
    def _():
        _load_bf16(win_hbm, wga_ref, stage_ref, sem_ref, col0=MIX_COLS)
        _load_bf16(win_hbm, wgb_ref, stage_ref, sem_ref, col0=MIX_COLS + D_MODEL)
        _load_bf16(wpa_hbm, wpa_ref, stage_ref, sem_ref)
        _load_bf16(wpb_hbm, wpb_ref, stage_ref, sem_ref)
        _load_bf16(wout_hbm, wout_ref, stage_ref, sem_ref)

    xp = xp_ref[...]
    xs = xs_ref[...]
    gmix = gmix_ref[...]
    h = _rows(_modulated_norm(xp, gmix, scmp_ref[...], shmp_ref[...]),
              _modulated_norm(xs, gmix, scms_ref[...], shms_ref[...])).astype(BF16)
    ya = _rows(yap_ref[...], yas_ref[...])
    yb = _rows(ybp_ref[...], ybs_ref[...])
    halves = []
    for k in range(D_MODEL // SEG):
        cols = slice(k * SEG, (k + 1) * SEG)
        branch_a = jax.nn.sigmoid(_dot(h, wga_ref[:, cols])) * _dot(ya, wpa_ref[:, cols])
        branch_b = jax.nn.sigmoid(_dot(h, wgb_ref[:, cols])) * _dot(yb, wpb_ref[:, cols])
        halves.append((branch_a + branch_b).astype(BF16))
    out = _dot(jnp.concatenate(halves, axis=1), wout_ref[...])
    x1p = xp + gtmp_ref[...] * out[:tp]
    x1s = xs + gtms_ref[...] * out[tp:]
    x1p_ref[...] = x1p
    x1s_ref[...] = x1s
    gffn = gffn_ref[...]
    h2p_ref[...] = _modulated_norm(x1p, gffn, scfp_ref[...], shfp_ref[...]).astype(BF16)
    h2s_ref[...] = _modulated_norm(x1s, gffn, scfs_ref[...], shfs_ref[...]).astype(BF16)


def _merge(xp, xs, ya_p, ya_s, yb_p, yb_s, mod_p, mod_s, g_mix, g_ffn, w_in, w_pa, w_pb, w_out, til):
    rows_p, rows_s = xp.shape[0], xs.shape[0]
    stage_rows = 256
    return pl.pallas_call(
        _merge_kernel,
        grid=(til.steps,),
        in_specs=[
            til.prompt(D_MODEL), til.sample(D_MODEL),
            til.prompt(CONV_DIM), til.sample(CONV_DIM), til.prompt(SG_DIM), til.sample(SG_DIM),
            *til.mods(1, 0, 2, 4, 3),
            _resident((1, D_MODEL)), _resident((1, D_MODEL)),
            _HBM, _HBM, _HBM, _HBM,
        ],
        out_specs=[til.prompt(D_MODEL), til.sample(D_MODEL), til.prompt(D_MODEL), til.sample(D_MODEL)],
        out_shape=[
            jax.ShapeDtypeStruct((rows_p, D_MODEL), F32), jax.ShapeDtypeStruct((rows_s, D_MODEL), F32),
            jax.ShapeDtypeStruct((rows_p, D_MODEL), BF16), jax.ShapeDtypeStruct((rows_s, D_MODEL), BF16),
        ],
        scratch_shapes=[
            pltpu.VMEM((D_MODEL, D_MODEL), BF16), pltpu.VMEM((D_MODEL, D_MODEL), BF16),
            pltpu.VMEM((CONV_DIM, D_MODEL), BF16), pltpu.VMEM((SG_DIM, D_MODEL), BF16),
            pltpu.VMEM((D_MODEL, D_MODEL), BF16),
            pltpu.VMEM((2, stage_rows, D_MODEL), F32),
            pltpu.SemaphoreType.DMA((2,)),
        ],
        compiler_params=_params(),
        name="merge",
    )(xp, xs, ya_p, ya_s, yb_p, yb_s, *([mod_p, mod_s] * 5), g_mix, g_ffn, w_in, w_pa, w_pb, w_out)


def _ffn_in_kernel(hp_ref, hs_ref, wg_ref, wu_ref, op_ref, os_ref):
    tp = hp_ref.shape[0]
    h = _rows(hp_ref[...], hs_ref[...])
    act = jax.nn.silu(_dot(h, wg_ref[...].astype(BF16))) * _dot(h, wu_ref[...].astype(BF16))
    op_ref[...] = act[:tp].astype(BF16)
    os_ref[...] = act[tp:].astype(BF16)


def _ffn_in(h2_p, h2_s, w_ffn_in, tp, tn):
    rows_p, rows_s = h2_p.shape[0], h2_s.shape[0]
    steps = rows_p // tp
    ts = rows_s // steps
    n_blocks = D_FF // tn
    assert rows_p % tp == 0 and rows_s % steps == 0 and ts % 16 == 0 and D_FF % tn == 0
    return pl.pallas_call(
        _ffn_in_kernel,
        grid=(n_blocks, steps),
        in_specs=[
            pl.BlockSpec((tp, D_MODEL), lambda j, i: (i, 0)),
            pl.BlockSpec((ts, D_MODEL), lambda j, i: (i, 0)),
            pl.BlockSpec((D_MODEL, tn), lambda j, i: (0, j)),
            pl.BlockSpec((D_MODEL, tn), lambda j, i: (0, j + n_blocks)),
        ],
        out_specs=[
            pl.BlockSpec((tp, tn), lambda j, i: (i, j)),
            pl.BlockSpec((ts, tn), lambda j, i: (i, j)),
        ],
        out_shape=[
            jax.ShapeDtypeStruct((rows_p, D_FF), BF16),
            jax.ShapeDtypeStruct((rows_s, D_FF), BF16),
        ],
        compiler_params=pltpu.CompilerParams(
            dimension_semantics=("arbitrary", "arbitrary"), vmem_limit_bytes=VMEM_LIMIT_BYTES),
        name="ffn_in",
    )(h2_p, h2_s, w_ffn_in, w_ffn_in)


def _ffn_out_kernel(ap_ref, as_ref, x1p_ref, x1s_ref, gtp_ref, gts_ref, gfin_ref, w_hbm, yp_ref, ys_ref,
                    w_ref, stage_ref, sem_ref):
    tp = ap_ref.shape[0]

    @pl.when(pl.program_id(0) == 0)
    def _():
        _load_bf16(w_hbm, w_ref, stage_ref, sem_ref)

    out = _dot(_rows(ap_ref[...], as_ref[...]), w_ref[...])
    gfin = gfin_ref[...]
    yp_ref[...] = _rms(x1p_ref[...] + gtp_ref[...] * out[:tp], gfin)
    ys_ref[...] = _rms(x1s_ref[...] + gts_ref[...] * out[tp:], gfin)


def _ffn_out(act_p, act_s, x1_p, x1_s, mod_p, mod_s, g_final, w_ffn_out, til):
    rows_p, rows_s = x1_p.shape[0], x1_s.shape[0]
    stage_rows = 256
    return pl.pallas_call(
        _ffn_out_kernel,
        grid=(til.steps,),
        in_specs=[
            til.prompt(D_FF), til.sample(D_FF), til.prompt(D_MODEL), til.sample(D_MODEL),
            *til.mods(5),
            _resident((1, D_MODEL)),
            _HBM,
        ],
        out_specs=[til.prompt(D_MODEL), til.sample(D_MODEL)],
        out_shape=[
            jax.ShapeDtypeStruct((rows_p, D_MODEL), F32),
            jax.ShapeDtypeStruct((rows_s, D_MODEL), F32),
        ],
        scratch_shapes=[
            pltpu.VMEM((D_FF, D_MODEL), BF16),
            pltpu.VMEM((2, stage_rows, D_MODEL), F32),
            pltpu.SemaphoreType.DMA((2,)),
        ],
        compiler_params=_params(),
        name="ffn_out",
    )(act_p, act_s, x1_p, x1_s, mod_p, mod_s, g_final, w_ffn_out)


def _lane_expand(per_group):
    return jnp.repeat(per_group, SG_HEAD, axis=1)


def kernel(x_prompt, x_sample, state_conv, c_prompt, c_sample, g_mix, g_ffn, w_ada, b_ada, w_in, w_conv, g_v,
           w_sg, b_sg, w_pa, w_pb, w_out, w_ffn_in, w_ffn_out, g_final):
    assert g_mix.shape[0] == 1, "one layer"
    batch, seq, _ = x_prompt.shape
    n_sample, steps, _ = x_sample.shape
    assert batch <= SUBLANES and seq % CHUNK == 0
    row = lambda v: v.reshape(1, -1)

    c_all = jnp.concatenate([c_prompt, jnp.zeros((SUBLANES - batch, D_MODEL), F32), c_sample], axis=0)
    mod_p, mod_s = _ada(c_all, w_ada[0], row(b_ada[0]), n_sample)
    mod_p = mod_p.reshape(N_MOD, SUBLANES, 1, D_MODEL)

    xp = x_prompt.reshape(batch * seq, D_MODEL)
    xs = x_sample.transpose(1, 0, 2).reshape(steps * n_sample, D_MODEL)
    state = state_conv[0].transpose(1, 0, 2).reshape((CONV_WIDTH - 1) * n_sample, CONV_DIM)
    til = _Tiling(batch * seq, steps * n_sample, seq, n_sample, tp=256)

    bias_p = _lane_expand(b_sg[0][:, :CHUNK].T)
    w_sg_s = _lane_expand(w_sg[0][:, :steps, :steps].transpose(1, 2, 0).reshape(steps * steps, SG_GROUPS))
    w_sg_s = w_sg_s.reshape(steps * steps, 1, SG_DIM)
    bias_s = _lane_expand(b_sg[0][:, :steps].T).reshape(steps, 1, SG_DIM)
    ya_p, ya_s, yb_p, yb_s, zst_p, vst_p, z_s, vn_s = _mix(
        xp, xs, mod_p, mod_s, row(g_mix[0]), w_in[0], w_conv[0], row(g_v[0]), w_sg[0], bias_p, w_sg_s, bias_s,
        state, til, n_sample, steps)

    x1_p, x1_s, h2_p, h2_s = _merge(xp, xs, ya_p, ya_s, yb_p, yb_s, mod_p, mod_s, row(g_mix[0]), row(g_ffn[0]),
                                    w_in[0], w_pa[0], w_pb[0], w_out[0], til)
    act_p, act_s = _ffn_in(h2_p, h2_s, w_ffn_in[0], tp=1024, tn=512)
    y_p, y_s = _ffn_out(act_p, act_s, x1_p, x1_s, mod_p, mod_s, row(g_final), w_ffn_out[0], til)

    by_batch = lambda a: a.reshape(steps, n_sample, *a.shape[1:]).swapaxes(0, 1)
    y_prompt = y_p.reshape(batch, seq, D_MODEL)
    y_sample = by_batch(y_s)
    conv_prompt = zst_p[:, SUBLANES - (CONV_WIDTH - 1):, :][None]
    conv_sample = by_batch(z_s)[:, steps - (CONV_WIDTH - 1):, :][None]
    sgv_prompt = vst_p.reshape(1, batch, CHUNK, SG_GROUPS, SG_HEAD)
    sgv_sample = by_batch(vn_s).reshape(1, n_sample, steps, SG_GROUPS, SG_HEAD)
    return (y_prompt, y_sample, conv_prompt, conv_sample, sgv_prompt, sgv_sample)
```

```python
import functools

import jax
import jax.numpy as jnp
from jax import lax
from jax.experimental import pallas as pl
from jax.experimental.pallas import tpu as pltpu

D_MODEL = 2048
CONV_DIM = D_MODEL // 2
CONV_WIDTH = 3
SG_DIM = D_MODEL // 2
SG_GROUPS = 8
SG_HEAD = SG_DIM // SG_GROUPS
CHUNK = 128
D_FF = 5632
N_MOD = 6
EPS = 1e-6
SEG = 1024
MIX_COLS = 3 * CONV_DIM + 2 * SG_DIM
SUBLANES = 8
VMEM_LIMIT_BYTES = 60 * 1024 * 1024

F32 = jnp.float32
BF16 = jnp.bfloat16


def _dot(a, b):
    return jnp.dot(a, b, preferred_element_type=F32)


def _rms(x, gain):
    return x * lax.rsqrt(jnp.mean(x * x, axis=-1, keepdims=True) + EPS) * gain


def _modulated_norm(x, gain, scale, shift):
    return _rms(x, gain) * (1 + scale) + shift


def _rows(*parts):
    return jnp.concatenate(parts, axis=0)


def _load_bf16(w_hbm, w_vmem, stage_ref, sem_ref, *, col0=0):
    k, n = w_vmem.shape
    rows = stage_ref.shape[1]
    n_chunks = k // rows

    def chunk_copy(c, slot):
        src = w_hbm.at[pl.ds(c * rows, rows), pl.ds(col0, n)]
        return pltpu.make_async_copy(src, stage_ref.at[slot], sem_ref.at[slot])

    chunk_copy(0, 0).start()

    def body(c, carry):
        slot = c % 2

        @pl.when(c + 1 < n_chunks)
        def _():
            chunk_copy(c + 1, 1 - slot).start()

        chunk_copy(c, slot).wait()
        w_vmem[pl.ds(pl.multiple_of(c * rows, rows), rows), :] = stage_ref[slot].astype(BF16)
        return carry

    lax.fori_loop(0, n_chunks, body, 0)


def _ada_kernel(c_ref, w_ref, b_ref, mp_ref, ms_ref):
    a = jax.nn.silu(c_ref[...]).astype(BF16)
    r = _dot(a, w_ref[...].astype(BF16)) + b_ref[...]
    mp_ref[...] = r[:SUBLANES]
    ms_ref[...] = r[SUBLANES:]


def _ada(c_all, w_ada, b_ada, n_sample):
    tn = 1024
    per = D_MODEL // tn
    rows = c_all.shape[0]
    return pl.pallas_call(
        _ada_kernel,
        grid=(N_MOD * per,),
        in_specs=[
            pl.BlockSpec((rows, D_MODEL), lambda j: (0, 0)),
            pl.BlockSpec((D_MODEL, tn), lambda j: (0, j)),
            pl.BlockSpec((1, tn), lambda j: (0, j)),
        ],
        out_specs=[
            pl.BlockSpec((None, SUBLANES, tn), lambda j: (j // per, 0, j % per)),
            pl.BlockSpec((None, n_sample, tn), lambda j: (j // per, 0, j % per)),
        ],
        out_shape=[
            jax.ShapeDtypeStruct((N_MOD, SUBLANES, D_MODEL), F32),
            jax.ShapeDtypeStruct((N_MOD, n_sample, D_MODEL), F32),
        ],
        compiler_params=pltpu.CompilerParams(
            dimension_semantics=("arbitrary",), vmem_limit_bytes=VMEM_LIMIT_BYTES),
        name="ada",
    )(c_all, w_ada, b_ada)


class _Tiling:
    def __init__(self, rows_p, rows_s, seq, n_sample, tp):
        self.tp = tp
        self.steps = rows_p // tp
        self.ts = rows_s // self.steps
        self.tiles_per_seq = seq // tp
        self.sample_blocks = n_sample // self.ts
        assert rows_p % tp == 0 and seq % tp == 0 and rows_s % self.steps == 0
        assert self.ts % 16 == 0 and n_sample % self.ts == 0

    def prompt(self, width):
        return pl.BlockSpec((self.tp, width), lambda i: (i, 0))

    def sample(self, width):
        return pl.BlockSpec((self.ts, width), lambda i: (i, 0))

    def sample_by_batch(self, width):
        return pl.BlockSpec((self.ts, width), lambda i: (i % self.sample_blocks, i // self.sample_blocks))

    def prompt_mod(self, comp):
        return pl.BlockSpec((None, None, 1, D_MODEL), lambda i: (comp, i // self.tiles_per_seq, 0, 0))

    def sample_mod(self, comp):
        return pl.BlockSpec((None, self.ts, D_MODEL), lambda i: (comp, i % self.sample_blocks, 0))

    def mods(self, *comps):
        return [spec(c) for c in comps for spec in (self.prompt_mod, self.sample_mod)]


def _resident(shape):
    return pl.BlockSpec(shape, lambda i: (0,) * len(shape), pipeline_mode=pl.Buffered(1))


_HBM = pl.BlockSpec(memory_space=pl.ANY)


def _params():
    return pltpu.CompilerParams(dimension_semantics=("arbitrary",), vmem_limit_bytes=VMEM_LIMIT_BYTES)


def _mix_kernel(xp_ref, xs_ref, scp_ref, scs_ref, shp_ref, shs_ref, gmix_ref, w_hbm, wconv_ref, gv_ref,
                wsg_ref, bsgp_ref, wsgs_ref, bsgs_ref, state_ref,
                hp_ref, hs_ref, yap_ref, yas_ref, ybp_ref, ybs_ref, zstp_ref, vstp_ref, zs_ref, vs_ref,
                w_ref, stage_ref, sem_ref, carry_ref, zhist_ref, vhist_ref, *, tiles_per_seq, n_sample, steps):
    i = pl.program_id(0)
    tp = xp_ref.shape[0]
    ts = xs_ref.shape[0]

    @pl.when(i == 0)
    def _():
        _load_bf16(w_hbm, w_ref, stage_ref, sem_ref)
        for k in range(CONV_WIDTH - 1):
            zhist_ref[k * n_sample:(k + 1) * n_sample] = state_ref[:, k * CONV_DIM:(k + 1) * CONV_DIM]
        vhist_ref[...] = jnp.zeros_like(vhist_ref)

    @pl.when(i % tiles_per_seq == 0)
    def _():
        carry_ref[...] = jnp.zeros_like(carry_ref)

    gmix = gmix_ref[...]
    h = _rows(_modulated_norm(xp_ref[...], gmix, scp_ref[...], shp_ref[...]),
              _modulated_norm(xs_ref[...], gmix, scs_ref[...], shs_ref[...])).astype(BF16)
    hp_ref[...] = h[:tp]
    hs_ref[...] = h[tp:]
    proj = lambda k: _dot(h, w_ref[:, k * SEG:(k + 1) * SEG])
    base = pl.multiple_of(i * ts, ts)

    vn = _rms(jax.nn.gelu(proj(4)), gv_ref[...])
    vstp_ref[...] = vn[tp - CHUNK:tp]
    vb = vn[:tp].astype(BF16)
    causal = (lax.broadcasted_iota(jnp.int32, (CHUNK, CHUNK), 0)
              >= lax.broadcasted_iota(jnp.int32, (CHUNK, CHUNK), 1))
    wgs = [jnp.where(causal, wsg_ref[g], 0.0).astype(BF16) for g in range(SG_GROUPS)]
    bias = bsgp_ref[...]
    chunks = []
    for c in range(tp // CHUNK):
        rows = slice(c * CHUNK, (c + 1) * CHUNK)
        parts = [_dot(wgs[g], vb[rows, g * SG_HEAD:(g + 1) * SG_HEAD]) for g in range(SG_GROUPS)]
        chunks.append(jnp.concatenate(parts, axis=1) + bias)
    gu = jax.nn.gelu(proj(3))
    ybp_ref[...] = (gu[:tp] * _rows(*chunks)).astype(BF16)

    vs = vn[tp:]
    vs_ref[...] = vs
    block = i % (n_sample // ts)
    t = i // (n_sample // ts)
    vhist_ref[pl.ds(base, ts), :] = vs
    sp = bsgs_ref[t]
    for s in range(steps):
        w_ts = jnp.where(s <= t, wsgs_ref[t * steps + s], 0.0)
        sp = sp + w_ts * vhist_ref[pl.ds(pl.multiple_of(block * ts, ts) + s * n_sample, ts), :]
    ybs_ref[...] = (gu[tp:] * sp).astype(BF16)

    wc = wconv_ref[...]
    z_all = proj(1) * proj(2)
    b_gate = proj(0)
    z = z_all[:tp]
    carry = carry_ref[...]
    prev2, prev1 = carry[SUBLANES - 2:SUBLANES - 1], carry[SUBLANES - 1:SUBLANES]
    row = lax.broadcasted_iota(jnp.int32, (SUBLANES, CONV_DIM), 0)
    z1 = pltpu.roll(z, 1, 0)
    z2 = pltpu.roll(z, 2, 0)
    z1 = _rows(jnp.where(row == 0, prev1, z1[:SUBLANES]), z1[SUBLANES:])
    z2 = _rows(jnp.where(row == 0, prev2, jnp.where(row == 1, prev1, z2[:SUBLANES])), z2[SUBLANES:])
    conv = wc[0:1] * z2 + wc[1:2] * z1 + wc[2:3] * z
    yap_ref[...] = (b_gate[:tp] * conv).astype(BF16)
    carry_ref[...] = z[tp - SUBLANES:]
    zstp_ref[...] = z[tp - SUBLANES:]

    hist = lambda ref, k: ref[pl.ds(base + k * n_sample, ts), :]
    zs = z_all[tp:]
    zhist_ref[pl.ds(base + (CONV_WIDTH - 1) * n_sample, ts), :] = zs
    conv_s = wc[0:1] * hist(zhist_ref, 0) + wc[1:2] * hist(zhist_ref, 1) + wc[2:3] * zs
    yas_ref[...] = (b_gate[tp:] * conv_s).astype(BF16)
    zs_ref[...] = zs


def _mix(xp, xs, mod_p, mod_s, g_mix, w_in, w_conv, g_v, w_sg, bias_p, w_sg_s, bias_s, state, til, n_sample,
         steps):
    rows_p = xp.shape[0]
    rows_s = n_sample * steps
    n_seq = rows_p // (til.tiles_per_seq * til.tp)
    per_seq = lambda rows, width: pl.BlockSpec((None, rows, width), lambda i: (i // til.tiles_per_seq, 0, 0))
    stage_rows = 128
    return pl.pallas_call(
        functools.partial(_mix_kernel, tiles_per_seq=til.tiles_per_seq, n_sample=n_sample, steps=steps),
        grid=(til.steps,),
        in_specs=[
            til.prompt(D_MODEL), til.sample_by_batch(D_MODEL), *til.mods(1, 0),
            _resident((1, D_MODEL)),
            _HBM,
            _resident((CONV_WIDTH, CONV_DIM)),
            _resident((1, SG_DIM)),
            _resident((SG_GROUPS, CHUNK, CHUNK)),
            _resident((CHUNK, SG_DIM)),
            _resident((steps * steps, 1, SG_DIM)),
            _resident((steps, 1, SG_DIM)),
            _resident((n_sample, (CONV_WIDTH - 1) * CONV_DIM)),
        ],
        out_specs=[
            til.prompt(D_MODEL), til.sample(D_MODEL),
            til.prompt(CONV_DIM), til.sample(CONV_DIM), til.prompt(SG_DIM), til.sample(SG_DIM),
            per_seq(SUBLANES, CONV_DIM), per_seq(CHUNK, SG_DIM),
            til.sample_by_batch(CONV_DIM), til.sample_by_batch(SG_DIM),
        ],
        out_shape=[
            jax.ShapeDtypeStruct((rows_p, D_MODEL), BF16), jax.ShapeDtypeStruct((rows_s, D_MODEL), BF16),
            jax.ShapeDtypeStruct((rows_p, CONV_DIM), BF16), jax.ShapeDtypeStruct((rows_s, CONV_DIM), BF16),
            jax.ShapeDtypeStruct((rows_p, SG_DIM), BF16), jax.ShapeDtypeStruct((rows_s, SG_DIM), BF16),
            jax.ShapeDtypeStruct((n_seq, SUBLANES, CONV_DIM), F32),
            jax.ShapeDtypeStruct((n_seq, CHUNK, SG_DIM), F32),
            jax.ShapeDtypeStruct((n_sample, steps * CONV_DIM), F32),
            jax.ShapeDtypeStruct((n_sample, steps * SG_DIM), F32),
        ],
        scratch_shapes=[
            pltpu.VMEM((D_MODEL, MIX_COLS), BF16),
            pltpu.VMEM((2, stage_rows, MIX_COLS), F32),
            pltpu.SemaphoreType.DMA((2,)),
            pltpu.VMEM((SUBLANES, CONV_DIM), F32),
            pltpu.VMEM(((CONV_WIDTH - 1) * n_sample + rows_s, CONV_DIM), F32),
            pltpu.VMEM((rows_s, SG_DIM), F32),
        ],
        compiler_params=_params(),
        name="mix",
    )(xp, xs, mod_p, mod_s, mod_p, mod_s, g_mix, w_in, w_conv, g_v, w_sg, bias_p, w_sg_s, bias_s, state)


def _merge_kernel(xp_ref, xs_ref, hp_ref, hs_ref, yap_ref, yas_ref, ybp_ref, ybs_ref,
                  gtmp_ref, gtms_ref, scfp_ref, scfs_ref, shfp_ref, shfs_ref, gffn_ref,
                  win_hbm, wpa_hbm, wpb_hbm, wout_hbm,
                  x1p_ref, x1s_ref, h2p_ref, h2s_ref,
                  wga_ref, wgb_ref, wpa_ref, wpb_ref, wout_ref, stage_ref, sem_ref):
    tp = xp_ref.shape[0]
    half = tp // 2

    @pl.when(pl.program_id(0) == 0)
    def _():
        _load_bf16(win_hbm, wga_ref, stage_ref, sem_ref, col0=MIX_COLS)
        _load_bf16(win_hbm, wgb_ref, stage_ref, sem_ref, col0=MIX_COLS + D_MODEL)
        _load_bf16(wpa_hbm, wpa_ref, stage_ref, sem_ref)
        _load_bf16(wpb_hbm, wpb_ref, stage_ref, sem_ref)
        _load_bf16(wout_hbm, wout_ref, stage_ref, sem_ref)

    h = _rows(hp_ref[...], hs_ref[...])
    ya = _rows(yap_ref[...], yas_ref[...])
    yb = _rows(ybp_ref[...], ybs_ref[...])
    halves = []
    for k in range(D_MODEL // SEG):
        cols = slice(k * SEG, (k + 1) * SEG)
        branch_a = jax.nn.sigmoid(_dot(h, wga_ref[:, cols])) * _dot(ya, wpa_ref[:, cols])
        branch_b = jax.nn.sigmoid(_dot(h, wgb_ref[:, cols])) * _dot(yb, wpb_ref[:, cols])
        halves.append((branch_a + branch_b).astype(BF16))
    merged = jnp.concatenate(halves, axis=1)

    gffn = gffn_ref[...]

    def finish(x, out, gate, scale, shift):
        x1 = x + gate * out
        return x1, _modulated_norm(x1, gffn, scale, shift).astype(BF16)

    gtp, scp, shp = gtmp_ref[...], scfp_ref[...], shfp_ref[...]
    out_a = _dot(merged[:half], wout_ref[...])
    x1p_ref[:half], h2p_ref[:half] = finish(xp_ref[:half], out_a, gtp, scp, shp)
    out_b = _dot(merged[half:], wout_ref[...])
    x1p_ref[half:], h2p_ref[half:] = finish(xp_ref[half:], out_b[:half], gtp, scp, shp)
    x1s_ref[...], h2s_ref[...] = finish(xs_ref[...], out_b[half:], gtms_ref[...], scfs_ref[...], shfs_ref[...])


def _merge(xp, xs, h_p, h_s, ya_p, ya_s, yb_p, yb_s, mod_p, mod_s, g_ffn, w_in, w_pa, w_pb, w_out, til):
    rows_p, rows_s = h_p.shape[0], h_s.shape[0]
    stage_rows = 256
    return pl.pallas_call(
        _merge_kernel,
        grid=(til.steps,),
        in_specs=[
            til.prompt(D_MODEL), til.sample_by_batch(D_MODEL), til.prompt(D_MODEL), til.sample(D_MODEL),
            til.prompt(CONV_DIM), til.sample(CONV_DIM), til.prompt(SG_DIM), til.sample(SG_DIM),
            *til.mods(2, 4, 3),
            _resident((1, D_MODEL)),
            _HBM, _HBM, _HBM, _HBM,
        ],
        out_specs=[til.prompt(D_MODEL), til.sample(D_MODEL), til.prompt(D_MODEL), til.sample(D_MODEL)],
        out_shape=[
            jax.ShapeDtypeStruct((rows_p, D_MODEL), F32), jax.ShapeDtypeStruct((rows_s, D_MODEL), F32),
            jax.ShapeDtypeStruct((rows_p, D_MODEL), BF16), jax.ShapeDtypeStruct((rows_s, D_MODEL), BF16),
        ],
        scratch_shapes=[
            pltpu.VMEM((D_MODEL, D_MODEL), BF16), pltpu.VMEM((D_MODEL, D_MODEL), BF16),
            pltpu.VMEM((CONV_DIM, D_MODEL), BF16), pltpu.VMEM((SG_DIM, D_MODEL), BF16),
            pltpu.VMEM((D_MODEL, D_MODEL), BF16),
            pltpu.VMEM((2, stage_rows, D_MODEL), F32),
            pltpu.SemaphoreType.DMA((2,)),
        ],
        compiler_params=_params(),
        name="merge",
    )(xp, xs, h_p, h_s, ya_p, ya_s, yb_p, yb_s, *([mod_p, mod_s] * 3), g_ffn, w_in, w_pa, w_pb, w_out)


def _ffn_in_kernel(hp_ref, hs_ref, wg_ref, wu_ref, op_ref, os_ref):
    tp = hp_ref.shape[0]
    h = _rows(hp_ref[...], hs_ref[...])
    act = jax.nn.silu(_dot(h, wg_ref[...].astype(BF16))) * _dot(h, wu_ref[...].astype(BF16))
    op_ref[...] = act[:tp].astype(BF16)
    os_ref[...] = act[tp:].astype(BF16)


def _ffn_in(h2_p, h2_s, w_ffn_in, tp, tn):
    rows_p, rows_s = h2_p.shape[0], h2_s.shape[0]
    steps = rows_p // tp
    ts = rows_s // steps
    n_blocks = D_FF // tn
    assert rows_p % tp == 0 and rows_s % steps == 0 and ts % 16 == 0 and D_FF % tn == 0
    return pl.pallas_call(
        _ffn_in_kernel,
        grid=(n_blocks, steps),
        in_specs=[
            pl.BlockSpec((tp, D_MODEL), lambda j, i: (i, 0)),
            pl.BlockSpec((ts, D_MODEL), lambda j, i: (i, 0)),
            pl.BlockSpec((D_MODEL, tn), lambda j, i: (0, j)),
            pl.BlockSpec((D_MODEL, tn), lambda j, i: (0, j + n_blocks)),
        ],
        out_specs=[
            pl.BlockSpec((tp, tn), lambda j, i: (i, j)),
            pl.BlockSpec((ts, tn), lambda j, i: (i, j)),
        ],
        out_shape=[
            jax.ShapeDtypeStruct((rows_p, D_FF), BF16),
            jax.ShapeDtypeStruct((rows_s, D_FF), BF16),
        ],
        compiler_params=pltpu.CompilerParams(
            dimension_semantics=("arbitrary", "arbitrary"), vmem_limit_bytes=VMEM_LIMIT_BYTES),
        name="ffn_in",
    )(h2_p, h2_s, w_ffn_in, w_ffn_in)


def _ffn_out_kernel(ap_ref, as_ref, x1p_ref, x1s_ref, gtp_ref, gts_ref, gfin_ref, w_hbm, yp_ref, ys_ref,
                    w_ref, stage_ref, sem_ref):
    half = ap_ref.shape[0] // 2

    @pl.when(pl.program_id(0) == 0)
    def _():
        _load_bf16(w_hbm, w_ref, stage_ref, sem_ref)

    gfin = gfin_ref[...]
    gtp = gtp_ref[...]
    out_a = _dot(ap_ref[:half], w_ref[...])
    yp_ref[:half] = _rms(x1p_ref[:half] + gtp * out_a, gfin)
    out_b = _dot(_rows(ap_ref[half:], as_ref[...]), w_ref[...])
    yp_ref[half:] = _rms(x1p_ref[half:] + gtp * out_b[:half], gfin)
    ys_ref[...] = _rms(x1s_ref[...] + gts_ref[...] * out_b[half:], gfin)


def _ffn_out(act_p, act_s, x1_p, x1_s, mod_p, mod_s, g_final, w_ffn_out, til, n_sample, steps):
    rows_p = x1_p.shape[0]
    stage_rows = 256
    return pl.pallas_call(
        _ffn_out_kernel,
        grid=(til.steps,),
        in_specs=[
            til.prompt(D_FF), til.sample(D_FF), til.prompt(D_MODEL), til.sample(D_MODEL),
            *til.mods(5),
            _resident((1, D_MODEL)),
            _HBM,
        ],
        out_specs=[til.prompt(D_MODEL), til.sample_by_batch(D_MODEL)],
        out_shape=[
            jax.ShapeDtypeStruct((rows_p, D_MODEL), F32),
            jax.ShapeDtypeStruct((n_sample, steps * D_MODEL), F32),
        ],
        scratch_shapes=[
            pltpu.VMEM((D_FF, D_MODEL), BF16),
            pltpu.VMEM((2, stage_rows, D_MODEL), F32),
            pltpu.SemaphoreType.DMA((2,)),
        ],
        compiler_params=_params(),
        name="ffn_out",
    )(act_p, act_s, x1_p, x1_s, mod_p, mod_s, g_final, w_ffn_out)


def _lane_expand(per_group):
    return jnp.repeat(per_group, SG_HEAD, axis=1)


def kernel(x_prompt, x_sample, state_conv, c_prompt, c_sample, g_mix, g_ffn, w_ada, b_ada, w_in, w_conv, g_v,
           w_sg, b_sg, w_pa, w_pb, w_out, w_ffn_in, w_ffn_out, g_final):
    assert g_mix.shape[0] == 1, "one layer"
    batch, seq, _ = x_prompt.shape
    n_sample, steps, _ = x_sample.shape
    assert batch <= SUBLANES and seq % CHUNK == 0
    row = lambda v: v.reshape(1, -1)

    c_all = jnp.concatenate([c_prompt, jnp.zeros((SUBLANES - batch, D_MODEL), F32), c_sample], axis=0)
    mod_p, mod_s = _ada(c_all, w_ada[0], row(b_ada[0]), n_sample)
    mod_p = mod_p.reshape(N_MOD, SUBLANES, 1, D_MODEL)

    xp = x_prompt.reshape(batch * seq, D_MODEL)
    xs = x_sample.reshape(n_sample, steps * D_MODEL)
    state = state_conv[0].reshape(n_sample, (CONV_WIDTH - 1) * CONV_DIM)
    til = _Tiling(batch * seq, steps * n_sample, seq, n_sample, tp=256)

    bias_p = _lane_expand(b_sg[0][:, :CHUNK].T)
    w_sg_s = _lane_expand(w_sg[0][:, :steps, :steps].transpose(1, 2, 0).reshape(steps * steps, SG_GROUPS))
    w_sg_s = w_sg_s.reshape(steps * steps, 1, SG_DIM)
    bias_s = _lane_expand(b_sg[0][:, :steps].T).reshape(steps, 1, SG_DIM)
    h_p, h_s, ya_p, ya_s, yb_p, yb_s, zst_p, vst_p, z_s, vn_s = _mix(
        xp, xs, mod_p, mod_s, row(g_mix[0]), w_in[0], w_conv[0], row(g_v[0]), w_sg[0], bias_p, w_sg_s, bias_s,
        state, til, n_sample, steps)

    x1_p, x1_s, h2_p, h2_s = _merge(xp, xs, h_p, h_s, ya_p, ya_s, yb_p, yb_s, mod_p, mod_s, row(g_ffn[0]),
                                    w_in[0], w_pa[0], w_pb[0], w_out[0], til)
    act_p, act_s = _ffn_in(h2_p, h2_s, w_ffn_in[0], tp=2048, tn=512)
    y_p, y_s = _ffn_out(act_p, act_s, x1_p, x1_s, mod_p, mod_s, row(g_final), w_ffn_out[0], til, n_sample, steps)

    y_prompt = y_p.reshape(batch, seq, D_MODEL)
    y_sample = y_s.reshape(n_sample, steps, D_MODEL)
    conv_prompt = zst_p[:, SUBLANES - (CONV_WIDTH - 1):, :][None]
    conv_sample = z_s.reshape(n_sample, steps, CONV_DIM)[:, steps - (CONV_WIDTH - 1):, :][None]
    sgv_prompt = vst_p.reshape(1, batch, CHUNK, SG_GROUPS, SG_HEAD)
    sgv_sample = vn_s.reshape(1, n_sample, steps, SG_GROUPS, SG_HEAD)
    return (y_prompt, y_sample, conv_prompt, conv_sample, sgv_prompt, sgv_sample)
```

```python
import functools

import jax
import jax.numpy as jnp
from jax import lax
from jax.experimental import pallas as pl
from jax.experimental.pallas import tpu as pltpu

D_MODEL = 2048
CONV_DIM = D_MODEL // 2
CONV_WIDTH = 3
SG_DIM = D_MODEL // 2
SG_GROUPS = 8
SG_HEAD = SG_DIM // SG_GROUPS
CHUNK = 128
D_FF = 5632
N_MOD = 6
EPS = 1e-6
SEG = 1024
MIX_COLS = 3 * CONV_DIM + 2 * SG_DIM
SUBLANES = 8
FFN_CHUNK = 256
STAGE_SLOTS = 4
VMEM_LIMIT_BYTES = 60 * 1024 * 1024

F32 = jnp.float32
BF16 = jnp.bfloat16


def _dot(a, b):
    return jnp.dot(a, b, preferred_element_type=F32)


def _rms(x, gain):
    return x * lax.rsqrt(jnp.mean(x * x, axis=-1, keepdims=True) + EPS) * gain


def _modulated_norm(x, gain, scale, shift):
    return _rms(x, gain) * (1 + scale) + shift


def _rows(*parts):
    return jnp.concatenate(parts, axis=0)


def _load_bf16(jobs, stage_ref, sem_ref):
    n_slots, rows, n = stage_ref.shape
    chunks = [(src, col0, dst, r0) for src, col0, dst in jobs for r0 in range(0, dst.shape[0], rows)]
    assert all(dst.shape[1] == n and dst.shape[0] % rows == 0 for _, _, dst in jobs)

    def chunk_copy(idx):
        src, col0, _, r0 = chunks[idx]
        slot = idx % n_slots
        return pltpu.make_async_copy(src.at[pl.ds(r0, rows), pl.ds(col0, n)], stage_ref.at[slot], sem_ref.at[slot])

    ahead = n_slots - 1
    for idx in range(min(ahead, len(chunks))):
        chunk_copy(idx).start()
    for idx, (_, _, dst, r0) in enumerate(chunks):
        if idx + ahead < len(chunks):
            chunk_copy(idx + ahead).start()
        chunk_copy(idx).wait()
        dst[r0:r0 + rows, :] = stage_ref[idx % n_slots].astype(BF16)


def _ada_kernel(c_ref, w_ref, b_ref, mp_ref, ms_ref):
    a = jax.nn.silu(c_ref[...]).astype(BF16)
    r = _dot(a, w_ref[...].astype(BF16)) + b_ref[...]
    mp_ref[...] = r[:SUBLANES]
    ms_ref[...] = r[SUBLANES:]


def _ada(c_all, w_ada, b_ada, n_sample):
    tn = 1024
    per = D_MODEL // tn
    rows = c_all.shape[0]
    return pl.pallas_call(
        _ada_kernel,
        grid=(N_MOD * per,),
        in_specs=[
            pl.BlockSpec((rows, D_MODEL), lambda j: (0, 0)),
            pl.BlockSpec((D_MODEL, tn), lambda j: (0, j)),
            pl.BlockSpec((1, tn), lambda j: (0, j)),
        ],
        out_specs=[
            pl.BlockSpec((None, SUBLANES, tn), lambda j: (j // per, 0, j % per)),
            pl.BlockSpec((None, n_sample, tn), lambda j: (j // per, 0, j % per)),
        ],
        out_shape=[
            jax.ShapeDtypeStruct((N_MOD, SUBLANES, D_MODEL), F32),
            jax.ShapeDtypeStruct((N_MOD, n_sample, D_MODEL), F32),
        ],
        compiler_params=pltpu.CompilerParams(
            dimension_semantics=("arbitrary",), vmem_limit_bytes=VMEM_LIMIT_BYTES),
        name="ada",
    )(c_all, w_ada, b_ada)


class _Tiling:
    def __init__(self, rows_p, rows_s, seq, n_sample, tp):
        self.tp = tp
        self.steps = rows_p // tp
        self.ts = rows_s // self.steps
        self.tiles_per_seq = seq // tp
        self.sample_blocks = n_sample // self.ts
        assert rows_p % tp == 0 and seq % tp == 0 and rows_s % self.steps == 0
        assert self.ts % 16 == 0 and n_sample % self.ts == 0

    def prompt(self, width):
        return pl.BlockSpec((self.tp, width), lambda i: (i, 0))

    def sample(self, width):
        return pl.BlockSpec((self.ts, width), lambda i: (i, 0))

    def sample_by_batch(self, width):
        return pl.BlockSpec((self.ts, width), lambda i: (i % self.sample_blocks, i // self.sample_blocks))

    def prompt_mod(self, comp):
        return pl.BlockSpec((None, None, 1, D_MODEL), lambda i: (comp, i // self.tiles_per_seq, 0, 0))

    def sample_mod(self, comp):
        return pl.BlockSpec((None, self.ts, D_MODEL), lambda i: (comp, i % self.sample_blocks, 0))

    def mods(self, *comps):
        return [spec(c) for c in comps for spec in (self.prompt_mod, self.sample_mod)]


def _resident(shape):
    return pl.BlockSpec(shape, lambda i: (0,) * len(shape), pipeline_mode=pl.Buffered(1))


_HBM = pl.BlockSpec(memory_space=pl.ANY)


def _params():
    return pltpu.CompilerParams(dimension_semantics=("arbitrary",), vmem_limit_bytes=VMEM_LIMIT_BYTES)


def _mix_kernel(xp_ref, xs_ref, scp_ref, scs_ref, shp_ref, shs_ref, gmix_ref, w_hbm, wconv_ref, gv_ref,
                wsg_ref, bsgp_ref, wsgs_ref, bsgs_ref, state_ref,
                hp_ref, hs_ref, yap_ref, yas_ref, ybp_ref, ybs_ref, zstp_ref, vstp_ref, zs_ref, vs_ref,
                w_ref, stage_ref, sem_ref, carry_ref, zhist_ref, vhist_ref, *, tiles_per_seq, n_sample, steps):
    i = pl.program_id(0)
    tp = xp_ref.shape[0]
    ts = xs_ref.shape[0]

    @pl.when(i == 0)
    def _():
        _load_bf16([(w_hbm, 0, w_ref)], stage_ref, sem_ref)
        for k in range(CONV_WIDTH - 1):
            zhist_ref[k * n_sample:(k + 1) * n_sample] = state_ref[:, k * CONV_DIM:(k + 1) * CONV_DIM]
        vhist_ref[...] = jnp.zeros_like(vhist_ref)

    @pl.when(i % tiles_per_seq == 0)
    def _():
        carry_ref[...] = jnp.zeros_like(carry_ref)

    gmix = gmix_ref[...]
    h = _rows(_modulated_norm(xp_ref[...], gmix, scp_ref[...], shp_ref[...]),
              _modulated_norm(xs_ref[...], gmix, scs_ref[...], shs_ref[...])).astype(BF16)
    hp_ref[...] = h[:tp]
    hs_ref[...] = h[tp:]
    proj = lambda k: _dot(h, w_ref[:, k * SEG:(k + 1) * SEG])
    base = pl.multiple_of(i * ts, ts)

    vn = _rms(jax.nn.gelu(proj(4)), gv_ref[...])
    vstp_ref[...] = vn[tp - CHUNK:tp]
    vb = vn[:tp].astype(BF16)
    causal = (lax.broadcasted_iota(jnp.int32, (CHUNK, CHUNK), 0)
              >= lax.broadcasted_iota(jnp.int32, (CHUNK, CHUNK), 1))
    wgs = [jnp.where(causal, wsg_ref[g], 0.0).astype(BF16) for g in range(SG_GROUPS)]
    bias = bsgp_ref[...]
    chunks = []
    for c in range(tp // CHUNK):
        rows = slice(c * CHUNK, (c + 1) * CHUNK)
        parts = [_dot(wgs[g], vb[rows, g * SG_HEAD:(g + 1) * SG_HEAD]) for g in range(SG_GROUPS)]
        chunks.append(jnp.concatenate(parts, axis=1) + bias)
    gu = jax.nn.gelu(proj(3))
    ybp_ref[...] = (gu[:tp] * _rows(*chunks)).astype(BF16)

    vs = vn[tp:]
    vs_ref[...] = vs
    block = i % (n_sample // ts)
    t = i // (n_sample // ts)
    vhist_ref[pl.ds(base, ts), :] = vs
    sp = bsgs_ref[t]
    for s in range(steps):
        w_ts = jnp.where(s <= t, wsgs_ref[t * steps + s], 0.0)
        sp = sp + w_ts * vhist_ref[pl.ds(pl.multiple_of(block * ts, ts) + s * n_sample, ts), :]
    ybs_ref[...] = (gu[tp:] * sp).astype(BF16)

    wc = wconv_ref[...]
    z_all = proj(1) * proj(2)
    b_gate = proj(0)
    z = z_all[:tp]
    carry = carry_ref[...]
    prev2, prev1 = carry[SUBLANES - 2:SUBLANES - 1], carry[SUBLANES - 1:SUBLANES]
    row = lax.broadcasted_iota(jnp.int32, (SUBLANES, CONV_DIM), 0)
    z1 = pltpu.roll(z, 1, 0)
    z2 = pltpu.roll(z, 2, 0)
    z1 = _rows(jnp.where(row == 0, prev1, z1[:SUBLANES]), z1[SUBLANES:])
    z2 = _rows(jnp.where(row == 0, prev2, jnp.where(row == 1, prev1, z2[:SUBLANES])), z2[SUBLANES:])
    conv = wc[0:1] * z2 + wc[1:2] * z1 + wc[2:3] * z
    yap_ref[...] = (b_gate[:tp] * conv).astype(BF16)
    carry_ref[...] = z[tp - SUBLANES:]
    zstp_ref[...] = z[tp - SUBLANES:]

    hist = lambda ref, k: ref[pl.ds(base + k * n_sample, ts), :]
    zs = z_all[tp:]
    zhist_ref[pl.ds(base + (CONV_WIDTH - 1) * n_sample, ts), :] = zs
    conv_s = wc[0:1] * hist(zhist_ref, 0) + wc[1:2] * hist(zhist_ref, 1) + wc[2:3] * zs
    yas_ref[...] = (b_gate[tp:] * conv_s).astype(BF16)
    zs_ref[...] = zs


def _mix(xp, xs, mod_p, mod_s, g_mix, w_in, w_conv, g_v, w_sg, bias_p, w_sg_s, bias_s, state, til, n_sample,
         steps):
    rows_p = xp.shape[0]
    rows_s = n_sample * steps
    n_seq = rows_p // (til.tiles_per_seq * til.tp)
    per_seq = lambda rows, width: pl.BlockSpec((None, rows, width), lambda i: (i // til.tiles_per_seq, 0, 0))
    stage_rows = 64
    return pl.pallas_call(
        functools.partial(_mix_kernel, tiles_per_seq=til.tiles_per_seq, n_sample=n_sample, steps=steps),
        grid=(til.steps,),
        in_specs=[
            til.prompt(D_MODEL), til.sample_by_batch(D_MODEL), *til.mods(1, 0),
            _resident((1, D_MODEL)),
            _HBM,
            _resident((CONV_WIDTH, CONV_DIM)),
            _resident((1, SG_DIM)),
            _resident((SG_GROUPS, CHUNK, CHUNK)),
            _resident((CHUNK, SG_DIM)),
            _resident((steps * steps, 1, SG_DIM)),
            _resident((steps, 1, SG_DIM)),
            _resident((n_sample, (CONV_WIDTH - 1) * CONV_DIM)),
        ],
        out_specs=[
            til.prompt(D_MODEL), til.sample(D_MODEL),
            til.prompt(CONV_DIM), til.sample(CONV_DIM), til.prompt(SG_DIM), til.sample(SG_DIM),
            per_seq(SUBLANES, CONV_DIM), per_seq(CHUNK, SG_DIM),
            til.sample_by_batch(CONV_DIM), til.sample_by_batch(SG_DIM),
        ],
        out_shape=[
            jax.ShapeDtypeStruct((rows_p, D_MODEL), BF16), jax.ShapeDtypeStruct((rows_s, D_MODEL), BF16),
            jax.ShapeDtypeStruct((rows_p, CONV_DIM), BF16), jax.ShapeDtypeStruct((rows_s, CONV_DIM), BF16),
            jax.ShapeDtypeStruct((rows_p, SG_DIM), BF16), jax.ShapeDtypeStruct((rows_s, SG_DIM), BF16),
            jax.ShapeDtypeStruct((n_seq, SUBLANES, CONV_DIM), F32),
            jax.ShapeDtypeStruct((n_seq, CHUNK, SG_DIM), F32),
            jax.ShapeDtypeStruct((n_sample, steps * CONV_DIM), F32),
            jax.ShapeDtypeStruct((n_sample, steps * SG_DIM), F32),
        ],
        scratch_shapes=[
            pltpu.VMEM((D_MODEL, MIX_COLS), BF16),
            pltpu.VMEM((STAGE_SLOTS, stage_rows, MIX_COLS), F32),
            pltpu.SemaphoreType.DMA((STAGE_SLOTS,)),
            pltpu.VMEM((SUBLANES, CONV_DIM), F32),
            pltpu.VMEM(((CONV_WIDTH - 1) * n_sample + rows_s, CONV_DIM), F32),
            pltpu.VMEM((rows_s, SG_DIM), F32),
        ],
        compiler_params=_params(),
        name="mix",
    )(xp, xs, mod_p, mod_s, mod_p, mod_s, g_mix, w_in, w_conv, g_v, w_sg, bias_p, w_sg_s, bias_s, state)


def _merge_kernel(xp_ref, xs_ref, hp_ref, hs_ref, yap_ref, yas_ref, ybp_ref, ybs_ref,
                  gtmp_ref, gtms_ref, scfp_ref, scfs_ref, shfp_ref, shfs_ref, gffn_ref,
                  win_hbm, wpa_hbm, wpb_hbm, wout_hbm,
                  x1p_ref, x1s_ref, h2p_ref, h2s_ref,
                  wga_ref, wgb_ref, wpa_ref, wpb_ref, wout_ref, stage_ref, sem_ref):
    tp = xp_ref.shape[0]
    half = tp // 2

    @pl.when(pl.program_id(0) == 0)
    def _():
        _load_bf16([(win_hbm, MIX_COLS, wga_ref), (win_hbm, MIX_COLS + D_MODEL, wgb_ref), (wpa_hbm, 0, wpa_ref),
                    (wpb_hbm, 0, wpb_ref), (wout_hbm, 0, wout_ref)], stage_ref, sem_ref)

    h = _rows(hp_ref[...], hs_ref[...])
    ya = _rows(yap_ref[...], yas_ref[...])
    yb = _rows(ybp_ref[...], ybs_ref[...])
    halves = []
    for k in range(D_MODEL // SEG):
        cols = slice(k * SEG, (k + 1) * SEG)
        branch_a = jax.nn.sigmoid(_dot(h, wga_ref[:, cols])) * _dot(ya, wpa_ref[:, cols])
        branch_b = jax.nn.sigmoid(_dot(h, wgb_ref[:, cols])) * _dot(yb, wpb_ref[:, cols])
        halves.append((branch_a + branch_b).astype(BF16))
    merged = jnp.concatenate(halves, axis=1)

    gffn = gffn_ref[...]

    def finish(x, out, gate, scale, shift):
        x1 = x + gate * out
        return x1, _modulated_norm(x1, gffn, scale, shift).astype(BF16)

    gtp, scp, shp = gtmp_ref[...], scfp_ref[...], shfp_ref[...]
    out_a = _dot(merged[:half], wout_ref[...])
    x1p_ref[:half], h2p_ref[:half] = finish(xp_ref[:half], out_a, gtp, scp, shp)
    out_b = _dot(merged[half:], wout_ref[...])
    x1p_ref[half:], h2p_ref[half:] = finish(xp_ref[half:], out_b[:half], gtp, scp, shp)
    x1s_ref[...], h2s_ref[...] = finish(xs_ref[...], out_b[half:], gtms_ref[...], scfs_ref[...], shfs_ref[...])


def _merge(xp, xs, h_p, h_s, ya_p, ya_s, yb_p, yb_s, mod_p, mod_s, g_ffn, w_in, w_pa, w_pb, w_out, til):
    rows_p, rows_s = h_p.shape[0], h_s.shape[0]
    stage_rows = 128
    return pl.pallas_call(
        _merge_kernel,
        grid=(til.steps,),
        in_specs=[
            til.prompt(D_MODEL), til.sample_by_batch(D_MODEL), til.prompt(D_MODEL), til.sample(D_MODEL),
            til.prompt(CONV_DIM), til.sample(CONV_DIM), til.prompt(SG_DIM), til.sample(SG_DIM),
            *til.mods(2, 4, 3),
            _resident((1, D_MODEL)),
            _HBM, _HBM, _HBM, _HBM,
        ],
        out_specs=[til.prompt(D_MODEL), til.sample(D_MODEL), til.prompt(D_MODEL), til.sample(D_MODEL)],
        out_shape=[
            jax.ShapeDtypeStruct((rows_p, D_MODEL), F32), jax.ShapeDtypeStruct((rows_s, D_MODEL), F32),
            jax.ShapeDtypeStruct((rows_p, D_MODEL), BF16), jax.ShapeDtypeStruct((rows_s, D_MODEL), BF16),
        ],
        scratch_shapes=[
            pltpu.VMEM((D_MODEL, D_MODEL), BF16), pltpu.VMEM((D_MODEL, D_MODEL), BF16),
            pltpu.VMEM((CONV_DIM, D_MODEL), BF16), pltpu.VMEM((SG_DIM, D_MODEL), BF16),
            pltpu.VMEM((D_MODEL, D_MODEL), BF16),
            pltpu.VMEM((STAGE_SLOTS, stage_rows, D_MODEL), F32),
            pltpu.SemaphoreType.DMA((STAGE_SLOTS,)),
        ],
        compiler_params=_params(),
        name="merge",
    )(xp, xs, h_p, h_s, ya_p, ya_s, yb_p, yb_s, *([mod_p, mod_s] * 3), g_ffn, w_in, w_pa, w_pb, w_out)


def _ffn_in_kernel(hp_ref, hs_ref, wg_ref, wu_ref, op_ref, os_ref):
    tp = hp_ref.shape[0]
    h = _rows(hp_ref[...], hs_ref[...])
    for c in range(0, wg_ref.shape[1], FFN_CHUNK):
        cols = slice(c, c + FFN_CHUNK)
        act = (jax.nn.silu(_dot(h, wg_ref[:, cols].astype(BF16))) * _dot(h, wu_ref[:, cols].astype(BF16)))
        op_ref[:, cols] = act[:tp].astype(BF16)
        os_ref[:, cols] = act[tp:].astype(BF16)


def _ffn_in(h2_p, h2_s, w_ffn_in, tp, tn):
    rows_p, rows_s = h2_p.shape[0], h2_s.shape[0]
    steps = rows_p // tp
    ts = rows_s // steps
    n_blocks = D_FF // tn
    assert rows_p % tp == 0 and rows_s % steps == 0 and ts % 16 == 0 and D_FF % tn == 0
    return pl.pallas_call(
        _ffn_in_kernel,
        grid=(n_blocks, steps),
        in_specs=[
            pl.BlockSpec((tp, D_MODEL), lambda j, i: (i, 0)),
            pl.BlockSpec((ts, D_MODEL), lambda j, i: (i, 0)),
            pl.BlockSpec((D_MODEL, tn), lambda j, i: (0, j)),
            pl.BlockSpec((D_MODEL, tn), lambda j, i: (0, j + n_blocks)),
        ],
        out_specs=[
            pl.BlockSpec((tp, tn), lambda j, i: (i, j)),
            pl.BlockSpec((ts, tn), lambda j, i: (i, j)),
        ],
        out_shape=[
            jax.ShapeDtypeStruct((rows_p, D_FF), BF16),
            jax.ShapeDtypeStruct((rows_s, D_FF), BF16),
        ],
        compiler_params=pltpu.CompilerParams(
            dimension_semantics=("arbitrary", "arbitrary"), vmem_limit_bytes=VMEM_LIMIT_BYTES),
        name="ffn_in",
    )(h2_p, h2_s, w_ffn_in, w_ffn_in)


def _ffn_out_kernel(ap_ref, as_ref, x1p_ref, x1s_ref, gtp_ref, gts_ref, gfin_ref, w_hbm, yp_ref, ys_ref,
                    w_ref, stage_ref, sem_ref):
    half = ap_ref.shape[0] // 2

    @pl.when(pl.program_id(0) == 0)
    def _():
        _load_bf16([(w_hbm, 0, w_ref)], stage_ref, sem_ref)

    gfin = gfin_ref[...]
    gtp = gtp_ref[...]
    out_a = _dot(ap_ref[:half], w_ref[...])
    yp_ref[:half] = _rms(x1p_ref[:half] + gtp * out_a, gfin)
    out_b = _dot(_rows(ap_ref[half:], as_ref[...]), w_ref[...])
    yp_ref[half:] = _rms(x1p_ref[half:] + gtp * out_b[:half], gfin)
    ys_ref[...] = _rms(x1s_ref[...] + gts_ref[...] * out_b[half:], gfin)


def _ffn_out(act_p, act_s, x1_p, x1_s, mod_p, mod_s, g_final, w_ffn_out, til, n_sample, steps):
    rows_p = x1_p.shape[0]
    stage_rows = 128
    return pl.pallas_call(
        _ffn_out_kernel,
        grid=(til.steps,),
        in_specs=[
            til.prompt(D_FF), til.sample(D_FF), til.prompt(D_MODEL), til.sample(D_MODEL),
            *til.mods(5),
            _resident((1, D_MODEL)),
            _HBM,
        ],
        out_specs=[til.prompt(D_MODEL), til.sample_by_batch(D_MODEL)],
        out_shape=[
            jax.ShapeDtypeStruct((rows_p, D_MODEL), F32),
            jax.ShapeDtypeStruct((n_sample, steps * D_MODEL), F32),
        ],
        scratch_shapes=[
            pltpu.VMEM((D_FF, D_MODEL), BF16),
            pltpu.VMEM((STAGE_SLOTS, stage_rows, D_MODEL), F32),
            pltpu.SemaphoreType.DMA((STAGE_SLOTS,)),
        ],
        compiler_params=_params(),
        name="ffn_out",
    )(act_p, act_s, x1_p, x1_s, mod_p, mod_s, g_final, w_ffn_out)


def _lane_expand(per_group):
    return jnp.repeat(per_group, SG_HEAD, axis=1)


def kernel(x_prompt, x_sample, state_conv, c_prompt, c_sample, g_mix, g_ffn, w_ada, b_ada, w_in, w_conv, g_v,
           w_sg, b_sg, w_pa, w_pb, w_out, w_ffn_in, w_ffn_out, g_final):
    assert g_mix.shape[0] == 1, "one layer"
    batch, seq, _ = x_prompt.shape
    n_sample, steps, _ = x_sample.shape
    assert batch <= SUBLANES and seq % CHUNK == 0
    row = lambda v: v.reshape(1, -1)

    c_all = jnp.concatenate([c_prompt, jnp.zeros((SUBLANES - batch, D_MODEL), F32), c_sample], axis=0)
    mod_p, mod_s = _ada(c_all, w_ada[0], row(b_ada[0]), n_sample)
    mod_p = mod_p.reshape(N_MOD, SUBLANES, 1, D_MODEL)

    xp = x_prompt.reshape(batch * seq, D_MODEL)
    xs = x_sample.reshape(n_sample, steps * D_MODEL)
    state = state_conv[0].reshape(n_sample, (CONV_WIDTH - 1) * CONV_DIM)
    til = _Tiling(batch * seq, steps * n_sample, seq, n_sample, tp=256)

    bias_p = _lane_expand(b_sg[0][:, :CHUNK].T)
    w_sg_s = _lane_expand(w_sg[0][:, :steps, :steps].transpose(1, 2, 0).reshape(steps * steps, SG_GROUPS))
    w_sg_s = w_sg_s.reshape(steps * steps, 1, SG_DIM)
    bias_s = _lane_expand(b_sg[0][:, :steps].T).reshape(steps, 1, SG_DIM)
    h_p, h_s, ya_p, ya_s, yb_p, yb_s, zst_p, vst_p, z_s, vn_s = _mix(
        xp, xs, mod_p, mod_s, row(g_mix[0]), w_in[0], w_conv[0], row(g_v[0]), w_sg[0], bias_p, w_sg_s, bias_s,
        state, til, n_sample, steps)

    x1_p, x1_s, h2_p, h2_s = _merge(xp, xs, h_p, h_s, ya_p, ya_s, yb_p, yb_s, mod_p, mod_s, row(g_ffn[0]),
                                    w_in[0], w_pa[0], w_pb[0], w_out[0], til)
    act_p, act_s = _ffn_in(h2_p, h2_s, w_ffn_in[0], tp=2048, tn=512)
    y_p, y_s = _ffn_out(act_p, act_s, x1_p, x1_s, mod_p, mod_s, row(g_final), w_ffn_out[0], til, n_sample, steps)

    y_prompt = y_p.reshape(batch, seq, D_MODEL)
    y_sample = y_s.reshape(n_sample, steps, D_MODEL)
    conv_prompt = zst_p[:, SUBLANES - (CONV_WIDTH - 1):, :][None]
    conv_sample = z_s.reshape(n_sample, steps, CONV_DIM)[:, steps - (CONV_WIDTH - 1):, :][None]
    sgv_prompt = vst_p.reshape(1, batch, CHUNK, SG_GROUPS, SG_HEAD)
    sgv_sample = vn_s.reshape(1, n_sample, steps, SG_GROUPS, SG_HEAD)
    return (y_prompt, y_sample, conv_prompt, conv_sample, sgv_prompt, sgv_sample)
```

```python
import functools

import jax
import jax.numpy as jnp
from jax import lax
from jax.experimental import pallas as pl
from jax.experimental.pallas import tpu as pltpu

D_MODEL = 2048
CONV_DIM = D_MODEL // 2
CONV_WIDTH = 3
SG_DIM = D_MODEL // 2
SG_GROUPS = 8
SG_HEAD = SG_DIM // SG_GROUPS
CHUNK = 128
D_FF = 5632
N_MOD = 6
EPS = 1e-6
SEG = 1024
MIX_COLS = 3 * CONV_DIM + 2 * SG_DIM
SUBLANES = 8
FFN_CHUNK = 256
STAGE_SLOTS = 4
VMEM_LIMIT_BYTES = 60 * 1024 * 1024

MOD_ORDER = (1, 0, 5, 2, 4, 3)
MIX_MODS, FFN_OUT_MODS, MERGE_MODS = (0, 2), (2, 1), (3, 3)

F32 = jnp.float32
BF16 = jnp.bfloat16


def _dot(a, b):
    return jnp.dot(a, b, preferred_element_type=F32)


def _rms(x, gain):
    return x * lax.rsqrt(jnp.mean(x * x, axis=-1, keepdims=True) + EPS) * gain


def _modulated_norm(x, gain, scale, shift):
    return _rms(x, gain) * (1 + scale) + shift


def _rows(*parts):
    return jnp.concatenate(parts, axis=0)


def _load_bf16(jobs, stage_ref, sem_ref):
    n_slots, rows, n = stage_ref.shape
    chunks = [(src, col0, dst, r0) for src, col0, dst in jobs for r0 in range(0, dst.shape[0], rows)]
    assert all(dst.shape[1] == n and dst.shape[0] % rows == 0 for _, _, dst in jobs)

    def chunk_copy(idx):
        src, col0, _, r0 = chunks[idx]
        slot = idx % n_slots
        return pltpu.make_async_copy(src.at[pl.ds(r0, rows), pl.ds(col0, n)], stage_ref.at[slot], sem_ref.at[slot])

    ahead = n_slots - 1
    for idx in range(min(ahead, len(chunks))):
        chunk_copy(idx).start()
    for idx, (_, _, dst, r0) in enumerate(chunks):
        if idx + ahead < len(chunks):
            chunk_copy(idx + ahead).start()
        chunk_copy(idx).wait()
        dst[r0:r0 + rows, :] = stage_ref[idx % n_slots].astype(BF16)


def _ada_kernel(c_ref, w_ref, b_ref, mp_ref, ms_ref):
    a = jax.nn.silu(c_ref[...]).astype(BF16)
    r = _dot(a, w_ref[...].astype(BF16)) + b_ref[...]
    mp_ref[...] = r[:SUBLANES]
    ms_ref[...] = r[SUBLANES:]


def _ada(c_all, w_ada, b_ada, n_sample):
    tn = 1024
    per = D_MODEL // tn
    rows = c_all.shape[0]

    def src_block(j):
        comp = 0
        for slot, c in enumerate(MOD_ORDER):
            comp = jnp.where(j // per == slot, c, comp)
        return (0, comp * per + j % per)

    return pl.pallas_call(
        _ada_kernel,
        grid=(N_MOD * per,),
        in_specs=[
            pl.BlockSpec((rows, D_MODEL), lambda j: (0, 0)),
            pl.BlockSpec((D_MODEL, tn), src_block),
            pl.BlockSpec((1, tn), src_block),
        ],
        out_specs=[
            pl.BlockSpec((None, SUBLANES, tn), lambda j: (j // per, 0, j % per)),
            pl.BlockSpec((None, n_sample, tn), lambda j: (j // per, 0, j % per)),
        ],
        out_shape=[
            jax.ShapeDtypeStruct((N_MOD, SUBLANES, D_MODEL), F32),
            jax.ShapeDtypeStruct((N_MOD, n_sample, D_MODEL), F32),
        ],
        compiler_params=pltpu.CompilerParams(
            dimension_semantics=("arbitrary",), vmem_limit_bytes=VMEM_LIMIT_BYTES),
        name="ada",
    )(c_all, w_ada, b_ada)


class _Tiling:
    def __init__(self, rows_p, rows_s, seq, n_sample, tp):
        self.tp = tp
        self.steps = rows_p // tp
        self.ts = rows_s // self.steps
        self.rows = rows_p + rows_s
        self.tiles_per_seq = seq // tp
        self.sample_blocks = n_sample // self.ts
        self.n_sample = n_sample
        assert rows_p % tp == 0 and seq % tp == 0 and rows_s % self.steps == 0
        assert self.ts % 16 == 0 and n_sample % self.ts == 0

    def prompt(self, width):
        return pl.BlockSpec((self.tp, width), lambda i: (i, 0))

    def sample_by_batch(self, width):
        return pl.BlockSpec((self.ts, width), lambda i: (i % self.sample_blocks, i // self.sample_blocks))

    def both(self, width):
        return pl.BlockSpec((self.tp + self.ts, width), lambda i: (i, 0))

    def mod_tables(self, slots):
        first, count = slots
        assert first % count == 0
        return [
            pl.BlockSpec((count, SUBLANES, 1, D_MODEL), lambda i: (first // count, 0, 0, 0),
                         pipeline_mode=pl.Buffered(1)),
            pl.BlockSpec((count, self.n_sample, D_MODEL), lambda i: (first // count, 0, 0),
                         pipeline_mode=pl.Buffered(1)),
        ]

    def mods(self, mp_ref, ms_ref, k):
        i = pl.program_id(0)
        start = pl.multiple_of((i % self.sample_blocks) * self.ts, self.ts)
        return mp_ref[k, i // self.tiles_per_seq], ms_ref[k, pl.ds(start, self.ts), :]


def _resident(shape):
    return pl.BlockSpec(shape, lambda i: (0,) * len(shape), pipeline_mode=pl.Buffered(1))


_HBM = pl.BlockSpec(memory_space=pl.ANY)


def _params():
    return pltpu.CompilerParams(dimension_semantics=("arbitrary",), vmem_limit_bytes=VMEM_LIMIT_BYTES)


def _mix_kernel(xp_ref, xs_ref, mp_ref, ms_ref, gmix_ref, w_hbm, wconv_ref, gv_ref,
                wsg_ref, bsgp_ref, wsgs_ref, bsgs_ref, state_ref,
                h_ref, ya_ref, yb_ref, zstp_ref, vstp_ref, zs_ref, vs_ref,
                w_ref, stage_ref, sem_ref, carry_ref, zhist_ref, vhist_ref, *, til, steps):
    i = pl.program_id(0)
    tp, ts, n_sample = til.tp, til.ts, til.n_sample

    @pl.when(i == 0)
    def _():
        _load_bf16([(w_hbm, 0, w_ref)], stage_ref, sem_ref)
        for k in range(CONV_WIDTH - 1):
            zhist_ref[k * n_sample:(k + 1) * n_sample] = state_ref[:, k * CONV_DIM:(k + 1) * CONV_DIM]
        vhist_ref[...] = jnp.zeros_like(vhist_ref)

    @pl.when(i % til.tiles_per_seq == 0)
    def _():
        carry_ref[...] = jnp.zeros_like(carry_ref)

    gmix = gmix_ref[...]
    scp, scs = til.mods(mp_ref, ms_ref, 0)
    shp, shs = til.mods(mp_ref, ms_ref, 1)
    h = _rows(_modulated_norm(xp_ref[...], gmix, scp, shp),
              _modulated_norm(xs_ref[...], gmix, scs, shs)).astype(BF16)
    h_ref[...] = h
    proj = lambda k: _dot(h, w_ref[:, k * SEG:(k + 1) * SEG])
    base = pl.multiple_of(i * ts, ts)

    vn = _rms(jax.nn.gelu(proj(4)), gv_ref[...])
    vstp_ref[...] = vn[tp - CHUNK:tp]
    vb = vn[:tp].astype(BF16)
    causal = (lax.broadcasted_iota(jnp.int32, (CHUNK, CHUNK), 0)
              >= lax.broadcasted_iota(jnp.int32, (CHUNK, CHUNK), 1))
    wgs = [jnp.where(causal, wsg_ref[g], 0.0).astype(BF16) for g in range(SG_GROUPS)]
    bias = bsgp_ref[...]
    chunks = []
    for c in range(tp // CHUNK):
        rows = slice(c * CHUNK, (c + 1) * CHUNK)
        parts = [_dot(wgs[g], vb[rows, g * SG_HEAD:(g + 1) * SG_HEAD]) for g in range(SG_GROUPS)]
        chunks.append(jnp.concatenate(parts, axis=1) + bias)
    gu = jax.nn.gelu(proj(3))
    yb_ref[:tp] = (gu[:tp] * _rows(*chunks)).astype(BF16)

    vs = vn[tp:]
    vs_ref[...] = vs
    block = i % til.sample_blocks
    t = i // til.sample_blocks
    vhist_ref[pl.ds(base, ts), :] = vs
    sp = bsgs_ref[t]
    for s in range(steps):
        w_ts = jnp.where(s <= t, wsgs_ref[t * steps + s], 0.0)
        sp = sp + w_ts * vhist_ref[pl.ds(pl.multiple_of(block * ts, ts) + s * n_sample, ts), :]
    yb_ref[tp:] = (gu[tp:] * sp).astype(BF16)

    wc = wconv_ref[...]
    z_all = proj(1) * proj(2)
    b_gate = proj(0)
    z = z_all[:tp]
    carry = carry_ref[...]
    prev2, prev1 = carry[SUBLANES - 2:SUBLANES - 1], carry[SUBLANES - 1:SUBLANES]
    row = lax.broadcasted_iota(jnp.int32, (SUBLANES, CONV_DIM), 0)
    z1 = pltpu.roll(z, 1, 0)
    z2 = pltpu.roll(z, 2, 0)
    z1 = _rows(jnp.where(row == 0, prev1, z1[:SUBLANES]), z1[SUBLANES:])
    z2 = _rows(jnp.where(row == 0, prev2, jnp.where(row == 1, prev1, z2[:SUBLANES])), z2[SUBLANES:])
    conv = wc[0:1] * z2 + wc[1:2] * z1 + wc[2:3] * z
    ya_ref[:tp] = (b_gate[:tp] * conv).astype(BF16)
    carry_ref[...] = z[tp - SUBLANES:]
    zstp_ref[...] = z[tp - SUBLANES:]

    hist = lambda ref, k: ref[pl.ds(base + k * n_sample, ts), :]
    zs = z_all[tp:]
    zhist_ref[pl.ds(base + (CONV_WIDTH - 1) * n_sample, ts), :] = zs
    conv_s = wc[0:1] * hist(zhist_ref, 0) + wc[1:2] * hist(zhist_ref, 1) + wc[2:3] * zs
    ya_ref[tp:] = (b_gate[tp:] * conv_s).astype(BF16)
    zs_ref[...] = zs


def _mix(xp, xs, mod_p, mod_s, g_mix, w_in, w_conv, g_v, w_sg, bias_p, w_sg_s, bias_s, state, til, steps):
    n_sample = til.n_sample
    rows_s = n_sample * steps
    n_seq = til.steps // til.tiles_per_seq
    per_seq = lambda rows, width: pl.BlockSpec((None, rows, width), lambda i: (i // til.tiles_per_seq, 0, 0))
    stage_rows = 64
    return pl.pallas_call(
        functools.partial(_mix_kernel, til=til, steps=steps),
        grid=(til.steps,),
        in_specs=[
            til.prompt(D_MODEL), til.sample_by_batch(D_MODEL), *til.mod_tables(MIX_MODS),
            _resident((1, D_MODEL)),
            _HBM,
            _resident((CONV_WIDTH, CONV_DIM)),
            _resident((1, SG_DIM)),
            _resident((SG_GROUPS, CHUNK, CHUNK)),
            _resident((CHUNK, SG_DIM)),
            _resident((steps * steps, 1, SG_DIM)),
            _resident((steps, 1, SG_DIM)),
            _resident((n_sample, (CONV_WIDTH - 1) * CONV_DIM)),
        ],
        out_specs=[
            til.both(D_MODEL), til.both(CONV_DIM), til.both(SG_DIM),
            per_seq(SUBLANES, CONV_DIM), per_seq(CHUNK, SG_DIM),
            til.sample_by_batch(CONV_DIM), til.sample_by_batch(SG_DIM),
        ],
        out_shape=[
            jax.ShapeDtypeStruct((til.rows, D_MODEL), BF16),
            jax.ShapeDtypeStruct((til.rows, CONV_DIM), BF16),
            jax.ShapeDtypeStruct((til.rows, SG_DIM), BF16),
            jax.ShapeDtypeStruct((n_seq, SUBLANES, CONV_DIM), F32),
            jax.ShapeDtypeStruct((n_seq, CHUNK, SG_DIM), F32),
            jax.ShapeDtypeStruct((n_sample, steps * CONV_DIM), F32),
            jax.ShapeDtypeStruct((n_sample, steps * SG_DIM), F32),
        ],
        scratch_shapes=[
            pltpu.VMEM((D_MODEL, MIX_COLS), BF16),
            pltpu.VMEM((STAGE_SLOTS, stage_rows, MIX_COLS), F32),
            pltpu.SemaphoreType.DMA((STAGE_SLOTS,)),
            pltpu.VMEM((SUBLANES, CONV_DIM), F32),
            pltpu.VMEM(((CONV_WIDTH - 1) * n_sample + rows_s, CONV_DIM), F32),
            pltpu.VMEM((rows_s, SG_DIM), F32),
        ],
        compiler_params=_params(),
        name="mix",
    )(xp, xs, mod_p, mod_s, g_mix, w_in, w_conv, g_v, w_sg, bias_p, w_sg_s, bias_s, state)


def _merge_kernel(xp_ref, xs_ref, h_ref, ya_ref, yb_ref, mp_ref, ms_ref, gffn_ref,
                  win_hbm, wpa_hbm, wpb_hbm, wout_hbm,
                  x1_ref, h2_ref,
                  wga_ref, wgb_ref, wpa_ref, wpb_ref, wout_ref, stage_ref, sem_ref, *, til):
    tp = til.tp
    half = tp // 2

    @pl.when(pl.program_id(0) == 0)
    def _():
        _load_bf16([(win_hbm, MIX_COLS, wga_ref), (win_hbm, MIX_COLS + D_MODEL, wgb_ref), (wpa_hbm, 0, wpa_ref),
                    (wpb_hbm, 0, wpb_ref), (wout_hbm, 0, wout_ref)], stage_ref, sem_ref)

    h = h_ref[...]
    ya = ya_ref[...]
    yb = yb_ref[...]
    halves = []
    for k in range(D_MODEL // SEG):
        cols = slice(k * SEG, (k + 1) * SEG)
        branch_a = jax.nn.sigmoid(_dot(h, wga_ref[:, cols])) * _dot(ya, wpa_ref[:, cols])
        branch_b = jax.nn.sigmoid(_dot(h, wgb_ref[:, cols])) * _dot(yb, wpb_ref[:, cols])
        halves.append((branch_a + branch_b).astype(BF16))
    merged = jnp.concatenate(halves, axis=1)

    gffn = gffn_ref[...]
    gtp, gts = til.mods(mp_ref, ms_ref, 0)
    scp, scs = til.mods(mp_ref, ms_ref, 1)
    shp, shs = til.mods(mp_ref, ms_ref, 2)

    def finish(rows, x, out, gate, scale, shift):
        x1 = x + gate * out
        x1_ref[rows] = x1
        h2_ref[rows] = _modulated_norm(x1, gffn, scale, shift).astype(BF16)

    out_a = _dot(merged[:half], wout_ref[...])
    finish(slice(0, half), xp_ref[:half], out_a, gtp, scp, shp)
    out_b = _dot(merged[half:], wout_ref[...])
    finish(slice(half, tp), xp_ref[half:], out_b[:half], gtp, scp, shp)
    finish(slice(tp, tp + til.ts), xs_ref[...], out_b[half:], gts, scs, shs)


def _merge(xp, xs, h, ya, yb, mod_p, mod_s, g_ffn, w_in, w_pa, w_pb, w_out, til):
    stage_rows = 128
    return pl.pallas_call(
        functools.partial(_merge_kernel, til=til),
        grid=(til.steps,),
        in_specs=[
            til.prompt(D_MODEL), til.sample_by_batch(D_MODEL),
            til.both(D_MODEL), til.both(CONV_DIM), til.both(SG_DIM),
            *til.mod_tables(MERGE_MODS),
            _resident((1, D_MODEL)),
            _HBM, _HBM, _HBM, _HBM,
        ],
        out_specs=[til.both(D_MODEL), til.both(D_MODEL)],
        out_shape=[
            jax.ShapeDtypeStruct((til.rows, D_MODEL), F32),
            jax.ShapeDtypeStruct((til.rows, D_MODEL), BF16),
        ],
        scratch_shapes=[
            pltpu.VMEM((D_MODEL, D_MODEL), BF16), pltpu.VMEM((D_MODEL, D_MODEL), BF16),
            pltpu.VMEM((CONV_DIM, D_MODEL), BF16), pltpu.VMEM((SG_DIM, D_MODEL), BF16),
            pltpu.VMEM((D_MODEL, D_MODEL), BF16),
            pltpu.VMEM((STAGE_SLOTS, stage_rows, D_MODEL), F32),
            pltpu.SemaphoreType.DMA((STAGE_SLOTS,)),
        ],
        compiler_params=_params(),
        name="merge",
    )(xp, xs, h, ya, yb, mod_p, mod_s, g_ffn, w_in, w_pa, w_pb, w_out)


def _ffn_in_kernel(h_ref, wg_ref, wu_ref, o_ref):
    h = h_ref[...]
    for c in range(0, wg_ref.shape[1], FFN_CHUNK):
        cols = slice(c, c + FFN_CHUNK)
        act = jax.nn.silu(_dot(h, wg_ref[:, cols].astype(BF16))) * _dot(h, wu_ref[:, cols].astype(BF16))
        o_ref[:, cols] = act.astype(BF16)


def _ffn_in(h2, w_ffn_in, tm, tn):
    rows = h2.shape[0]
    n_blocks = D_FF // tn
    assert rows % tm == 0 and D_FF % tn == 0
    return pl.pallas_call(
        _ffn_in_kernel,
        grid=(n_blocks, rows // tm),
        in_specs=[
            pl.BlockSpec((tm, D_MODEL), lambda j, i: (i, 0)),
            pl.BlockSpec((D_MODEL, tn), lambda j, i: (0, j)),
            pl.BlockSpec((D_MODEL, tn), lambda j, i: (0, j + n_blocks)),
        ],
        out_specs=pl.BlockSpec((tm, tn), lambda j, i: (i, j)),
        out_shape=jax.ShapeDtypeStruct((rows, D_FF), BF16),
        compiler_params=pltpu.CompilerParams(
            dimension_semantics=("arbitrary", "arbitrary"), vmem_limit_bytes=VMEM_LIMIT_BYTES),
        name="ffn_in",
    )(h2, w_ffn_in, w_ffn_in)


def _ffn_out_kernel(a_ref, x1_ref, mp_ref, ms_ref, gfin_ref, w_hbm, yp_ref, ys_ref,
                    w_ref, stage_ref, sem_ref, *, til):
    tp = til.tp
    half = tp // 2

    @pl.when(pl.program_id(0) == 0)
    def _():
        _load_bf16([(w_hbm, 0, w_ref)], stage_ref, sem_ref)

    gfin = gfin_ref[...]
    gtp, gts = til.mods(mp_ref, ms_ref, 0)
    out_a = _dot(a_ref[:half], w_ref[...])
    yp_ref[:half] = _rms(x1_ref[:half] + gtp * out_a, gfin)
    out_b = _dot(a_ref[half:], w_ref[...])
    yp_ref[half:] = _rms(x1_ref[half:tp] + gtp * out_b[:half], gfin)
    ys_ref[...] = _rms(x1_ref[tp:] + gts * out_b[half:], gfin)


def _ffn_out(act, x1, mod_p, mod_s, g_final, w_ffn_out, til, steps):
    stage_rows = 128
    return pl.pallas_call(
        functools.partial(_ffn_out_kernel, til=til),
        grid=(til.steps,),
        in_specs=[
            til.both(D_FF), til.both(D_MODEL),
            *til.mod_tables(FFN_OUT_MODS),
            _resident((1, D_MODEL)),
            _HBM,
        ],
        out_specs=[til.prompt(D_MODEL), til.sample_by_batch(D_MODEL)],
        out_shape=[
            jax.ShapeDtypeStruct((til.steps * til.tp, D_MODEL), F32),
            jax.ShapeDtypeStruct((til.n_sample, steps * D_MODEL), F32),
        ],
        scratch_shapes=[
            pltpu.VMEM((D_FF, D_MODEL), BF16),
            pltpu.VMEM((STAGE_SLOTS, stage_rows, D_MODEL), F32),
            pltpu.SemaphoreType.DMA((STAGE_SLOTS,)),
        ],
        compiler_params=_params(),
        name="ffn_out",
    )(act, x1, mod_p, mod_s, g_final, w_ffn_out)


def _lane_expand(per_group):
    return jnp.repeat(per_group, SG_HEAD, axis=1)


def kernel(x_prompt, x_sample, state_conv, c_prompt, c_sample, g_mix, g_ffn, w_ada, b_ada, w_in, w_conv, g_v,
           w_sg, b_sg, w_pa, w_pb, w_out, w_ffn_in, w_ffn_out, g_final):
    assert g_mix.shape[0] == 1, "one layer"
    batch, seq, _ = x_prompt.shape
    n_sample, steps, _ = x_sample.shape
    assert batch <= SUBLANES and seq % CHUNK == 0
    row = lambda v: v.reshape(1, -1)

    c_all = jnp.concatenate([c_prompt, jnp.zeros((SUBLANES - batch, D_MODEL), F32), c_sample], axis=0)
    mod_p, mod_s = _ada(c_all, w_ada[0], row(b_ada[0]), n_sample)
    mod_p = mod_p.reshape(N_MOD, SUBLANES, 1, D_MODEL)

    xp =x_prompt.reshape(batch * seq, D_MODEL)
    xs = x_sample.reshape(n_sample, steps * D_MODEL)
    state = state_conv[0].reshape(n_sample, (CONV_WIDTH - 1) * CONV_DIM)
    til = _Tiling(batch * seq, steps * n_sample, seq, n_sample, tp=256)

    bias_p = _lane_expand(b_sg[0][:, :CHUNK].T)
    w_sg_s = _lane_expand(w_sg[0][:, :steps, :steps].transpose(1, 2, 0).reshape(steps * steps, SG_GROUPS))
    w_sg_s = w_sg_s.reshape(steps * steps, 1, SG_DIM)
    bias_s = _lane_expand(b_sg[0][:, :steps].T).reshape(steps, 1, SG_DIM)
    h, ya, yb, zst_p, vst_p, z_s, vn_s = _mix(
        xp, xs, mod_p, mod_s, row(g_mix[0]), w_in[0], w_conv[0], row(g_v[0]), w_sg[0], bias_p, w_sg_s, bias_s,
        state, til, steps)

    x1, h2 = _merge(xp, xs, h, ya, yb, mod_p, mod_s, row(g_ffn[0]), w_in[0], w_pa[0], w_pb[0], w_out[0], til)
    act = _ffn_in(h2, w_ffn_in[0], tm=8 * (til.tp + til.ts), tn=512)
    y_p, y_s = _ffn_out(act, x1, mod_p, mod_s, row(g_final), w_ffn_out[0], til, steps)

    y_prompt = y_p.reshape(batch, seq, D_MODEL)
    y_sample = y_s.reshape(n_sample, steps, D_MODEL)
    conv_prompt = zst_p[:, SUBLANES - (CONV_WIDTH - 1):, :][None]
    conv_sample = z_s.reshape(n_sample, steps, CONV_DIM)[:, steps - (CONV_WIDTH - 1):, :][None]
    sgv_prompt = vst_p.reshape(1, batch, CHUNK, SG_GROUPS, SG_HEAD)
    sgv_sample = vn_s.reshape(1, n_sample, steps, SG_GROUPS, SG_HEAD)
    return (y_prompt, y_sample, conv_prompt, conv_sample, sgv_prompt, sgv_sample)
```

```python
import functools

import jax
import jax.numpy as jnp
from jax import lax
from jax.experimental import pallas as pl
from jax.experimental.pallas import tpu as pltpu

D_MODEL = 2048
CONV_DIM = D_MODEL // 2
CONV_WIDTH = 3
SG_DIM = D_MODEL // 2
SG_GROUPS = 8
SG_HEAD = SG_DIM // SG_GROUPS
CHUNK = 128
D_FF = 5632
N_MOD = 6
EPS = 1e-6
SEG = 1024
MIX_COLS = 3 * CONV_DIM + 2 * SG_DIM
SUBLANES = 8
FFN_CHUNK = 256
STAGE_SLOTS = 4
VMEM_LIMIT_BYTES = 60 * 1024 * 1024

MOD_ORDER = (1, 0, 5, 2, 4, 3)
MIX_MODS, FFN_OUT_MODS, MERGE_MODS = (0, 2), (2, 1), (3, 3)

F32 = jnp.float32
BF16 = jnp.bfloat16


def _dot(a, b):
    return jnp.dot(a, b, preferred_element_type=F32)


def _rms(x, gain):
    return x * lax.rsqrt(jnp.mean(x * x, axis=-1, keepdims=True) + EPS) * gain


def _modulated_norm(x, gain, scale, shift):
    return _rms(x, gain) * (1 + scale) + shift


def _rows(*parts):
    return jnp.concatenate(parts, axis=0)


def _load_bf16(jobs, stage_ref, sem_ref):
    n_slots, rows, n = stage_ref.shape
    chunks = [(src, col0, dst, r0) for src, col0, dst in jobs for r0 in range(0, dst.shape[0], rows)]
    assert all(dst.shape[1] == n and dst.shape[0] % rows == 0 for _, _, dst in jobs)

    def chunk_copy(idx):
        src, col0, _, r0 = chunks[idx]
        slot = idx % n_slots
        return pltpu.make_async_copy(src.at[pl.ds(r0, rows), pl.ds(col0, n)], stage_ref.at[slot], sem_ref.at[slot])

    ahead = n_slots - 1
    for idx in range(min(ahead, len(chunks))):
        chunk_copy(idx).start(priority=idx % 2)
    for idx, (_, _, dst, r0) in enumerate(chunks):
        if idx + ahead < len(chunks):
            chunk_copy(idx + ahead).start(priority=(idx + ahead) % 2)
        chunk_copy(idx).wait()
        dst[r0:r0 + rows, :] = stage_ref[idx % n_slots].astype(BF16)


def _ada_kernel(c_ref, w_ref, b_ref, mp_ref, ms_ref):
    a = jax.nn.silu(c_ref[...]).astype(BF16)
    r = _dot(a, w_ref[...].astype(BF16)) + b_ref[...]
    mp_ref[...] = r[:SUBLANES]
    ms_ref[...] = r[SUBLANES:]


def _ada(c_all, w_ada, b_ada, n_sample):
    tn = 1024
    per = D_MODEL // tn
    rows = c_all.shape[0]

    def src_block(j):
        comp = 0
        for slot, c in enumerate(MOD_ORDER):
            comp = jnp.where(j // per == slot, c, comp)
        return (0, comp * per + j % per)

    return pl.pallas_call(
        _ada_kernel,
        grid=(N_MOD * per,),
        in_specs=[
            pl.BlockSpec((rows, D_MODEL), lambda j: (0, 0)),
            pl.BlockSpec((D_MODEL, tn), src_block),
            pl.BlockSpec((1, tn), src_block),
        ],
        out_specs=[
            pl.BlockSpec((None, SUBLANES, tn), lambda j: (j // per, 0, j % per)),
            pl.BlockSpec((None, n_sample, tn), lambda j: (j // per, 0, j % per)),
        ],
        out_shape=[
            jax.ShapeDtypeStruct((N_MOD, SUBLANES, D_MODEL), F32),
            jax.ShapeDtypeStruct((N_MOD, n_sample, D_MODEL), F32),
        ],
        compiler_params=pltpu.CompilerParams(
            dimension_semantics=("arbitrary",), vmem_limit_bytes=VMEM_LIMIT_BYTES),
        name="ada",
    )(c_all, w_ada, b_ada)


class _Tiling:
    def __init__(self, rows_p, rows_s, seq, n_sample, tp):
        self.tp = tp
        self.steps = rows_p // tp
        self.ts = rows_s // self.steps
        self.rows = rows_p + rows_s
        self.tiles_per_seq = seq // tp
        self.sample_blocks = n_sample // self.ts
        self.n_sample = n_sample
        assert rows_p % tp == 0 and seq % tp == 0 and rows_s % self.steps == 0
        assert self.ts % 16 == 0 and n_sample % self.ts == 0

    def prompt(self, width):
        return pl.BlockSpec((self.tp, width), lambda i: (i, 0))

    def sample_by_batch(self, width):
        return pl.BlockSpec((self.ts, width), lambda i: (i % self.sample_blocks, i // self.sample_blocks))

    def both(self, width):
        return pl.BlockSpec((self.tp + self.ts, width), lambda i: (i, 0))

    def mod_tables(self, slots):
        first, count = slots
        assert first % count == 0
        return [
            pl.BlockSpec((count, SUBLANES, 1, D_MODEL), lambda i: (first // count, 0, 0, 0),
                         pipeline_mode=pl.Buffered(1)),
            pl.BlockSpec((count, self.n_sample, D_MODEL), lambda i: (first // count, 0, 0),
                         pipeline_mode=pl.Buffered(1)),
        ]

    def mods(self, mp_ref, ms_ref, k):
        i = pl.program_id(0)
        start = pl.multiple_of((i % self.sample_blocks) * self.ts, self.ts)
        return mp_ref[k, i // self.tiles_per_seq], ms_ref[k, pl.ds(start, self.ts), :]


def _resident(shape):
    return pl.BlockSpec(shape, lambda i: (0,) * len(shape), pipeline_mode=pl.Buffered(1))


_HBM = pl.BlockSpec(memory_space=pl.ANY)


def _params():
    return pltpu.CompilerParams(dimension_semantics=("arbitrary",), vmem_limit_bytes=VMEM_LIMIT_BYTES)


def _mix_kernel(xp_ref, xs_ref, mp_ref, ms_ref, gmix_ref, w_hbm, wconv_ref, gv_ref,
                wsg_ref, bsgp_ref, wsgs_ref, bsgs_ref, state_ref,
                h_ref, ya_ref, yb_ref, zstp_ref, vstp_ref, zs_ref, vs_ref,
                w_ref, stage_ref, sem_ref, carry_ref, zhist_ref, vhist_ref, *, til, steps):
    i = pl.program_id(0)
    tp, ts, n_sample = til.tp, til.ts, til.n_sample

    @pl.when(i == 0)
    def _():
        _load_bf16([(w_hbm, 0, w_ref)], stage_ref, sem_ref)
        for k in range(CONV_WIDTH - 1):
            zhist_ref[k * n_sample:(k + 1) * n_sample] = state_ref[:, k * CONV_DIM:(k + 1) * CONV_DIM]
        vhist_ref[...] = jnp.zeros_like(vhist_ref)

    @pl.when(i % til.tiles_per_seq == 0)
    def _():
        carry_ref[...] = jnp.zeros_like(carry_ref)

    gmix = gmix_ref[...]
    scp, scs = til.mods(mp_ref, ms_ref, 0)
    shp, shs = til.mods(mp_ref, ms_ref, 1)
    h = _rows(_modulated_norm(xp_ref[...], gmix, scp, shp),
              _modulated_norm(xs_ref[...], gmix, scs, shs)).astype(BF16)
    h_ref[...] = h
    proj = lambda k: _dot(h, w_ref[:, k * SEG:(k + 1) * SEG])
    base = pl.multiple_of(i * ts, ts)

    vn = _rms(jax.nn.gelu(proj(4)), gv_ref[...])
    z_all = proj(1) * proj(2)
    vstp_ref[...] = vn[tp - CHUNK:tp]
    vb = vn[:tp].astype(BF16)
    causal = (lax.broadcasted_iota(jnp.int32, (CHUNK, CHUNK), 0)
              >= lax.broadcasted_iota(jnp.int32, (CHUNK, CHUNK), 1))
    wgs = [jnp.where(causal, wsg_ref[g], 0.0).astype(BF16) for g in range(SG_GROUPS)]
    bias = bsgp_ref[...]
    chunks = []
    for c in range(tp // CHUNK):
        rows = slice(c * CHUNK, (c + 1) * CHUNK)
        parts = [_dot(wgs[g], vb[rows, g * SG_HEAD:(g + 1) * SG_HEAD]) for g in range(SG_GROUPS)]
        chunks.append(jnp.concatenate(parts, axis=1) + bias)
    gu = jax.nn.gelu(proj(3))
    b_gate = proj(0)
    yb_ref[:tp] = (gu[:tp] * _rows(*chunks)).astype(BF16)

    vs = vn[tp:]
    vs_ref[...] = vs
    block = i % til.sample_blocks
    t = i // til.sample_blocks
    vhist_ref[pl.ds(base, ts), :] = vs
    sp = bsgs_ref[t]
    for s in range(steps):
        w_ts = jnp.where(s <= t, wsgs_ref[t * steps + s], 0.0)
        sp = sp + w_ts * vhist_ref[pl.ds(pl.multiple_of(block * ts, ts) + s * n_sample, ts), :]
    yb_ref[tp:] = (gu[tp:] * sp).astype(BF16)

    wc = wconv_ref[...]
    z = z_all[:tp]
    carry = carry_ref[...]
    prev2, prev1 = carry[SUBLANES - 2:SUBLANES - 1], carry[SUBLANES - 1:SUBLANES]
    row = lax.broadcasted_iota(jnp.int32, (SUBLANES, CONV_DIM), 0)
    z1 = pltpu.roll(z, 1, 0)
    z2 = pltpu.roll(z, 2, 0)
    z1 = _rows(jnp.where(row == 0, prev1, z1[:SUBLANES]), z1[SUBLANES:])
    z2 = _rows(jnp.where(row == 0, prev2, jnp.where(row == 1, prev1, z2[:SUBLANES])), z2[SUBLANES:])
    conv = wc[0:1] * z2 + wc[1:2] * z1 + wc[2:3] * z
    ya_ref[:tp] = (b_gate[:tp] * conv).astype(BF16)
    carry_ref[...] = z[tp - SUBLANES:]
    zstp_ref[...] = z[tp - SUBLANES:]

    hist = lambda ref, k: ref[pl.ds(base + k * n_sample, ts), :]
    zs = z_all[tp:]
    zhist_ref[pl.ds(base + (CONV_WIDTH - 1) * n_sample, ts), :] = zs
    conv_s = wc[0:1] * hist(zhist_ref, 0) + wc[1:2] * hist(zhist_ref, 1) + wc[2:3] * zs
    ya_ref[tp:] = (b_gate[tp:] * conv_s).astype(BF16)
    zs_ref[...] = zs


def _mix(xp, xs, mod_p, mod_s, g_mix, w_in, w_conv, g_v, w_sg, bias_p, w_sg_s, bias_s, state, til, steps):
    n_sample = til.n_sample
    rows_s = n_sample * steps
    n_seq = til.steps // til.tiles_per_seq
    per_seq = lambda rows, width: pl.BlockSpec((None, rows, width), lambda i: (i // til.tiles_per_seq, 0, 0))
    stage_rows = 64
    return pl.pallas_call(
        functools.partial(_mix_kernel, til=til, steps=steps),
        grid=(til.steps,),
        in_specs=[
            til.prompt(D_MODEL), til.sample_by_batch(D_MODEL), *til.mod_tables(MIX_MODS),
            _resident((1, D_MODEL)),
            _HBM,
            _resident((CONV_WIDTH, CONV_DIM)),
            _resident((1, SG_DIM)),
            _resident((SG_GROUPS, CHUNK, CHUNK)),
            _resident((CHUNK, SG_DIM)),
            _resident((steps * steps, 1, SG_DIM)),
            _resident((steps, 1, SG_DIM)),
            _resident((n_sample, (CONV_WIDTH - 1) * CONV_DIM)),
        ],
        out_specs=[
            til.both(D_MODEL), til.both(CONV_DIM), til.both(SG_DIM),
            per_seq(SUBLANES, CONV_DIM), per_seq(CHUNK, SG_DIM),
            til.sample_by_batch(CONV_DIM), til.sample_by_batch(SG_DIM),
        ],
        out_shape=[
            jax.ShapeDtypeStruct((til.rows, D_MODEL), BF16),
            jax.ShapeDtypeStruct((til.rows, CONV_DIM), BF16),
            jax.ShapeDtypeStruct((til.rows, SG_DIM), BF16),
            jax.ShapeDtypeStruct((n_seq, SUBLANES, CONV_DIM), F32),
            jax.ShapeDtypeStruct((n_seq, CHUNK, SG_DIM), F32),
            jax.ShapeDtypeStruct((n_sample, steps * CONV_DIM), F32),
            jax.ShapeDtypeStruct((n_sample, steps * SG_DIM), F32),
        ],
        scratch_shapes=[
            pltpu.VMEM((D_MODEL, MIX_COLS), BF16),
            pltpu.VMEM((STAGE_SLOTS, stage_rows, MIX_COLS), F32),
            pltpu.SemaphoreType.DMA((STAGE_SLOTS,)),
            pltpu.VMEM((SUBLANES, CONV_DIM), F32),
            pltpu.VMEM(((CONV_WIDTH - 1) * n_sample + rows_s, CONV_DIM), F32),
            pltpu.VMEM((rows_s, SG_DIM), F32),
        ],
        compiler_params=_params(),
        name="mix",
    )(xp, xs, mod_p, mod_s, g_mix, w_in, w_conv, g_v, w_sg, bias_p, w_sg_s, bias_s, state)


def _merge_kernel(xp_ref, xs_ref, h_ref, ya_ref, yb_ref, mp_ref, ms_ref, gffn_ref,
                  win_hbm, wpa_hbm, wpb_hbm, wout_hbm,
                  x1_ref, h2_ref,
                  wga_ref, wgb_ref, wpa_ref, wpb_ref, wout_ref, stage_ref, sem_ref, *, til):
    tp = til.tp
    half = tp // 2

    @pl.when(pl.program_id(0) == 0)
    def _():
        _load_bf16([(win_hbm, MIX_COLS, wga_ref), (win_hbm, MIX_COLS + D_MODEL, wgb_ref), (wpa_hbm, 0, wpa_ref),
                    (wpb_hbm, 0, wpb_ref), (wout_hbm, 0, wout_ref)], stage_ref, sem_ref)

    h = h_ref[...]
    ya = ya_ref[...]
    yb = yb_ref[...]
    halves = []
    for k in range(D_MODEL // SEG):
        cols = slice(k * SEG, (k + 1) * SEG)
        branch_a = jax.nn.sigmoid(_dot(h, wga_ref[:, cols])) * _dot(ya, wpa_ref[:, cols])
        branch_b = jax.nn.sigmoid(_dot(h, wgb_ref[:, cols])) * _dot(yb, wpb_ref[:, cols])
        halves.append((branch_a + branch_b).astype(BF16))
    merged = jnp.concatenate(halves, axis=1)

    gffn = gffn_ref[...]
    gtp, gts = til.mods(mp_ref, ms_ref, 0)
    scp, scs = til.mods(mp_ref, ms_ref, 1)
    shp, shs = til.mods(mp_ref, ms_ref, 2)

    def finish(rows, x, out, gate, scale, shift):
        x1 = x + gate * out
        x1_ref[rows] = x1
        h2_ref[rows] = _modulated_norm(x1, gffn, scale, shift).astype(BF16)

    out_a = _dot(merged[:half], wout_ref[...])
    finish(slice(0, half), xp_ref[:half], out_a, gtp, scp, shp)
    out_b = _dot(merged[half:], wout_ref[...])
    finish(slice(half, tp), xp_ref[half:], out_b[:half], gtp, scp, shp)
    finish(slice(tp, tp + til.ts), xs_ref[...], out_b[half:], gts, scs, shs)


def _merge(xp, xs, h, ya, yb, mod_p, mod_s, g_ffn, w_in, w_pa, w_pb, w_out, til):
    stage_rows = 128
    return pl.pallas_call(
        functools.partial(_merge_kernel, til=til),
        grid=(til.steps,),
        in_specs=[
            til.prompt(D_MODEL), til.sample_by_batch(D_MODEL),
            til.both(D_MODEL), til.both(CONV_DIM), til.both(SG_DIM),
            *til.mod_tables(MERGE_MODS),
            _resident((1, D_MODEL)),
            _HBM, _HBM, _HBM, _HBM,
        ],
        out_specs=[til.both(D_MODEL), til.both(D_MODEL)],
        out_shape=[
            jax.ShapeDtypeStruct((til.rows, D_MODEL), F32),
            jax.ShapeDtypeStruct((til.rows, D_MODEL), BF16),
        ],
        scratch_shapes=[
            pltpu.VMEM((D_MODEL, D_MODEL), BF16), pltpu.VMEM((D_MODEL, D_MODEL), BF16),
            pltpu.VMEM((CONV_DIM, D_MODEL), BF16), pltpu.VMEM((SG_DIM, D_MODEL), BF16),
            pltpu.VMEM((D_MODEL, D_MODEL), BF16),
            pltpu.VMEM((STAGE_SLOTS, stage_rows, D_MODEL), F32),
            pltpu.SemaphoreType.DMA((STAGE_SLOTS,)),
        ],
        compiler_params=_params(),
        name="merge",
    )(xp, xs, h, ya, yb, mod_p, mod_s, g_ffn, w_in, w_pa, w_pb, w_out)


def _ffn_in_kernel(h_ref, wg_ref, wu_ref, o_ref):
    h = h_ref[...]
    for c in range(0, wg_ref.shape[1], FFN_CHUNK):
        cols = slice(c, c + FFN_CHUNK)
        act = jax.nn.silu(_dot(h, wg_ref[:, cols].astype(BF16))) * _dot(h, wu_ref[:, cols].astype(BF16))
        o_ref[:, cols] = act.astype(BF16)


def _ffn_in(h2, w_ffn_in, tm, tn):
    rows = h2.shape[0]
    n_blocks = D_FF // tn
    assert rows % tm == 0 and D_FF % tn == 0
    return pl.pallas_call(
        _ffn_in_kernel,
        grid=(n_blocks, rows // tm),
        in_specs=[
            pl.BlockSpec((tm, D_MODEL), lambda j, i: (i, 0)),
            pl.BlockSpec((D_MODEL, tn), lambda j, i: (0, j)),
            pl.BlockSpec((D_MODEL, tn), lambda j, i: (0, j + n_blocks)),
        ],
        out_specs=pl.BlockSpec((tm, tn), lambda j, i: (i, j)),
        out_shape=jax.ShapeDtypeStruct((rows, D_FF), BF16),
        compiler_params=pltpu.CompilerParams(
            dimension_semantics=("arbitrary", "arbitrary"), vmem_limit_bytes=VMEM_LIMIT_BYTES),
        name="ffn_in",
    )(h2, w_ffn_in, w_ffn_in)


def _ffn_out_kernel(a_ref, x1_ref, mp_ref, ms_ref, gfin_ref, w_hbm, yp_ref, ys_ref,
                    w_ref, stage_ref, sem_ref, *, til):
    tp = til.tp
    half = tp // 2

    @pl.when(pl.program_id(0) == 0)
    def _():
        _load_bf16([(w_hbm, 0, w_ref)], stage_ref, sem_ref)

    gfin = gfin_ref[...]
    gtp, gts = til.mods(mp_ref, ms_ref, 0)
    out_a = _dot(a_ref[:half], w_ref[...])
    yp_ref[:half] = _rms(x1_ref[:half] + gtp * out_a, gfin)
    out_b = _dot(a_ref[half:], w_ref[...])
    yp_ref[half:] = _rms(x1_ref[half:tp] + gtp * out_b[:half], gfin)
    ys_ref[...] = _rms(x1_ref[tp:] + gts * out_b[half:], gfin)


def _ffn_out(act, x1, mod_p, mod_s, g_final, w_ffn_out, til, steps):
    stage_rows = 128
    return pl.pallas_call(
        functools.partial(_ffn_out_kernel, til=til),
        grid=(til.steps,),
        in_specs=[
            til.both(D_FF), til.both(D_MODEL),
            *til.mod_tables(FFN_OUT_MODS),
            _resident((1, D_MODEL)),
            _HBM,
        ],
        out_specs=[til.prompt(D_MODEL), til.sample_by_batch(D_MODEL)],
        out_shape=[
            jax.ShapeDtypeStruct((til.steps * til.tp, D_MODEL), F32),
            jax.ShapeDtypeStruct((til.n_sample, steps * D_MODEL), F32),
        ],
        scratch_shapes=[
            pltpu.VMEM((D_FF, D_MODEL), BF16),
            pltpu.VMEM((STAGE_SLOTS, stage_rows, D_MODEL), F32),
            pltpu.SemaphoreType.DMA((STAGE_SLOTS,)),
        ],
        compiler_params=_params(),
        name="ffn_out",
    )(act, x1, mod_p, mod_s, g_final, w_ffn_out)


def _lane_expand(per_group):
    return jnp.repeat(per_group, SG_HEAD, axis=1)


def kernel(x_prompt, x_sample, state_conv, c_prompt, c_sample, g_mix, g_ffn, w_ada, b_ada, w_in, w_conv, g_v,
           w_sg, b_sg, w_pa, w_pb, w_out, w_ffn_in, w_ffn_out, g_final):
    assert g_mix.shape[0] == 1, "one layer"
    batch, seq, _ = x_prompt.shape
    n_sample, steps, _ = x_sample.shape
    assert batch <= SUBLANES and seq % CHUNK == 0
    row = lambda v: v.reshape(1, -1)

    c_all = jnp.concatenate([c_prompt, jnp.zeros((SUBLANES - batch, D_MODEL), F32), c_sample], axis=0)
    mod_p, mod_s = _ada(c_all, w_ada[0], row(b_ada[0]), n_sample)
    mod_p = mod_p.reshape(N_MOD, SUBLANES, 1, D_MODEL)

    xp =x_prompt.reshape(batch * seq, D_MODEL)
    xs = x_sample.reshape(n_sample, steps * D_MODEL)
    state = state_conv[0].reshape(n_sample, (CONV_WIDTH - 1) * CONV_DIM)
    til = _Tiling(batch * seq, steps * n_sample, seq, n_sample, tp=256)

    bias_p = _lane_expand(b_sg[0][:, :CHUNK].T)
    w_sg_s = _lane_expand(w_sg[0][:, :steps, :steps].transpose(1, 2, 0).reshape(steps * steps, SG_GROUPS))
    w_sg_s = w_sg_s.reshape(steps * steps, 1, SG_DIM)
    bias_s = _lane_expand(b_sg[0][:, :steps].T).reshape(steps, 1, SG_DIM)
    h, ya, yb, zst_p, vst_p, z_s, vn_s = _mix(
        xp, xs, mod_p, mod_s, row(g_mix[0]), w_in[0], w_conv[0], row(g_v[0]), w_sg[0], bias_p, w_sg_s, bias_s,
        state, til, steps)

    x1, h2 = _merge(xp, xs, h, ya, yb, mod_p, mod_s, row(g_ffn[0]), w_in[0], w_pa[0], w_pb[0], w_out[0], til)
    act = _ffn_in(h2, w_ffn_in[0], tm=8 * (til.tp + til.ts), tn=512)
    y_p, y_s = _ffn_out(act, x1, mod_p, mod_s, row(g_final), w_ffn_out[0], til, steps)

    y_prompt = y_p.reshape(batch, seq, D_MODEL)
    y_sample = y_s.reshape(n_sample, steps, D_MODEL)
    conv_prompt = zst_p[:, SUBLANES - (CONV_WIDTH - 1):, :][None]
    conv_sample = z_s.reshape(n_sample, steps, CONV_DIM)[:, steps - (CONV_WIDTH - 1):, :][None]
    sgv_prompt = vst_p.reshape(1, batch, CHUNK, SG_GROUPS, SG_HEAD)
    sgv_sample = vn_s.reshape(1, n_sample, steps, SG_GROUPS, SG_HEAD)
    return (y_prompt, y_sample, conv_prompt, conv_sample, sgv_prompt, sgv_sample)
```

```python
import functools

import jax
import jax.numpy as jnp
from jax import lax
from jax.experimental import pallas as pl
from jax.experimental.pallas import tpu as pltpu

D_MODEL = 2048
CONV_DIM = D_MODEL // 2
CONV_WIDTH = 3
SG_DIM = D_MODEL // 2
SG_GROUPS = 8
SG_HEAD = SG_DIM // SG_GROUPS
CHUNK = 128
D_FF = 5632
N_MOD = 6
EPS = 1e-6
SEG = 1024
MIX_COLS = 3 * CONV_DIM + 2 * SG_DIM
SUBLANES = 8
FFN_CHUNK = 256
STAGE_SLOTS = 4
VMEM_LIMIT_BYTES = 60 * 1024 * 1024

MOD_ORDER = (1, 0, 2, 4, 3, 5)
ADA_SLOTS, SIDE_SLOTS = (0, 2), (2, 4)
MIX_MODS, MERGE_MODS, FFN_OUT_MODS = (0, 2), (0, 3), (3, 1)
SIDE_TN = 256

F32 = jnp.float32
BF16 = jnp.bfloat16


def _dot(a, b):
    return jnp.dot(a, b, preferred_element_type=F32)


def _rms(x, gain):
    return x * lax.rsqrt(jnp.mean(x * x, axis=-1, keepdims=True) + EPS) * gain


def _modulated_norm(x, gain, scale, shift):
    return _rms(x, gain) * (1 + scale) + shift


def _rows(*parts):
    return jnp.concatenate(parts, axis=0)


def _load_bf16(jobs, stage_ref, sem_ref):
    n_slots, rows, n = stage_ref.shape
    chunks = [(src, col0, dst, r0) for src, col0, dst in jobs for r0 in range(0, dst.shape[0], rows)]
    assert all(dst.shape[1] == n and dst.shape[0] % rows == 0 for _, _, dst in jobs)

    def chunk_copy(idx):
        src, col0, _, r0 = chunks[idx]
        slot = idx % n_slots
        return pltpu.make_async_copy(src.at[pl.ds(r0, rows), pl.ds(col0, n)], stage_ref.at[slot], sem_ref.at[slot])

    ahead = n_slots - 1
    for idx in range(min(ahead, len(chunks))):
        chunk_copy(idx).start()
    for idx, (_, _, dst, r0) in enumerate(chunks):
        if idx + ahead < len(chunks):
            chunk_copy(idx + ahead).start()
        chunk_copy(idx).wait()
        dst[r0:r0 + rows, :] = stage_ref[idx % n_slots].astype(BF16)


def _ada_columns(first_slot, tn):
    per = D_MODEL // tn

    def index_map(j):
        comp = 0
        for slot, c in enumerate(MOD_ORDER):
            comp = jnp.where(first_slot + j // per == slot, c, comp)
        return (0, comp * per + j % per)

    return index_map


def _ada_table_specs(n_sample, tn):
    per = D_MODEL // tn
    return [
        pl.BlockSpec((None, SUBLANES, tn), lambda j: (j // per, 0, j % per)),
        pl.BlockSpec((None, n_sample, tn), lambda j: (j // per, 0, j % per)),
    ]


def _ada_block(a, w_ref, b_ref, mp_ref, ms_ref):
    r = _dot(a, w_ref[...].astype(BF16)) + b_ref[...]
    mp_ref[...] = r[:SUBLANES]
    ms_ref[...] = r[SUBLANES:SUBLANES + ms_ref.shape[0]]


def _ada_kernel(c_ref, w_ref, b_ref, a_ref, mp_ref, ms_ref):
    a = jax.nn.silu(c_ref[...]).astype(BF16)
    a_ref[...] = a
    _ada_block(a, w_ref, b_ref, mp_ref, ms_ref)


def _ada(c_all, w_ada, b_ada, n_sample, slots):
    tn = 1024
    first, count = slots
    rows = c_all.shape[0]
    return pl.pallas_call(
        _ada_kernel,
        grid=(count * (D_MODEL // tn),),
        in_specs=[
            pl.BlockSpec((rows, D_MODEL), lambda j: (0, 0)),
            pl.BlockSpec((D_MODEL, tn), _ada_columns(first, tn)),
            pl.BlockSpec((1, tn), _ada_columns(first, tn)),
        ],
        out_specs=[pl.BlockSpec((rows, D_MODEL), lambda j: (0, 0)), *_ada_table_specs(n_sample, tn)],
        out_shape=[
            jax.ShapeDtypeStruct((rows, D_MODEL), BF16),
            jax.ShapeDtypeStruct((count, SUBLANES, D_MODEL), F32),
            jax.ShapeDtypeStruct((count, n_sample, D_MODEL), F32),
        ],
        compiler_params=pltpu.CompilerParams(
            dimension_semantics=("arbitrary",), vmem_limit_bytes=VMEM_LIMIT_BYTES),
        name="ada",
    )(c_all, w_ada, b_ada)


class _Tiling:
    def __init__(self, rows_p, rows_s, seq, n_sample, tp):
        self.tp = tp
        self.steps = rows_p // tp
        self.ts = rows_s // self.steps
        self.rows = rows_p + rows_s
        self.tiles_per_seq = seq // tp
        self.sample_blocks = n_sample // self.ts
        self.n_sample = n_sample
        assert rows_p % tp == 0 and seq % tp == 0 and rows_s % self.steps == 0
        assert self.ts % 16 == 0 and n_sample % self.ts == 0

    def prompt(self, width):
        return pl.BlockSpec((self.tp, width), lambda i: (i, 0))

    def sample_by_batch(self, width):
        return pl.BlockSpec((self.ts, width), lambda i: (i % self.sample_blocks, i // self.sample_blocks))

    def both(self, width):
        return pl.BlockSpec((self.tp + self.ts, width), lambda i: (i, 0))

    def mod_tables(self, slots):
        first, count = slots
        assert first % count == 0
        return [
            pl.BlockSpec((count, SUBLANES, 1, D_MODEL), lambda i: (first // count, 0, 0, 0),
                         pipeline_mode=pl.Buffered(1)),
            pl.BlockSpec((count, self.n_sample, D_MODEL), lambda i: (first // count, 0, 0),
                         pipeline_mode=pl.Buffered(1)),
        ]

    def mods(self, mp_ref, ms_ref, k):
        i = pl.program_id(0)
        start = pl.multiple_of((i % self.sample_blocks) * self.ts, self.ts)
        return mp_ref[k, i // self.tiles_per_seq], ms_ref[k, pl.ds(start, self.ts), :]


def _resident(shape):
    return pl.BlockSpec(shape, lambda i: (0,) * len(shape), pipeline_mode=pl.Buffered(1))


_HBM = pl.BlockSpec(memory_space=pl.ANY)


def _params():
    return pltpu.CompilerParams(dimension_semantics=("arbitrary",), vmem_limit_bytes=VMEM_LIMIT_BYTES)


def _mix_kernel(xp_ref, xs_ref, mp_ref, ms_ref, gmix_ref, w_hbm, wconv_ref, gv_ref,
                wsg_ref, bsgp_ref, wsgs_ref, bsgs_ref, state_ref,
                a_ref, wada_ref, bada_ref, wg0_ref, wg1_ref, wg2_ref, wg3_ref, wpa_ref, wpb_ref, wout_ref,
                h_ref, ya_ref, yb_ref, zstp_ref, vstp_ref, zs_ref, vs_ref,
                mpr_ref, msr_ref, wgate_b_ref, wpa_b_ref, wpb_b_ref, wout_b_ref,
                w_ref, stage_ref, sem_ref, carry_ref, zhist_ref, vhist_ref, *, til, steps):
    i = pl.program_id(0)
    tp, ts, n_sample = til.tp, til.ts, til.n_sample

    @pl.when(i == 0)
    def _():
        _load_bf16([(w_hbm, 0, w_ref)], stage_ref, sem_ref)
        for k in range(CONV_WIDTH - 1):
            zhist_ref[k * n_sample:(k + 1) * n_sample] = state_ref[:, k * CONV_DIM:(k + 1) * CONV_DIM]
        vhist_ref[...] = jnp.zeros_like(vhist_ref)

    @pl.when(i % til.tiles_per_seq == 0)
    def _():
        carry_ref[...] = jnp.zeros_like(carry_ref)

    gmix = gmix_ref[...]
    scp, scs = til.mods(mp_ref, ms_ref, 0)
    shp, shs = til.mods(mp_ref, ms_ref, 1)
    h = _rows(_modulated_norm(xp_ref[...], gmix, scp, shp),
              _modulated_norm(xs_ref[...], gmix, scs, shs)).astype(BF16)
    h_ref[...] = h

    _ada_block(a_ref[...], wada_ref, bada_ref, mpr_ref, msr_ref)
    for k, wg_ref in enumerate((wg0_ref, wg1_ref, wg2_ref, wg3_ref)):
        wgate_b_ref[:, k * SEG:(k + 1) * SEG] = wg_ref[...].astype(BF16)
    wpa_b_ref[...] = wpa_ref[...].astype(BF16)
    wpb_b_ref[...] = wpb_ref[...].astype(BF16)
    wout_b_ref[...] = wout_ref[...].astype(BF16)
    proj = lambda k: _dot(h, w_ref[:, k * SEG:(k + 1) * SEG])
    base = pl.multiple_of(i * ts, ts)

    vn = _rms(jax.nn.gelu(proj(4)), gv_ref[...])
    z_all = proj(1) * proj(2)
    vstp_ref[...] = vn[tp - CHUNK:tp]
    vb = vn[:tp].astype(BF16)
    causal = (lax.broadcasted_iota(jnp.int32, (CHUNK, CHUNK), 0)
              >= lax.broadcasted_iota(jnp.int32, (CHUNK, CHUNK), 1))
    wgs = [jnp.where(causal, wsg_ref[g], 0.0).astype(BF16) for g in range(SG_GROUPS)]
    bias = bsgp_ref[...]
    chunks = []
    for c in range(tp // CHUNK):
        rows = slice(c * CHUNK, (c + 1) * CHUNK)
        parts = [_dot(wgs[g], vb[rows, g * SG_HEAD:(g + 1) * SG_HEAD]) for g in range(SG_GROUPS)]
        chunks.append(jnp.concatenate(parts, axis=1) + bias)
    gu = jax.nn.gelu(proj(3))
    b_gate = proj(0)
    yb_ref[:tp] = (gu[:tp] * _rows(*chunks)).astype(BF16)

    vs = vn[tp:]
    vs_ref[...] = vs
    block = i % til.sample_blocks
    t = i // til.sample_blocks
    vhist_ref[pl.ds(base, ts), :] = vs
    sp = bsgs_ref[t]
    for s in range(steps):
        w_ts = jnp.where(s <= t, wsgs_ref[t * steps + s], 0.0)
        sp = sp + w_ts * vhist_ref[pl.ds(pl.multiple_of(block * ts, ts) + s * n_sample, ts), :]
    yb_ref[tp:] = (gu[tp:] * sp).astype(BF16)

    wc = wconv_ref[...]
    z = z_all[:tp]
    carry = carry_ref[...]
    prev2, prev1 = carry[SUBLANES - 2:SUBLANES - 1], carry[SUBLANES - 1:SUBLANES]
    row = lax.broadcasted_iota(jnp.int32, (SUBLANES, CONV_DIM), 0)
    z1 = pltpu.roll(z, 1, 0)
    z2 = pltpu.roll(z, 2, 0)
    z1 = _rows(jnp.where(row == 0, prev1, z1[:SUBLANES]), z1[SUBLANES:])
    z2 = _rows(jnp.where(row == 0, prev2, jnp.where(row == 1, prev1, z2[:SUBLANES])), z2[SUBLANES:])
    conv = wc[0:1] * z2 + wc[1:2] * z1 + wc[2:3] * z
    ya_ref[:tp] = (b_gate[:tp] * conv).astype(BF16)
    carry_ref[...] = z[tp - SUBLANES:]
    zstp_ref[...] = z[tp - SUBLANES:]

    hist = lambda ref, k: ref[pl.ds(base + k * n_sample, ts), :]
    zs = z_all[tp:]
    zhist_ref[pl.ds(base + (CONV_WIDTH - 1) * n_sample, ts), :] = zs
    conv_s = wc[0:1] * hist(zhist_ref, 0) + wc[1:2] * hist(zhist_ref, 1) + wc[2:3] * zs
    ya_ref[tp:] = (b_gate[tp:] * conv_s).astype(BF16)
    zs_ref[...] = zs


def _row_slab(rows_total, n_steps, width):
    assert rows_total % (16 * n_steps) == 0
    return pl.BlockSpec((rows_total // n_steps, width), lambda i: (i, 0))


def _mix(xp, xs, mod_p, mod_s, g_mix, w_in, w_conv, g_v, w_sg, bias_p, w_sg_s, bias_s, state,
         a, w_ada, b_ada, w_pa, w_pb, w_out, til, steps):
    n_sample = til.n_sample
    rows_s = n_sample * steps
    n_seq = til.steps // til.tiles_per_seq
    per_seq = lambda rows, width: pl.BlockSpec((None, rows, width), lambda i: (i // til.tiles_per_seq, 0, 0))
    stage_rows = 32
    n_side = SIDE_SLOTS[1]
    assert til.steps * SIDE_TN == n_side * D_MODEL, "one modulation-table column block per grid step"
    slab = lambda k, width: _row_slab(k, til.steps, width)
    return pl.pallas_call(
        functools.partial(_mix_kernel, til=til, steps=steps),
        grid=(til.steps,),
        in_specs=[
            til.prompt(D_MODEL), til.sample_by_batch(D_MODEL), *til.mod_tables(MIX_MODS),
            _resident((1, D_MODEL)),
            _HBM,
            _resident((CONV_WIDTH, CONV_DIM)),
            _resident((1, SG_DIM)),
            _resident((SG_GROUPS, CHUNK, CHUNK)),
            _resident((CHUNK, SG_DIM)),
            _resident((steps * steps, 1, SG_DIM)),
            _resident((steps, 1, SG_DIM)),
            _resident((n_sample, (CONV_WIDTH - 1) * CONV_DIM)),
            _resident(a.shape),
            pl.BlockSpec((D_MODEL, SIDE_TN), _ada_columns(SIDE_SLOTS[0], SIDE_TN)),
            pl.BlockSpec((1, SIDE_TN), _ada_columns(SIDE_SLOTS[0], SIDE_TN)),
            *[pl.BlockSpec((D_MODEL // til.steps, SEG), lambda i, k=k: (i, MIX_COLS // SEG + k))
              for k in range(2 * D_MODEL // SEG)],
            slab(CONV_DIM, D_MODEL), slab(SG_DIM, D_MODEL), slab(D_MODEL, D_MODEL),
        ],
        out_specs=[
            til.both(D_MODEL), til.both(CONV_DIM), til.both(SG_DIM),
            per_seq(SUBLANES, CONV_DIM), per_seq(CHUNK, SG_DIM),
            til.sample_by_batch(CONV_DIM), til.sample_by_batch(SG_DIM),
            *_ada_table_specs(n_sample, SIDE_TN),
            slab(D_MODEL, 2 * D_MODEL), slab(CONV_DIM, D_MODEL), slab(SG_DIM, D_MODEL), slab(D_MODEL, D_MODEL),
        ],
        out_shape=[
            jax.ShapeDtypeStruct((til.rows, D_MODEL), BF16),
            jax.ShapeDtypeStruct((til.rows, CONV_DIM), BF16),
            jax.ShapeDtypeStruct((til.rows, SG_DIM), BF16),
            jax.ShapeDtypeStruct((n_seq, SUBLANES, CONV_DIM), F32),
            jax.ShapeDtypeStruct((n_seq, CHUNK, SG_DIM), F32),
            jax.ShapeDtypeStruct((n_sample, steps * CONV_DIM), F32),
            jax.ShapeDtypeStruct((n_sample, steps * SG_DIM), F32),
            jax.ShapeDtypeStruct((n_side, SUBLANES, D_MODEL), F32),
            jax.ShapeDtypeStruct((n_side, n_sample, D_MODEL), F32),
            jax.ShapeDtypeStruct((D_MODEL, 2 * D_MODEL), BF16),
            jax.ShapeDtypeStruct((CONV_DIM, D_MODEL), BF16),
            jax.ShapeDtypeStruct((SG_DIM, D_MODEL), BF16),
            jax.ShapeDtypeStruct((D_MODEL, D_MODEL), BF16),
        ],
        scratch_shapes=[
            pltpu.VMEM((D_MODEL, MIX_COLS), BF16),
            pltpu.VMEM((STAGE_SLOTS, stage_rows, MIX_COLS), F32),
            pltpu.SemaphoreType.DMA((STAGE_SLOTS,)),
            pltpu.VMEM((SUBLANES, CONV_DIM), F32),
            pltpu.VMEM(((CONV_WIDTH - 1) * n_sample + rows_s, CONV_DIM), F32),
            pltpu.VMEM((rows_s, SG_DIM), F32),
        ],
        compiler_params=_params(),
        name="mix",
    )(xp, xs, mod_p, mod_s, g_mix, w_in, w_conv, g_v, w_sg, bias_p, w_sg_s, bias_s, state,
      a, w_ada, b_ada, w_in, w_in, w_in, w_in, w_pa, w_pb, w_out)


def _merge_kernel(xp_ref, xs_ref, h_ref, ya_ref, yb_ref, mp_ref, ms_ref, gffn_ref,
                  wgate_ref, wpa_ref, wpb_ref, wout_ref, x1_ref, h2_ref, *, til):
    tp = til.tp
    half = tp // 2
    h = h_ref[...]
    ya = ya_ref[...]
    yb = yb_ref[...]
    halves = []
    for k in range(D_MODEL // SEG):
        cols = slice(k * SEG, (k + 1) * SEG)
        gate_b_cols = slice(D_MODEL + k * SEG, D_MODEL + (k + 1) * SEG)
        branch_a = jax.nn.sigmoid(_dot(h, wgate_ref[:, cols])) * _dot(ya, wpa_ref[:, cols])
        branch_b = jax.nn.sigmoid(_dot(h, wgate_ref[:, gate_b_cols])) * _dot(yb, wpb_ref[:, cols])
        halves.append((branch_a + branch_b).astype(BF16))
    merged = jnp.concatenate(halves, axis=1)

    gffn = gffn_ref[...]
    gtp, gts = til.mods(mp_ref, ms_ref, 0)
    scp, scs = til.mods(mp_ref, ms_ref, 1)
    shp, shs = til.mods(mp_ref, ms_ref, 2)

    def finish(rows, x, out, gate, scale, shift):
        x1 = x + gate * out
        x1_ref[rows] = x1
        h2_ref[rows] = _modulated_norm(x1, gffn, scale, shift).astype(BF16)

    out_a = _dot(merged[:half], wout_ref[...])
    finish(slice(0, half), xp_ref[:half], out_a, gtp, scp, shp)
    out_b = _dot(merged[half:], wout_ref[...])
    finish(slice(half, tp), xp_ref[half:], out_b[:half], gtp, scp, shp)
    finish(slice(tp, tp + til.ts), xs_ref[...], out_b[half:], gts, scs, shs)


def _merge(xp, xs, h, ya, yb, mod_p, mod_s, g_ffn, w_gate, w_pa, w_pb, w_out, til):
    return pl.pallas_call(
        functools.partial(_merge_kernel, til=til),
        grid=(til.steps,),
        in_specs=[
            til.prompt(D_MODEL), til.sample_by_batch(D_MODEL),
            til.both(D_MODEL), til.both(CONV_DIM), til.both(SG_DIM),
            *til.mod_tables(MERGE_MODS),
            _resident((1, D_MODEL)),
            _resident(w_gate.shape), _resident(w_pa.shape), _resident(w_pb.shape), _resident(w_out.shape),
        ],
        out_specs=[til.both(D_MODEL), til.both(D_MODEL)],
        out_shape=[
            jax.ShapeDtypeStruct((til.rows, D_MODEL), F32),
            jax.ShapeDtypeStruct((til.rows, D_MODEL), BF16),
        ],
        compiler_params=_params(),
        name="merge",
    )(xp, xs, h, ya, yb, mod_p, mod_s, g_ffn, w_gate, w_pa, w_pb, w_out)


def _ffn_in_kernel(h_ref, wg_ref, wu_ref, wdown_ref, o_ref, wdown_b_ref):
    h = h_ref[...]
    for c in range(0, wg_ref.shape[1], FFN_CHUNK):
        cols = slice(c, c + FFN_CHUNK)
        act = jax.nn.silu(_dot(h, wg_ref[:, cols].astype(BF16))) * _dot(h, wu_ref[:, cols].astype(BF16))
        o_ref[:, cols] = act.astype(BF16)
    wdown_b_ref[...] = wdown_ref[...].astype(BF16)


def _ffn_in(h2, w_ffn_in, w_ffn_out, tm, tn):
    rows = h2.shape[0]
    n_blocks = D_FF // tn
    m_steps = rows // tm
    slab_rows = D_FF // (n_blocks * m_steps)
    assert rows % tm == 0 and D_FF % tn == 0 and D_FF % (16 * n_blocks * m_steps) == 0
    slab = pl.BlockSpec((slab_rows, D_MODEL), lambda j, i: (j * m_steps + i, 0))
    return pl.pallas_call(
        _ffn_in_kernel,
        grid=(n_blocks, m_steps),
        in_specs=[
            pl.BlockSpec((tm, D_MODEL), lambda j, i: (i, 0)),
            pl.BlockSpec((D_MODEL, tn), lambda j, i: (0, j)),
            pl.BlockSpec((D_MODEL, tn), lambda j, i: (0, j + n_blocks)),
            slab,
        ],
        out_specs=[pl.BlockSpec((tm, tn), lambda j, i: (i, j)), slab],
        out_shape=[
            jax.ShapeDtypeStruct((rows, D_FF), BF16),
            jax.ShapeDtypeStruct((D_FF, D_MODEL), BF16),
        ],
        compiler_params=pltpu.CompilerParams(
            dimension_semantics=("arbitrary", "arbitrary"), vmem_limit_bytes=VMEM_LIMIT_BYTES),
        name="ffn_in",
    )(h2, w_ffn_in, w_ffn_in, w_ffn_out)


def _ffn_out_kernel(a_ref, x1_ref, mp_ref, ms_ref, gfin_ref, w_ref, yp_ref, ys_ref, *, til):
    tp = til.tp
    half = tp // 2
    gfin = gfin_ref[...]
    gtp, gts = til.mods(mp_ref, ms_ref, 0)
    out_a = _dot(a_ref[:half], w_ref[...])
    yp_ref[:half] = _rms(x1_ref[:half] + gtp * out_a, gfin)
    out_b = _dot(a_ref[half:], w_ref[...])
    yp_ref[half:] = _rms(x1_ref[half:tp] + gtp * out_b[:half], gfin)
    ys_ref[...] = _rms(x1_ref[tp:] + gts * out_b[half:], gfin)


def _ffn_out(act, x1, mod_p, mod_s, g_final, w_ffn_out, til, steps):
    return pl.pallas_call(
        functools.partial(_ffn_out_kernel, til=til),
        grid=(til.steps,),
        in_specs=[
            til.both(D_FF), til.both(D_MODEL),
            *til.mod_tables(FFN_OUT_MODS),
            _resident((1, D_MODEL)),
            _resident(w_ffn_out.shape),
        ],
        out_specs=[til.prompt(D_MODEL), til.sample_by_batch(D_MODEL)],
        out_shape=[
            jax.ShapeDtypeStruct((til.steps * til.tp, D_MODEL), F32),
            jax.ShapeDtypeStruct((til.n_sample, steps * D_MODEL), F32),
        ],
        compiler_params=_params(),
        name="ffn_out",
    )(act, x1, mod_p, mod_s, g_final, w_ffn_out)


def _lane_expand(per_group):
    return jnp.repeat(per_group, SG_HEAD, axis=1)


def kernel(x_prompt, x_sample, state_conv, c_prompt, c_sample, g_mix, g_ffn, w_ada, b_ada, w_in, w_conv, g_v,
           w_sg, b_sg, w_pa, w_pb, w_out, w_ffn_in, w_ffn_out, g_final):
    assert g_mix.shape[0] == 1, "one layer"
    batch, seq, _ = x_prompt.shape
    n_sample, steps, _ = x_sample.shape
    assert batch <= SUBLANES and seq % CHUNK == 0
    row = lambda v: v.reshape(1, -1)

    pad = lambda n: jnp.zeros((n, D_MODEL), F32)
    c_all = jnp.concatenate([c_prompt, pad(SUBLANES - batch), c_sample, pad(-(SUBLANES + n_sample) % 16)], axis=0)
    table = lambda m: m.reshape(m.shape[0], SUBLANES, 1, D_MODEL)
    a, mod_p, mod_s = _ada(c_all, w_ada[0], row(b_ada[0]), n_sample, ADA_SLOTS)
    mod_p = table(mod_p)

    xp = x_prompt.reshape(batch * seq, D_MODEL)
    xs = x_sample.reshape(n_sample, steps * D_MODEL)
    state = state_conv[0].reshape(n_sample, (CONV_WIDTH - 1) * CONV_DIM)
    til = _Tiling(batch * seq, steps * n_sample, seq, n_sample, tp=256)

    bias_p = _lane_expand(b_sg[0][:, :CHUNK].T)
    w_sg_s = _lane_expand(w_sg[0][:, :steps, :steps].transpose(1, 2, 0).reshape(steps * steps, SG_GROUPS))
    w_sg_s = w_sg_s.reshape(steps * steps, 1, SG_DIM)
    bias_s = _lane_expand(b_sg[0][:, :steps].T).reshape(steps, 1, SG_DIM)
    h, ya, yb, zst_p, vst_p, z_s, vn_s, mod2_p, mod2_s, w_gate_b, w_pa_b, w_pb_b, w_out_b = _mix(
        xp, xs, mod_p, mod_s, row(g_mix[0]), w_in[0], w_conv[0], row(g_v[0]), w_sg[0], bias_p, w_sg_s, bias_s,
        state, a, w_ada[0], row(b_ada[0]), w_pa[0], w_pb[0], w_out[0], til, steps)
    mod2_p = table(mod2_p)

    x1, h2 = _merge(xp, xs, h, ya, yb, mod2_p, mod2_s, row(g_ffn[0]), w_gate_b, w_pa_b, w_pb_b, w_out_b, til)
    act, w_down_b = _ffn_in(h2, w_ffn_in[0], w_ffn_out[0], tm=8 * (til.tp + til.ts), tn=512)
    y_p, y_s = _ffn_out(act, x1, mod2_p, mod2_s, row(g_final), w_down_b, til, steps)

    y_prompt = y_p.reshape(batch, seq, D_MODEL)
    y_sample = y_s.reshape(n_sample, steps, D_MODEL)
    conv_prompt = zst_p[:, SUBLANES - (CONV_WIDTH - 1):, :][None]
    conv_sample = z_s.reshape(n_sample, steps, CONV_DIM)[:, steps - (CONV_WIDTH - 1):, :][None]
    sgv_prompt = vst_p.reshape(1, batch, CHUNK, SG_GROUPS, SG_HEAD)
    sgv_sample = vn_s.reshape(1, n_sample, steps, SG_GROUPS, SG_HEAD)
    return (y_prompt, y_sample, conv_prompt, conv_sample, sgv_prompt, sgv_sample)
```

```python
import functools

import jax
import jax.numpy as jnp
from jax import lax
from jax.experimental import pallas as pl
from jax.experimental.pallas import tpu as pltpu

D_MODEL = 2048
CONV_DIM = D_MODEL // 2
CONV_WIDTH = 3
SG_DIM = D_MODEL // 2
SG_GROUPS = 8
SG_HEAD = SG_DIM // SG_GROUPS
CHUNK = 128
D_FF = 5632
N_MOD = 6
EPS = 1e-6
SEG = 1024
MIX_COLS = 3 * CONV_DIM + 2 * SG_DIM
SUBLANES = 8
FFN_CHUNK = 256
STAGE_SLOTS = 4
VMEM_LIMIT_BYTES = 60 * 1024 * 1024

MOD_ORDER = (1, 0, 2, 4, 3, 5)
ADA_SLOTS, SIDE_SLOTS = (0, 2), (2, 4)
MIX_MODS, MERGE_MODS, FFN_OUT_MODS = (0, 2), (0, 3), (3, 1)
SIDE_TN = 256

F32 = jnp.float32
BF16 = jnp.bfloat16


def _dot(a, b):
    return jnp.dot(a, b, preferred_element_type=F32)


def _rms(x, gain):
    return x * lax.rsqrt(jnp.mean(x * x, axis=-1, keepdims=True) + EPS) * gain


def _modulated_norm(x, gain, scale, shift):
    return _rms(x, gain) * (1 + scale) + shift


def _rows(*parts):
    return jnp.concatenate(parts, axis=0)


def _load_bf16(jobs, stage_ref, sem_ref):
    n_slots, rows, n = stage_ref.shape
    chunks = [(src, col0, dst, r0) for src, col0, dst in jobs for r0 in range(0, dst.shape[0], rows)]
    assert all(dst.shape[1] == n and dst.shape[0] % rows == 0 for _, _, dst in jobs)

    def chunk_copy(idx):
        src, col0, _, r0 = chunks[idx]
        slot = idx % n_slots
        return pltpu.make_async_copy(src.at[pl.ds(r0, rows), pl.ds(col0, n)], stage_ref.at[slot], sem_ref.at[slot])

    ahead = n_slots - 1
    for idx in range(min(ahead, len(chunks))):
        chunk_copy(idx).start()
    for idx, (_, _, dst, r0) in enumerate(chunks):
        if idx + ahead < len(chunks):
            chunk_copy(idx + ahead).start()
        chunk_copy(idx).wait()
        dst[r0:r0 + rows, :] = stage_ref[idx % n_slots].astype(BF16)


def _ada_columns(first_slot, tn):
    per = D_MODEL // tn

    def index_map(j):
        comp = 0
        for slot, c in enumerate(MOD_ORDER):
            comp = jnp.where(first_slot + j // per == slot, c, comp)
        return (0, comp * per + j % per)

    return index_map


def _ada_table_specs(n_sample, tn):
    per = D_MODEL // tn
    return [
        pl.BlockSpec((None, SUBLANES, tn), lambda j: (j // per, 0, j % per)),
        pl.BlockSpec((None, n_sample, tn), lambda j: (j // per, 0, j % per)),
    ]


def _ada_block(a, w_ref, b_ref, mp_ref, ms_ref):
    r = _dot(a, w_ref[...].astype(BF16)) + b_ref[...]
    mp_ref[...] = r[:SUBLANES]
    ms_ref[...] = r[SUBLANES:SUBLANES + ms_ref.shape[0]]


def _ada_kernel(c_ref, w_ref, b_ref, a_ref, mp_ref, ms_ref):
    a = jax.nn.silu(c_ref[...]).astype(BF16)
    a_ref[...] = a
    _ada_block(a, w_ref, b_ref, mp_ref, ms_ref)


def _ada(c_all, w_ada, b_ada, n_sample, slots):
    tn = 1024
    first, count = slots
    rows = c_all.shape[0]
    return pl.pallas_call(
        _ada_kernel,
        grid=(count * (D_MODEL // tn),),
        in_specs=[
            pl.BlockSpec((rows, D_MODEL), lambda j: (0, 0)),
            pl.BlockSpec((D_MODEL, tn), _ada_columns(first, tn)),
            pl.BlockSpec((1, tn), _ada_columns(first, tn)),
        ],
        out_specs=[pl.BlockSpec((rows, D_MODEL), lambda j: (0, 0)), *_ada_table_specs(n_sample, tn)],
        out_shape=[
            jax.ShapeDtypeStruct((rows, D_MODEL), BF16),
            jax.ShapeDtypeStruct((count, SUBLANES, D_MODEL), F32),
            jax.ShapeDtypeStruct((count, n_sample, D_MODEL), F32),
        ],
        compiler_params=pltpu.CompilerParams(
            dimension_semantics=("arbitrary",), vmem_limit_bytes=VMEM_LIMIT_BYTES),
        name="ada",
    )(c_all, w_ada, b_ada)


class _Tiling:
    def __init__(self, rows_p, rows_s, seq, n_sample, positions, tp):
        self.tp = tp
        self.steps = rows_p // tp
        self.ts = rows_s // self.steps
        self.rows = rows_p + rows_s
        self.tiles_per_seq = seq // tp
        self.positions = positions
        self.n_sample = n_sample
        assert rows_p % tp == 0 and seq % tp == 0 and rows_s % self.steps == 0
        assert self.ts % 16 == 0 and n_sample % self.ts == 0 and rows_s == n_sample * positions

    def prompt(self, width):
        return pl.BlockSpec((self.tp, width), lambda i: (i, 0))

    def sample_rows(self, width):
        return pl.BlockSpec((self.ts, width), lambda i: (i, 0))

    def sample_block(self, *shape):
        return pl.BlockSpec((self.ts, *shape), lambda i: (i // self.positions,) + (0,) * len(shape))

    def position(self):
        return pl.program_id(0) % self.positions

    def both(self, width):
        return pl.BlockSpec((self.tp + self.ts, width), lambda i: (i, 0))

    def mod_tables(self, slots):
        first, count = slots
        assert first % count == 0
        return [
            pl.BlockSpec((count, SUBLANES, 1, D_MODEL), lambda i: (first // count, 0, 0, 0),
                         pipeline_mode=pl.Buffered(1)),
            pl.BlockSpec((count, self.n_sample, D_MODEL), lambda i: (first // count, 0, 0),
                         pipeline_mode=pl.Buffered(1)),
        ]

    def mods(self, mp_ref, ms_ref, k):
        i = pl.program_id(0)
        start = pl.multiple_of((i // self.positions) * self.ts, self.ts)
        return mp_ref[k, i // self.tiles_per_seq], ms_ref[k, pl.ds(start, self.ts), :]


def _resident(shape):
    return pl.BlockSpec(shape, lambda i: (0,) * len(shape), pipeline_mode=pl.Buffered(1))


_HBM = pl.BlockSpec(memory_space=pl.ANY)


def _params():
    return pltpu.CompilerParams(dimension_semantics=("arbitrary",), vmem_limit_bytes=VMEM_LIMIT_BYTES)


def _mix_kernel(xp_ref, xs_ref, mp_ref, ms_ref, gmix_ref, w_hbm, wconv_ref, gv_ref,
                wsg_ref, bsgp_ref, wsgs_ref, bsgs_ref, state_ref,
                a_ref, wada_ref, bada_ref, wg0_ref, wg1_ref, wg2_ref, wg3_ref, wpa_ref, wpb_ref, wout_ref,
                h_ref, ya_ref, yb_ref, xsr_ref, zstp_ref, vstp_ref, zs_ref, vs_ref,
                mpr_ref, msr_ref, wgate_b_ref, wpa_b_ref, wpb_b_ref, wout_b_ref,
                w_ref, stage_ref, sem_ref, carry_ref, zhist_ref, vhist_ref, *, til, steps):
    i = pl.program_id(0)
    tp, ts = til.tp, til.ts
    t = til.position()

    def slot(ref, k):
        start = k * ts if isinstance(k, int) else pl.multiple_of(k * ts, ts)
        return ref[pl.ds(start, ts), :]

    @pl.when(i == 0)
    def _():
        _load_bf16([(w_hbm, 0, w_ref)], stage_ref, sem_ref)
        vhist_ref[...] = jnp.zeros_like(vhist_ref)

    @pl.when(i % til.tiles_per_seq == 0)
    def _():
        carry_ref[...] = jnp.zeros_like(carry_ref)

    gmix = gmix_ref[...]
    scp, scs = til.mods(mp_ref, ms_ref, 0)
    shp, shs = til.mods(mp_ref, ms_ref, 1)
    xs = xs_ref[:, t, :]
    xsr_ref[...] = xs
    h = _rows(_modulated_norm(xp_ref[...], gmix, scp, shp), _modulated_norm(xs, gmix, scs, shs)).astype(BF16)
    h_ref[...] = h

    _ada_block(a_ref[...], wada_ref, bada_ref, mpr_ref, msr_ref)
    for k, wg_ref in enumerate((wg0_ref, wg1_ref, wg2_ref, wg3_ref)):
        wgate_b_ref[:, k * SEG:(k + 1) * SEG] = wg_ref[...].astype(BF16)
    wpa_b_ref[...] = wpa_ref[...].astype(BF16)
    wpb_b_ref[...] = wpb_ref[...].astype(BF16)
    wout_b_ref[...] = wout_ref[...].astype(BF16)
    proj = lambda k: _dot(h, w_ref[:, k * SEG:(k + 1) * SEG])

    vn = _rms(jax.nn.gelu(proj(4)), gv_ref[...])
    z_all = proj(1) * proj(2)
    vstp_ref[...] = vn[tp - CHUNK:tp]
    vb = vn[:tp].astype(BF16)
    causal = (lax.broadcasted_iota(jnp.int32, (CHUNK, CHUNK), 0)
              >= lax.broadcasted_iota(jnp.int32, (CHUNK, CHUNK), 1))
    wgs = [jnp.where(causal, wsg_ref[g], 0.0).astype(BF16) for g in range(SG_GROUPS)]
    bias = bsgp_ref[...]
    chunks = []
    for c in range(tp // CHUNK):
        rows = slice(c * CHUNK, (c + 1) * CHUNK)
        parts = [_dot(wgs[g], vb[rows, g * SG_HEAD:(g + 1) * SG_HEAD]) for g in range(SG_GROUPS)]
        chunks.append(jnp.concatenate(parts, axis=1) + bias)
    gu = jax.nn.gelu(proj(3))
    b_gate = proj(0)
    yb_ref[:tp] = (gu[:tp] * _rows(*chunks)).astype(BF16)

    vs = vn[tp:]
    for g in range(SG_GROUPS):
        vs_ref[:, t, g, :] = vs[:, g * SG_HEAD:(g + 1) * SG_HEAD]
    vhist_ref[pl.ds(pl.multiple_of(t * ts, ts), ts), :] = vs
    sp = bsgs_ref[t]
    for s in range(steps):
        w_ts = jnp.where(s <= t, wsgs_ref[t * steps + s], 0.0)
        sp = sp + w_ts * slot(vhist_ref, s)
    yb_ref[tp:] = (gu[tp:] * sp).astype(BF16)

    wc = wconv_ref[...]
    z = z_all[:tp]
    carry = carry_ref[...]
    prev2, prev1 = carry[SUBLANES - 2:SUBLANES - 1], carry[SUBLANES - 1:SUBLANES]
    row = lax.broadcasted_iota(jnp.int32, (SUBLANES, CONV_DIM), 0)
    z1 = pltpu.roll(z, 1, 0)
    z2 = pltpu.roll(z, 2, 0)
    z1 = _rows(jnp.where(row == 0, prev1, z1[:SUBLANES]), z1[SUBLANES:])
    z2 = _rows(jnp.where(row == 0, prev2, jnp.where(row == 1, prev1, z2[:SUBLANES])), z2[SUBLANES:])
    conv = wc[0:1] * z2 + wc[1:2] * z1 + wc[2:3] * z
    ya_ref[:tp] = (b_gate[:tp] * conv).astype(BF16)
    carry_ref[...] = z[tp - SUBLANES:]
    zstp_ref[...] = z[tp - SUBLANES:]

    zs = z_all[tp:]
    for k in range(CONV_WIDTH - 1):
        zhist_ref[k * ts:(k + 1) * ts, :] = state_ref[:, k, :]
    zhist_ref[pl.ds(pl.multiple_of((t + CONV_WIDTH - 1) * ts, ts), ts), :] = zs
    conv_s = wc[0:1] * slot(zhist_ref, t) + wc[1:2] * slot(zhist_ref, t + 1) + wc[2:3] * zs
    ya_ref[tp:] = (b_gate[tp:] * conv_s).astype(BF16)
    zs_ref[:, jnp.maximum(t - (steps - (CONV_WIDTH - 1)), 0), :] = zs


def _row_slab(rows_total, n_steps, width):
    assert rows_total % (16 * n_steps) == 0
    return pl.BlockSpec((rows_total // n_steps, width), lambda i: (i, 0))


def _mix(xp, xs, mod_p, mod_s, g_mix, w_in, w_conv, g_v, w_sg, bias_p, w_sg_s, bias_s, state,
         a, w_ada, b_ada, w_pa, w_pb, w_out, til, steps):
    n_sample = til.n_sample
    n_seq = til.steps // til.tiles_per_seq
    per_seq = lambda rows, width: pl.BlockSpec((None, rows, width), lambda i: (i // til.tiles_per_seq, 0, 0))
    stage_rows = 64
    assert steps >= CONV_WIDTH - 1
    n_side = SIDE_SLOTS[1]
    assert til.steps * SIDE_TN == n_side * D_MODEL, "one modulation-table column block per grid step"
    slab = lambda k, width: _row_slab(k, til.steps, width)
    return pl.pallas_call(
        functools.partial(_mix_kernel, til=til, steps=steps),
        grid=(til.steps,),
        in_specs=[
            til.prompt(D_MODEL), til.sample_block(steps, D_MODEL), *til.mod_tables(MIX_MODS),
            _resident((1, D_MODEL)),
            _HBM,
            _resident((CONV_WIDTH, CONV_DIM)),
            _resident((1, SG_DIM)),
            _resident((SG_GROUPS, CHUNK, CHUNK)),
            _resident((CHUNK, SG_DIM)),
            _resident((steps * steps, 1, SG_DIM)),
            _resident((steps, 1, SG_DIM)),
            til.sample_block(CONV_WIDTH - 1, CONV_DIM),
            _resident(a.shape),
            pl.BlockSpec((D_MODEL, SIDE_TN), _ada_columns(SIDE_SLOTS[0], SIDE_TN)),
            pl.BlockSpec((1, SIDE_TN), _ada_columns(SIDE_SLOTS[0], SIDE_TN)),
            *[pl.BlockSpec((D_MODEL // til.steps, SEG), lambda i, k=k: (i, MIX_COLS // SEG + k))
              for k in range(2 * D_MODEL // SEG)],
            slab(CONV_DIM, D_MODEL), slab(SG_DIM, D_MODEL), slab(D_MODEL, D_MODEL),
        ],
        out_specs=[
            til.both(D_MODEL), til.both(CONV_DIM), til.both(SG_DIM), til.sample_rows(D_MODEL),
            per_seq(SUBLANES, CONV_DIM), per_seq(CHUNK, SG_DIM),
            til.sample_block(CONV_WIDTH - 1, CONV_DIM), til.sample_block(steps, SG_GROUPS, SG_HEAD),
            *_ada_table_specs(n_sample, SIDE_TN),
            slab(D_MODEL, 2 * D_MODEL), slab(CONV_DIM, D_MODEL), slab(SG_DIM, D_MODEL), slab(D_MODEL, D_MODEL),
        ],
        out_shape=[
            jax.ShapeDtypeStruct((til.rows, D_MODEL), BF16),
            jax.ShapeDtypeStruct((til.rows, CONV_DIM), BF16),
            jax.ShapeDtypeStruct((til.rows, SG_DIM), BF16),
            jax.ShapeDtypeStruct((til.steps * til.ts, D_MODEL), F32),
            jax.ShapeDtypeStruct((n_seq, SUBLANES, CONV_DIM), F32),
            jax.ShapeDtypeStruct((n_seq, CHUNK, SG_DIM), F32),
            jax.ShapeDtypeStruct((n_sample, CONV_WIDTH - 1, CONV_DIM), F32),
            jax.ShapeDtypeStruct((n_sample, steps, SG_GROUPS, SG_HEAD), F32),
            jax.ShapeDtypeStruct((n_side, SUBLANES, D_MODEL), F32),
            jax.ShapeDtypeStruct((n_side, n_sample, D_MODEL), F32),
            jax.ShapeDtypeStruct((D_MODEL, 2 * D_MODEL), BF16),
            jax.ShapeDtypeStruct((CONV_DIM, D_MODEL), BF16),
            jax.ShapeDtypeStruct((SG_DIM, D_MODEL), BF16),
            jax.ShapeDtypeStruct((D_MODEL, D_MODEL), BF16),
        ],
        scratch_shapes=[
            pltpu.VMEM((D_MODEL, MIX_COLS), BF16),
            pltpu.VMEM((STAGE_SLOTS, stage_rows, MIX_COLS), F32),
            pltpu.SemaphoreType.DMA((STAGE_SLOTS,)),
            pltpu.VMEM((SUBLANES, CONV_DIM), F32),
            pltpu.VMEM(((CONV_WIDTH - 1 + steps) * til.ts, CONV_DIM), F32),
            pltpu.VMEM((steps * til.ts, SG_DIM), F32),
        ],
        compiler_params=_params(),
        name="mix",
    )(xp, xs, mod_p, mod_s, g_mix, w_in, w_conv, g_v, w_sg, bias_p, w_sg_s, bias_s, state,
      a, w_ada, b_ada, w_in, w_in, w_in, w_in, w_pa, w_pb, w_out)


def _merge_kernel(xp_ref, xs_ref, h_ref, ya_ref, yb_ref, mp_ref, ms_ref, gffn_ref,
                  wgate_ref, wpa_ref, wpb_ref, wout_ref, x1_ref, h2_ref, *, til):
    tp = til.tp
    half = tp // 2
    h = h_ref[...]
    ya = ya_ref[...]
    yb = yb_ref[...]
    halves = []
    for k in range(D_MODEL // SEG):
        cols = slice(k * SEG, (k + 1) * SEG)
        gate_b_cols = slice(D_MODEL + k * SEG, D_MODEL + (k + 1) * SEG)
        branch_a = jax.nn.sigmoid(_dot(h, wgate_ref[:, cols])) * _dot(ya, wpa_ref[:, cols])
        branch_b = jax.nn.sigmoid(_dot(h, wgate_ref[:, gate_b_cols])) * _dot(yb, wpb_ref[:, cols])
        halves.append((branch_a + branch_b).astype(BF16))
    merged = jnp.concatenate(halves, axis=1)

    gffn = gffn_ref[...]
    gtp, gts = til.mods(mp_ref, ms_ref, 0)
    scp, scs = til.mods(mp_ref, ms_ref, 1)
    shp, shs = til.mods(mp_ref, ms_ref, 2)

    def finish(rows, x, out, gate, scale, shift):
        x1 = x + gate * out
        x1_ref[rows] = x1
        h2_ref[rows] = _modulated_norm(x1, gffn, scale, shift).astype(BF16)

    out_a = _dot(merged[:half], wout_ref[...])
    finish(slice(0, half), xp_ref[:half], out_a, gtp, scp, shp)
    out_b = _dot(merged[half:], wout_ref[...])
    finish(slice(half, tp), xp_ref[half:], out_b[:half], gtp, scp, shp)
    finish(slice(tp, tp + til.ts), xs_ref[...], out_b[half:], gts, scs, shs)


def _merge(xp, xs, h, ya, yb, mod_p, mod_s, g_ffn, w_gate, w_pa, w_pb, w_out, til):
    return pl.pallas_call(
        functools.partial(_merge_kernel, til=til),
        grid=(til.steps,),
        in_specs=[
            til.prompt(D_MODEL), til.sample_rows(D_MODEL),
            til.both(D_MODEL), til.both(CONV_DIM), til.both(SG_DIM),
            *til.mod_tables(MERGE_MODS),
            _resident((1, D_MODEL)),
            _resident(w_gate.shape), _resident(w_pa.shape), _resident(w_pb.shape), _resident(w_out.shape),
        ],
        out_specs=[til.both(D_MODEL), til.both(D_MODEL)],
        out_shape=[
            jax.ShapeDtypeStruct((til.rows, D_MODEL), F32),
            jax.ShapeDtypeStruct((til.rows, D_MODEL), BF16),
        ],
        compiler_params=_params(),
        name="merge",
    )(xp, xs, h, ya, yb, mod_p, mod_s, g_ffn, w_gate, w_pa, w_pb, w_out)


def _ffn_in_kernel(h_ref, wg_ref, wu_ref, wdown_ref, o_ref, wdown_b_ref):
    h = h_ref[...]
    for c in range(0, wg_ref.shape[1], FFN_CHUNK):
        cols = slice(c, c + FFN_CHUNK)
        act = jax.nn.silu(_dot(h, wg_ref[:, cols].astype(BF16))) * _dot(h, wu_ref[:, cols].astype(BF16))
        o_ref[:, cols] = act.astype(BF16)
    wdown_b_ref[...] = wdown_ref[...].astype(BF16)


def _ffn_in(h2, w_ffn_in, w_ffn_out, tm, tn):
    rows = h2.shape[0]
    n_blocks = D_FF // tn
    m_steps = rows // tm
    slab_rows = D_FF // (n_blocks * m_steps)
    assert rows % tm == 0 and D_FF % tn == 0 and D_FF % (16 * n_blocks * m_steps) == 0
    slab = pl.BlockSpec((slab_rows, D_MODEL), lambda j, i: (j * m_steps + i, 0))
    return pl.pallas_call(
        _ffn_in_kernel,
        grid=(n_blocks, m_steps),
        in_specs=[
            pl.BlockSpec((tm, D_MODEL), lambda j, i: (i, 0)),
            pl.BlockSpec((D_MODEL, tn), lambda j, i: (0, j)),
            pl.BlockSpec((D_MODEL, tn), lambda j, i: (0, j + n_blocks)),
            slab,
        ],
        out_specs=[pl.BlockSpec((tm, tn), lambda j, i: (i, j)), slab],
        out_shape=[
            jax.ShapeDtypeStruct((rows, D_FF), BF16),
            jax.ShapeDtypeStruct((D_FF, D_MODEL), BF16),
        ],
        compiler_params=pltpu.CompilerParams(
            dimension_semantics=("arbitrary", "arbitrary"), vmem_limit_bytes=VMEM_LIMIT_BYTES),
        name="ffn_in",
    )(h2, w_ffn_in, w_ffn_in, w_ffn_out)


def _ffn_out_kernel(a_ref, x1_ref, mp_ref, ms_ref, gfin_ref, w_ref, yp_ref, ys_ref, *, til):
    tp = til.tp
    half = tp // 2
    gfin = gfin_ref[...]
    gtp, gts = til.mods(mp_ref, ms_ref, 0)
    out_a = _dot(a_ref[:half], w_ref[...])
    yp_ref[:half] = _rms(x1_ref[:half] + gtp * out_a, gfin)
    out_b = _dot(a_ref[half:], w_ref[...])
    yp_ref[half:] = _rms(x1_ref[half:tp] + gtp * out_b[:half], gfin)
    ys_ref[:, til.position(), :] = _rms(x1_ref[tp:] + gts * out_b[half:], gfin)


def _ffn_out(act, x1, mod_p, mod_s, g_final, w_ffn_out, til, steps):
    return pl.pallas_call(
        functools.partial(_ffn_out_kernel, til=til),
        grid=(til.steps,),
        in_specs=[
            til.both(D_FF), til.both(D_MODEL),
            *til.mod_tables(FFN_OUT_MODS),
            _resident((1, D_MODEL)),
            _resident(w_ffn_out.shape),
        ],
        out_specs=[til.prompt(D_MODEL), til.sample_block(steps, D_MODEL)],
        out_shape=[
            jax.ShapeDtypeStruct((til.steps * til.tp, D_MODEL), F32),
            jax.ShapeDtypeStruct((til.n_sample, steps, D_MODEL), F32),
        ],
        compiler_params=_params(),
        name="ffn_out",
    )(act, x1, mod_p, mod_s, g_final, w_ffn_out)


def _lane_expand(per_group):
    return jnp.repeat(per_group, SG_HEAD, axis=1)


def kernel(x_prompt, x_sample, state_conv, c_prompt, c_sample, g_mix, g_ffn, w_ada, b_ada, w_in, w_conv, g_v,
           w_sg, b_sg, w_pa, w_pb, w_out, w_ffn_in, w_ffn_out, g_final):
    assert g_mix.shape[0] == 1, "one layer"
    batch, seq, _ = x_prompt.shape
    n_sample, steps, _ = x_sample.shape
    assert batch <= SUBLANES and seq % CHUNK == 0
    row = lambda v: v.reshape(1, -1)

    pad = lambda n: jnp.zeros((n, D_MODEL), F32)
    c_all = jnp.concatenate([c_prompt, pad(SUBLANES - batch), c_sample, pad(-(SUBLANES + n_sample) % 16)], axis=0)
    table = lambda m: m.reshape(m.shape[0], SUBLANES, 1, D_MODEL)
    a, mod_p, mod_s = _ada(c_all, w_ada[0], row(b_ada[0]), n_sample, ADA_SLOTS)
    mod_p = table(mod_p)

    xp = x_prompt.reshape(batch * seq, D_MODEL)
    xs = x_sample
    state = state_conv[0]
    til = _Tiling(batch * seq, steps * n_sample, seq, n_sample, steps, tp=256)

    bias_p = _lane_expand(b_sg[0][:, :CHUNK].T)
    w_sg_s = _lane_expand(w_sg[0][:, :steps, :steps].transpose(1, 2, 0).reshape(steps * steps, SG_GROUPS))
    w_sg_s = w_sg_s.reshape(steps * steps, 1, SG_DIM)
    bias_s = _lane_expand(b_sg[0][:, :steps].T).reshape(steps, 1, SG_DIM)
    h, ya, yb, xs_rows, zst_p, vst_p, z_s, vn_s, mod2_p, mod2_s, w_gate_b, w_pa_b, w_pb_b, w_out_b = _mix(
        xp, xs, mod_p, mod_s, row(g_mix[0]), w_in[0], w_conv[0], row(g_v[0]), w_sg[0], bias_p, w_sg_s, bias_s,
        state, a, w_ada[0], row(b_ada[0]), w_pa[0], w_pb[0], w_out[0], til, steps)
    mod2_p = table(mod2_p)

    x1, h2 = _merge(xp, xs_rows, h, ya, yb, mod2_p, mod2_s, row(g_ffn[0]), w_gate_b, w_pa_b, w_pb_b, w_out_b, til)
    act, w_down_b = _ffn_in(h2, w_ffn_in[0], w_ffn_out[0], tm=8 * (til.tp + til.ts), tn=512)
    y_p, y_s = _ffn_out(act, x1, mod2_p, mod2_s, row(g_final), w_down_b, til, steps)

    y_prompt = y_p.reshape(batch, seq, D_MODEL)
    conv_prompt = zst_p[:, SUBLANES - (CONV_WIDTH - 1):, :][None]
    sgv_prompt = vst_p.reshape(1, batch, CHUNK, SG_GROUPS, SG_HEAD)
    return (y_prompt, y_s, conv_prompt, z_s[None], sgv_prompt, vn_s[None])
```

```python
import functools

import jax
import jax.numpy as jnp
from jax import lax
from jax.experimental import pallas as pl
from jax.experimental.pallas import tpu as pltpu

D_MODEL = 2048
CONV_DIM = D_MODEL // 2
CONV_WIDTH = 3
SG_DIM = D_MODEL // 2
SG_GROUPS = 8
SG_HEAD = SG_DIM // SG_GROUPS
CHUNK = 128
D_FF = 5632
N_MOD = 6
EPS = 1e-6
SEG = 1024
MIX_COLS = 3 * CONV_DIM + 2 * SG_DIM
SUBLANES = 8
FFN_CHUNK = 256
STAGE_SLOTS = 4
VMEM_LIMIT_BYTES = 60 * 1024 * 1024

MOD_ORDER = (1, 0, 2, 4, 3, 5)
ADA_SLOTS, SIDE_SLOTS = (0, 2), (2, 4)
MIX_MODS, MERGE_MODS, FFN_OUT_MODS = (0, 2), (0, 3), (3, 1)
SIDE_TN = 256

F32 = jnp.float32
BF16 = jnp.bfloat16


def _dot(a, b):
    return jnp.dot(a, b, preferred_element_type=F32)


def _rms(x, gain):
    return x * lax.rsqrt(jnp.mean(x * x, axis=-1, keepdims=True) + EPS) * gain


def _modulated_norm(x, gain, scale, shift):
    return _rms(x, gain) * (1 + scale) + shift


def _rows(*parts):
    return jnp.concatenate(parts, axis=0)


def _load_bf16(jobs, stage_ref, sem_ref):
    n_slots, rows, n = stage_ref.shape
    chunks = [(src, col0, dst, r0) for src, col0, dst in jobs for r0 in range(0, dst.shape[0], rows)]
    assert all(dst.shape[1] == n and dst.shape[0] % rows == 0 for _, _, dst in jobs)

    def chunk_copy(idx):
        src, col0, _, r0 = chunks[idx]
        slot = idx % n_slots
        return pltpu.make_async_copy(src.at[pl.ds(r0, rows), pl.ds(col0, n)], stage_ref.at[slot], sem_ref.at[slot])

    ahead = n_slots - 1
    for idx in range(min(ahead, len(chunks))):
        chunk_copy(idx).start()
    for idx, (_, _, dst, r0) in enumerate(chunks):
        if idx + ahead < len(chunks):
            chunk_copy(idx + ahead).start()
        chunk_copy(idx).wait()
        dst[r0:r0 + rows, :] = stage_ref[idx % n_slots].astype(BF16)


def _ada_columns(first_slot, tn):
    per = D_MODEL // tn

    def index_map(j):
        comp = 0
        for slot, c in enumerate(MOD_ORDER):
            comp = jnp.where(first_slot + j // per == slot, c, comp)
        return (0, comp * per + j % per)

    return index_map


def _ada_table_specs(n_sample, tn):
    per = D_MODEL // tn
    return [
        pl.BlockSpec((None, SUBLANES, tn), lambda j: (j // per, 0, j % per)),
        pl.BlockSpec((None, n_sample, tn), lambda j: (j // per, 0, j % per)),
    ]


def _ada_block(a, w_ref, b_ref, mp_ref, ms_ref):
    r = _dot(a, w_ref[...].astype(BF16)) + b_ref[...]
    mp_ref[...] = r[:SUBLANES]
    ms_ref[...] = r[SUBLANES:SUBLANES + ms_ref.shape[0]]


def _ada_kernel(c_ref, w_ref, b_ref, a_ref, mp_ref, ms_ref):
    a = jax.nn.silu(c_ref[...]).astype(BF16)
    a_ref[...] = a
    _ada_block(a, w_ref, b_ref, mp_ref, ms_ref)


def _ada(c_all, w_ada, b_ada, n_sample, slots):
    tn = 1024
    first, count = slots
    rows = c_all.shape[0]
    return pl.pallas_call(
        _ada_kernel,
        grid=(count * (D_MODEL // tn),),
        in_specs=[
            pl.BlockSpec((rows, D_MODEL), lambda j: (0, 0)),
            pl.BlockSpec((D_MODEL, tn), _ada_columns(first, tn)),
            pl.BlockSpec((1, tn), _ada_columns(first, tn)),
        ],
        out_specs=[pl.BlockSpec((rows, D_MODEL), lambda j: (0, 0)), *_ada_table_specs(n_sample, tn)],
        out_shape=[
            jax.ShapeDtypeStruct((rows, D_MODEL), BF16),
            jax.ShapeDtypeStruct((count, SUBLANES, D_MODEL), F32),
            jax.ShapeDtypeStruct((count, n_sample, D_MODEL), F32),
        ],
        compiler_params=pltpu.CompilerParams(
            dimension_semantics=("arbitrary",), vmem_limit_bytes=VMEM_LIMIT_BYTES),
        name="ada",
    )(c_all, w_ada, b_ada)


class _Tiling:
    def __init__(self, rows_p, rows_s, seq, n_sample, positions, tp, groups=1):
        n_groups = rows_p // tp
        self.tp = tp
        self.groups = groups
        self.steps = n_groups // groups
        self.ts = rows_s // n_groups
        self.rows = rows_p + rows_s
        self.tiles_per_seq = seq // tp
        self.positions = positions
        self.n_sample = n_sample
        self._args = (rows_p, rows_s, seq, n_sample, positions, tp)
        assert rows_p % tp == 0 and seq % tp == 0 and rows_s % n_groups == 0
        assert self.ts % 16 == 0 and n_sample % self.ts == 0 and rows_s == n_sample * positions
        assert positions % groups == 0 and self.tiles_per_seq % groups == 0

    def regrouped(self, groups):
        return _Tiling(*self._args, groups=groups)

    def group(self, g=0):
        return pl.program_id(0) * self.groups + g

    def prompt(self, width):
        return pl.BlockSpec((self.groups * self.tp, width), lambda i: (i, 0))

    def sample_rows(self, width):
        return pl.BlockSpec((self.groups * self.ts, width), lambda i: (i, 0))

    def sample_block(self, *shape):
        return pl.BlockSpec((self.ts, *shape), lambda i: (i * self.groups // self.positions,) + (0,) * len(shape))

    def position(self, g=0):
        return self.group(g) % self.positions

    def both(self, width):
        return pl.BlockSpec((self.groups * (self.tp + self.ts), width), lambda i: (i, 0))

    def mod_tables(self, slots):
        first, count = slots
        assert first % count == 0
        return [
            pl.BlockSpec((count, SUBLANES, 1, D_MODEL), lambda i: (first // count, 0, 0, 0),
                         pipeline_mode=pl.Buffered(1)),
            pl.BlockSpec((count, self.n_sample, D_MODEL), lambda i: (first // count, 0, 0),
                         pipeline_mode=pl.Buffered(1)),
        ]

    def mods(self, mp_ref, ms_ref, k, g=0):
        s = self.group(g)
        start = pl.multiple_of((s // self.positions) * self.ts, self.ts)
        return mp_ref[k, s // self.tiles_per_seq], ms_ref[k, pl.ds(start, self.ts), :]


def _resident(shape):
    return pl.BlockSpec(shape, lambda i: (0,) * len(shape), pipeline_mode=pl.Buffered(1))


_HBM = pl.BlockSpec(memory_space=pl.ANY)


def _params():
    return pltpu.CompilerParams(dimension_semantics=("arbitrary",), vmem_limit_bytes=VMEM_LIMIT_BYTES)


def _mix_kernel(xp_ref, xs_ref, mp_ref, ms_ref, gmix_ref, w_hbm, wconv_ref, gv_ref,
                wsg_ref, bsgp_ref, wsgs_ref, bsgs_ref, state_ref,
                a_ref, wada_ref, bada_ref, wg0_ref, wg1_ref, wg2_ref, wg3_ref, wpa_ref, wpb_ref, wout_ref,
                h_ref, ya_ref, yb_ref, xsr_ref, zstp_ref, vstp_ref, zs_ref, vs_ref,
                mpr_ref, msr_ref, wgate_b_ref, wpa_b_ref, wpb_b_ref, wout_b_ref,
                w_ref, stage_ref, sem_ref, carry_ref, zhist_ref, vhist_ref, *, til, steps):
    i = pl.program_id(0)
    tp, ts = til.tp, til.ts
    t = til.position()

    def slot(ref, k):
        start = k * ts if isinstance(k, int) else pl.multiple_of(k * ts, ts)
        return ref[pl.ds(start, ts), :]

    @pl.when(i == 0)
    def _():
        _load_bf16([(w_hbm, 0, w_ref)], stage_ref, sem_ref)
        vhist_ref[...] = jnp.zeros_like(vhist_ref)

    @pl.when(i % til.tiles_per_seq == 0)
    def _():
        carry_ref[...] = jnp.zeros_like(carry_ref)

    gmix = gmix_ref[...]
    scp, scs = til.mods(mp_ref, ms_ref, 0)
    shp, shs = til.mods(mp_ref, ms_ref, 1)
    xs = xs_ref[:, t, :]
    xsr_ref[...] = xs
    h = _rows(_modulated_norm(xp_ref[...], gmix, scp, shp), _modulated_norm(xs, gmix, scs, shs)).astype(BF16)
    h_ref[...] = h

    _ada_block(a_ref[...], wada_ref, bada_ref, mpr_ref, msr_ref)
    for k, wg_ref in enumerate((wg0_ref, wg1_ref, wg2_ref, wg3_ref)):
        wgate_b_ref[:, k * SEG:(k + 1) * SEG] = wg_ref[...].astype(BF16)
    wpa_b_ref[...] = wpa_ref[...].astype(BF16)
    wpb_b_ref[...] = wpb_ref[...].astype(BF16)
    wout_b_ref[...] = wout_ref[...].astype(BF16)
    proj = lambda k: _dot(h, w_ref[:, k * SEG:(k + 1) * SEG])

    vn = _rms(jax.nn.gelu(proj(4)), gv_ref[...])
    z_all = proj(1) * proj(2)
    vstp_ref[...] = vn[tp - CHUNK:tp]
    vb = vn[:tp].astype(BF16)
    causal = (lax.broadcasted_iota(jnp.int32, (CHUNK, CHUNK), 0)
              >= lax.broadcasted_iota(jnp.int32, (CHUNK, CHUNK), 1))
    wgs = [jnp.where(causal, wsg_ref[g], 0.0).astype(BF16) for g in range(SG_GROUPS)]
    bias = bsgp_ref[...]
    chunks = []
    for c in range(tp // CHUNK):
        rows = slice(c * CHUNK, (c + 1) * CHUNK)
        parts = [_dot(wgs[g], vb[rows, g * SG_HEAD:(g + 1) * SG_HEAD]) for g in range(SG_GROUPS)]
        chunks.append(jnp.concatenate(parts, axis=1) + bias)
    gu = jax.nn.gelu(proj(3))
    b_gate = proj(0)
    yb_ref[:tp] = (gu[:tp] * _rows(*chunks)).astype(BF16)

    vs = vn[tp:]
    for g in range(SG_GROUPS):
        vs_ref[:, t, g, :] = vs[:, g * SG_HEAD:(g + 1) * SG_HEAD]
    vhist_ref[pl.ds(pl.multiple_of(t * ts, ts), ts), :] = vs
    sp = bsgs_ref[t]
    for s in range(steps):
        w_ts = jnp.where(s <= t, wsgs_ref[t * steps + s], 0.0)
        sp = sp + w_ts * slot(vhist_ref, s)
    yb_ref[tp:] = (gu[tp:] * sp).astype(BF16)

    wc = wconv_ref[...]
    z = z_all[:tp]
    carry = carry_ref[...]
    prev2, prev1 = carry[SUBLANES - 2:SUBLANES - 1], carry[SUBLANES - 1:SUBLANES]
    row = lax.broadcasted_iota(jnp.int32, (SUBLANES, CONV_DIM), 0)
    z1 = pltpu.roll(z, 1, 0)
    z2 = pltpu.roll(z, 2, 0)
    z1 = _rows(jnp.where(row == 0, prev1, z1[:SUBLANES]), z1[SUBLANES:])
    z2 = _rows(jnp.where(row == 0, prev2, jnp.where(row == 1, prev1, z2[:SUBLANES])), z2[SUBLANES:])
    conv = wc[0:1] * z2 + wc[1:2] * z1 + wc[2:3] * z
    ya_ref[:tp] = (b_gate[:tp] * conv).astype(BF16)
    carry_ref[...] = z[tp - SUBLANES:]
    zstp_ref[...] = z[tp - SUBLANES:]

    zs = z_all[tp:]
    for k in range(CONV_WIDTH - 1):
        zhist_ref[k * ts:(k + 1) * ts, :] = state_ref[:, k, :]
    zhist_ref[pl.ds(pl.multiple_of((t + CONV_WIDTH - 1) * ts, ts), ts), :] = zs
    conv_s = wc[0:1] * slot(zhist_ref, t) + wc[1:2] * slot(zhist_ref, t + 1) + wc[2:3] * zs
    ya_ref[tp:] = (b_gate[tp:] * conv_s).astype(BF16)
    zs_ref[:, jnp.maximum(t - (steps - (CONV_WIDTH - 1)), 0), :] = zs


def _row_slab(rows_total, n_steps, width):
    assert rows_total % (16 * n_steps) == 0
    return pl.BlockSpec((rows_total // n_steps, width), lambda i: (i, 0))


def _mix(xp, xs, mod_p, mod_s, g_mix, w_in, w_conv, g_v, w_sg, bias_p, w_sg_s, bias_s, state,
         a, w_ada, b_ada, w_pa, w_pb, w_out, til, steps):
    n_sample = til.n_sample
    n_seq = til.steps // til.tiles_per_seq
    per_seq = lambda rows, width: pl.BlockSpec((None, rows, width), lambda i: (i // til.tiles_per_seq, 0, 0))
    stage_rows = 64
    assert steps >= CONV_WIDTH - 1
    n_side = SIDE_SLOTS[1]
    assert til.steps * SIDE_TN == n_side * D_MODEL, "one modulation-table column block per grid step"
    slab = lambda k, width: _row_slab(k, til.steps, width)
    return pl.pallas_call(
        functools.partial(_mix_kernel, til=til, steps=steps),
        grid=(til.steps,),
        in_specs=[
            til.prompt(D_MODEL), til.sample_block(steps, D_MODEL), *til.mod_tables(MIX_MODS),
            _resident((1, D_MODEL)),
            _HBM,
            _resident((CONV_WIDTH, CONV_DIM)),
            _resident((1, SG_DIM)),
            _resident((SG_GROUPS, CHUNK, CHUNK)),
            _resident((CHUNK, SG_DIM)),
            _resident((steps * steps, 1, SG_DIM)),
            _resident((steps, 1, SG_DIM)),
            til.sample_block(CONV_WIDTH - 1, CONV_DIM),
            _resident(a.shape),
            pl.BlockSpec((D_MODEL, SIDE_TN), _ada_columns(SIDE_SLOTS[0], SIDE_TN)),
            pl.BlockSpec((1, SIDE_TN), _ada_columns(SIDE_SLOTS[0], SIDE_TN)),
            *[pl.BlockSpec((D_MODEL // til.steps, SEG), lambda i, k=k: (i, MIX_COLS // SEG + k))
              for k in range(2 * D_MODEL // SEG)],
            slab(CONV_DIM, D_MODEL), slab(SG_DIM, D_MODEL), slab(D_MODEL, D_MODEL),
        ],
        out_specs=[
            til.both(D_MODEL), til.both(CONV_DIM), til.both(SG_DIM), til.sample_rows(D_MODEL),
            per_seq(SUBLANES, CONV_DIM), per_seq(CHUNK, SG_DIM),
            til.sample_block(CONV_WIDTH - 1, CONV_DIM), til.sample_block(steps, SG_GROUPS, SG_HEAD),
            *_ada_table_specs(n_sample, SIDE_TN),
            slab(D_MODEL, 2 * D_MODEL), slab(CONV_DIM, D_MODEL), slab(SG_DIM, D_MODEL), slab(D_MODEL, D_MODEL),
        ],
        out_shape=[
            jax.ShapeDtypeStruct((til.rows, D_MODEL), BF16),
            jax.ShapeDtypeStruct((til.rows, CONV_DIM), BF16),
            jax.ShapeDtypeStruct((til.rows, SG_DIM), BF16),
            jax.ShapeDtypeStruct((til.steps * til.ts, D_MODEL), F32),
            jax.ShapeDtypeStruct((n_seq, SUBLANES, CONV_DIM), F32),
            jax.ShapeDtypeStruct((n_seq, CHUNK, SG_DIM), F32),
            jax.ShapeDtypeStruct((n_sample, CONV_WIDTH - 1, CONV_DIM), F32),
            jax.ShapeDtypeStruct((n_sample, steps, SG_GROUPS, SG_HEAD), F32),
            jax.ShapeDtypeStruct((n_side, SUBLANES, D_MODEL), F32),
            jax.ShapeDtypeStruct((n_side, n_sample, D_MODEL), F32),
            jax.ShapeDtypeStruct((D_MODEL, 2 * D_MODEL), BF16),
            jax.ShapeDtypeStruct((CONV_DIM, D_MODEL), BF16),
            jax.ShapeDtypeStruct((SG_DIM, D_MODEL), BF16),
            jax.ShapeDtypeStruct((D_MODEL, D_MODEL), BF16),
        ],
        scratch_shapes=[
            pltpu.VMEM((D_MODEL, MIX_COLS), BF16),
            pltpu.VMEM((STAGE_SLOTS, stage_rows, MIX_COLS), F32),
            pltpu.SemaphoreType.DMA((STAGE_SLOTS,)),
            pltpu.VMEM((SUBLANES, CONV_DIM), F32),
            pltpu.VMEM(((CONV_WIDTH - 1 + steps) * til.ts, CONV_DIM), F32),
            pltpu.VMEM((steps * til.ts, SG_DIM), F32),
        ],
        compiler_params=_params(),
        name="mix",
    )(xp, xs, mod_p, mod_s, g_mix, w_in, w_conv, g_v, w_sg, bias_p, w_sg_s, bias_s, state,
      a, w_ada, b_ada, w_in, w_in, w_in, w_in, w_pa, w_pb, w_out)


def _merge_kernel(xp_ref, xs_ref, h_ref, ya_ref, yb_ref, mp_ref, ms_ref, gffn_ref,
                  wgate_ref, wpa_ref, wpb_ref, wout_ref, x1_ref, h2_ref, *, til):
    tp = til.tp
    half = tp // 2
    h = h_ref[...]
    ya = ya_ref[...]
    yb = yb_ref[...]
    halves = []
    for k in range(D_MODEL // SEG):
        cols = slice(k * SEG, (k + 1) * SEG)
        gate_b_cols = slice(D_MODEL + k * SEG, D_MODEL + (k + 1) * SEG)
        branch_a = jax.nn.sigmoid(_dot(h, wgate_ref[:, cols])) * _dot(ya, wpa_ref[:, cols])
        branch_b = jax.nn.sigmoid(_dot(h, wgate_ref[:, gate_b_cols])) * _dot(yb, wpb_ref[:, cols])
        halves.append((branch_a + branch_b).astype(BF16))
    merged = jnp.concatenate(halves, axis=1)

    gffn = gffn_ref[...]
    gtp, gts = til.mods(mp_ref, ms_ref, 0)
    scp, scs = til.mods(mp_ref, ms_ref, 1)
    shp, shs = til.mods(mp_ref, ms_ref, 2)

    def finish(rows, x, out, gate, scale, shift):
        x1 = x + gate * out
        x1_ref[rows] = x1
        h2_ref[rows] = _modulated_norm(x1, gffn, scale, shift).astype(BF16)

    out_a = _dot(merged[:half], wout_ref[...])
    finish(slice(0, half), xp_ref[:half], out_a, gtp, scp, shp)
    out_b = _dot(merged[half:], wout_ref[...])
    finish(slice(half, tp), xp_ref[half:], out_b[:half], gtp, scp, shp)
    finish(slice(tp, tp + til.ts), xs_ref[...], out_b[half:], gts, scs, shs)


def _merge(xp, xs, h, ya, yb, mod_p, mod_s, g_ffn, w_gate, w_pa, w_pb, w_out, til):
    return pl.pallas_call(
        functools.partial(_merge_kernel, til=til),
        grid=(til.steps,),
        in_specs=[
            til.prompt(D_MODEL), til.sample_rows(D_MODEL),
            til.both(D_MODEL), til.both(CONV_DIM), til.both(SG_DIM),
            *til.mod_tables(MERGE_MODS),
            _resident((1, D_MODEL)),
            _resident(w_gate.shape), _resident(w_pa.shape), _resident(w_pb.shape), _resident(w_out.shape),
        ],
        out_specs=[til.both(D_MODEL), til.both(D_MODEL)],
        out_shape=[
            jax.ShapeDtypeStruct((til.rows, D_MODEL), F32),
            jax.ShapeDtypeStruct((til.rows, D_MODEL), BF16),
        ],
        compiler_params=_params(),
        name="merge",
    )(xp, xs, h, ya, yb, mod_p, mod_s, g_ffn, w_gate, w_pa, w_pb, w_out)


def _ffn_in_kernel(h_ref, wg_ref, wu_ref, wdown_ref, o_ref, wdown_b_ref):
    h = h_ref[...]
    for c in range(0, wg_ref.shape[1], FFN_CHUNK):
        cols = slice(c, c + FFN_CHUNK)
        act = jax.nn.silu(_dot(h, wg_ref[:, cols].astype(BF16))) * _dot(h, wu_ref[:, cols].astype(BF16))
        o_ref[:, cols] = act.astype(BF16)
    wdown_b_ref[...] = wdown_ref[...].astype(BF16)


def _ffn_in(h2, w_ffn_in, w_ffn_out, tm, tn):
    rows = h2.shape[0]
    n_blocks = D_FF // tn
    m_steps = rows // tm
    slab_rows = D_FF // (n_blocks * m_steps)
    assert rows % tm == 0 and D_FF % tn == 0 and D_FF % (16 * n_blocks * m_steps) == 0
    slab = pl.BlockSpec((slab_rows, D_MODEL), lambda j, i: (j * m_steps + i, 0))
    return pl.pallas_call(
        _ffn_in_kernel,
        grid=(n_blocks, m_steps),
        in_specs=[
            pl.BlockSpec((tm, D_MODEL), lambda j, i: (i, 0)),
            pl.BlockSpec((D_MODEL, tn), lambda j, i: (0, j)),
            pl.BlockSpec((D_MODEL, tn), lambda j, i: (0, j + n_blocks)),
            slab,
        ],
        out_specs=[pl.BlockSpec((tm, tn), lambda j, i: (i, j)), slab],
        out_shape=[
            jax.ShapeDtypeStruct((rows, D_FF), BF16),
            jax.ShapeDtypeStruct((D_FF, D_MODEL), BF16),
        ],
        compiler_params=pltpu.CompilerParams(
            dimension_semantics=("arbitrary", "arbitrary"), vmem_limit_bytes=VMEM_LIMIT_BYTES),
        name="ffn_in",
    )(h2, w_ffn_in, w_ffn_in, w_ffn_out)


def _ffn_out_kernel(a_ref, x1_ref, mp_ref, ms_ref, gfin_ref, w_ref, yp_ref, ys_ref, *, til):
    tp, rows = til.tp, til.tp + til.ts
    gfin = gfin_ref[...]
    for g in range(til.groups):
        gtp, gts = til.mods(mp_ref, ms_ref, 0, g)
        r0 = g * rows
        out = _dot(a_ref[r0:r0 + rows], w_ref[...])
        yp_ref[g * tp:(g + 1) * tp] = _rms(x1_ref[r0:r0 + tp] + gtp * out[:tp], gfin)
        ys_ref[:, til.position(g), :] = _rms(x1_ref[r0 + tp:r0 + rows] + gts * out[tp:], gfin)


def _ffn_out(act, x1, mod_p, mod_s, g_final, w_ffn_out, til, steps):
    return pl.pallas_call(
        functools.partial(_ffn_out_kernel, til=til),
        grid=(til.steps,),
        in_specs=[
            til.both(D_FF), til.both(D_MODEL),
            *til.mod_tables(FFN_OUT_MODS),
            _resident((1, D_MODEL)),
            _resident(w_ffn_out.shape),
        ],
        out_specs=[til.prompt(D_MODEL), til.sample_block(steps, D_MODEL)],
        out_shape=[
            jax.ShapeDtypeStruct((til.steps * til.groups * til.tp, D_MODEL), F32),
            jax.ShapeDtypeStruct((til.n_sample, steps, D_MODEL), F32),
        ],
        compiler_params=_params(),
        name="ffn_out",
    )(act, x1, mod_p, mod_s, g_final, w_ffn_out)


def _lane_expand(per_group):
    return jnp.repeat(per_group, SG_HEAD, axis=1)


def kernel(x_prompt, x_sample, state_conv, c_prompt, c_sample, g_mix, g_ffn, w_ada, b_ada, w_in, w_conv, g_v,
           w_sg, b_sg, w_pa, w_pb, w_out, w_ffn_in, w_ffn_out, g_final):
    assert g_mix.shape[0] == 1, "one layer"
    batch, seq, _ = x_prompt.shape
    n_sample, steps, _ = x_sample.shape
    assert batch <= SUBLANES and seq % CHUNK == 0
    row = lambda v: v.reshape(1, -1)

    pad = lambda n: jnp.zeros((n, D_MODEL), F32)
    c_all = jnp.concatenate([c_prompt, pad(SUBLANES - batch), c_sample, pad(-(SUBLANES + n_sample) % 16)], axis=0)
    table = lambda m: m.reshape(m.shape[0], SUBLANES, 1, D_MODEL)
    a, mod_p, mod_s = _ada(c_all, w_ada[0], row(b_ada[0]), n_sample, ADA_SLOTS)
    mod_p = table(mod_p)

    xp = x_prompt.reshape(batch * seq, D_MODEL)
    xs = x_sample
    state = state_conv[0]
    til = _Tiling(batch * seq, steps * n_sample, seq, n_sample, steps, tp=256)

    bias_p = _lane_expand(b_sg[0][:, :CHUNK].T)
    w_sg_s = _lane_expand(w_sg[0][:, :steps, :steps].transpose(1, 2, 0).reshape(steps * steps, SG_GROUPS))
    w_sg_s = w_sg_s.reshape(steps * steps, 1, SG_DIM)
    bias_s = _lane_expand(b_sg[0][:, :steps].T).reshape(steps, 1, SG_DIM)
    h, ya, yb, xs_rows, zst_p, vst_p, z_s, vn_s, mod2_p, mod2_s, w_gate_b, w_pa_b, w_pb_b, w_out_b = _mix(
        xp, xs, mod_p, mod_s, row(g_mix[0]), w_in[0], w_conv[0], row(g_v[0]), w_sg[0], bias_p, w_sg_s, bias_s,
        state, a, w_ada[0], row(b_ada[0]), w_pa[0], w_pb[0], w_out[0], til, steps)
    mod2_p = table(mod2_p)

    x1, h2 = _merge(xp, xs_rows, h, ya, yb, mod2_p, mod2_s, row(g_ffn[0]), w_gate_b, w_pa_b, w_pb_b, w_out_b, til)
    act, w_down_b = _ffn_in(h2, w_ffn_in[0], w_ffn_out[0], tm=8 * (til.tp + til.ts), tn=512)
    y_p, y_s = _ffn_out(act, x1, mod2_p, mod2_s, row(g_final), w_down_b, til.regrouped(2), steps)

    y_prompt = y_p.reshape(batch, seq, D_MODEL)
    conv_prompt = zst_p[:, SUBLANES - (CONV_WIDTH - 1):, :][None]
    sgv_prompt = vst_p.reshape(1, batch, CHUNK, SG_GROUPS, SG_HEAD)
    return (y_prompt, y_s, conv_prompt, z_s[None], sgv_prompt, vn_s[None])
```

```python
import functools

import jax
import jax.numpy as jnp
from jax import lax
from jax.experimental import pallas as pl
from jax.experimental.pallas import tpu as pltpu

D_MODEL = 2048
CONV_DIM = D_MODEL // 2
CONV_WIDTH = 3
SG_DIM = D_MODEL // 2
SG_GROUPS = 8
SG_HEAD = SG_DIM // SG_GROUPS
CHUNK = 128
D_FF = 5632
N_MOD = 6
EPS = 1e-6
SEG = 1024
MIX_COLS = 3 * CONV_DIM + 2 * SG_DIM
SUBLANES = 8
FFN_CHUNK = 256
STAGE_SLOTS = 4
VMEM_LIMIT_BYTES = 60 * 1024 * 1024

MOD_ORDER = (1, 0, 2, 4, 3, 5)
ADA_SLOTS, SIDE_SLOTS = (0, 2), (2, 4)
MIX_MODS, MERGE_MODS, FFN_OUT_MODS = (0, 2), (0, 3), (3, 1)
SIDE_TN = 256

F32 = jnp.float32
BF16 = jnp.bfloat16


def _dot(a, b):
    return jnp.dot(a, b, preferred_element_type=F32)


def _rms(x, gain):
    return x * lax.rsqrt(jnp.mean(x * x, axis=-1, keepdims=True) + EPS) * gain


def _modulated_norm(x, gain, scale, shift):
    return _rms(x, gain) * (1 + scale) + shift


def _rows(*parts):
    return jnp.concatenate(parts, axis=0)


def _load_bf16(jobs, stage_ref, sem_ref):
    n_slots, rows, n = stage_ref.shape
    chunks = [(src, col0, dst, r0) for src, col0, dst in jobs for r0 in range(0, dst.shape[0], rows)]
    assert all(dst.shape[1] == n and dst.shape[0] % rows == 0 for _, _, dst in jobs)

    def chunk_copy(idx):
        src, col0, _, r0 = chunks[idx]
        slot = idx % n_slots
        return pltpu.make_async_copy(src.at[pl.ds(r0, rows), pl.ds(col0, n)], stage_ref.at[slot], sem_ref.at[slot])

    ahead = n_slots - 1
    for idx in range(min(ahead, len(chunks))):
        chunk_copy(idx).start()
    for idx, (_, _, dst, r0) in enumerate(chunks):
        if idx + ahead < len(chunks):
            chunk_copy(idx + ahead).start()
        chunk_copy(idx).wait()
        dst[r0:r0 + rows, :] = stage_ref[idx % n_slots].astype(BF16)


def _ada_columns(first_slot, tn):
    per = D_MODEL // tn

    def index_map(j):
        comp = 0
        for slot, c in enumerate(MOD_ORDER):
            comp = jnp.where(first_slot + j // per == slot, c, comp)
        return (0, comp * per + j % per)

    return index_map


def _ada_table_specs(n_sample, tn):
    per = D_MODEL // tn
    return [
        pl.BlockSpec((None, SUBLANES, tn), lambda j: (j // per, 0, j % per)),
        pl.BlockSpec((None, n_sample, tn), lambda j: (j // per, 0, j % per)),
    ]


def _ada_block(a, w_ref, b_ref, mp_ref, ms_ref):
    r = _dot(a, w_ref[...].astype(BF16)) + b_ref[...]
    mp_ref[...] = r[:SUBLANES]
    ms_ref[...] = r[SUBLANES:SUBLANES + ms_ref.shape[0]]


def _ada_kernel(c_ref, w_ref, b_ref, a_ref, mp_ref, ms_ref):
    a = jax.nn.silu(c_ref[...]).astype(BF16)
    a_ref[...] = a
    _ada_block(a, w_ref, b_ref, mp_ref, ms_ref)


def _ada(c_all, w_ada, b_ada, n_sample, slots):
    tn = 256
    first, count = slots
    rows = c_all.shape[0]
    return pl.pallas_call(
        _ada_kernel,
        grid=(count * (D_MODEL // tn),),
        in_specs=[
            pl.BlockSpec((rows, D_MODEL), lambda j: (0, 0)),
            pl.BlockSpec((D_MODEL, tn), _ada_columns(first, tn)),
            pl.BlockSpec((1, tn), _ada_columns(first, tn)),
        ],
        out_specs=[pl.BlockSpec((rows, D_MODEL), lambda j: (0, 0)), *_ada_table_specs(n_sample, tn)],
        out_shape=[
            jax.ShapeDtypeStruct((rows, D_MODEL), BF16),
            jax.ShapeDtypeStruct((count, SUBLANES, D_MODEL), F32),
            jax.ShapeDtypeStruct((count, n_sample, D_MODEL), F32),
        ],
        compiler_params=pltpu.CompilerParams(
            dimension_semantics=("arbitrary",), vmem_limit_bytes=VMEM_LIMIT_BYTES),
        name="ada",
    )(c_all, w_ada, b_ada)


class _Tiling:
    def __init__(self, rows_p, rows_s, seq, n_sample, positions, tp, groups=1):
        n_groups = rows_p // tp
        self.tp = tp
        self.groups = groups
        self.steps = n_groups // groups
        self.ts = rows_s // n_groups
        self.rows = rows_p + rows_s
        self.tiles_per_seq = seq // tp
        self.positions = positions
        self.n_sample = n_sample
        self._args = (rows_p, rows_s, seq, n_sample, positions, tp)
        assert rows_p % tp == 0 and seq % tp == 0 and rows_s % n_groups == 0
        assert self.ts % 16 == 0 and n_sample % self.ts == 0 and rows_s == n_sample * positions
        assert positions % groups == 0 and self.tiles_per_seq % groups == 0

    def regrouped(self, groups):
        return _Tiling(*self._args, groups=groups)

    def group(self, g=0):
        return pl.program_id(0) * self.groups + g

    def prompt(self, width):
        return pl.BlockSpec((self.groups * self.tp, width), lambda i: (i, 0))

    def sample_rows(self, width):
        return pl.BlockSpec((self.groups * self.ts, width), lambda i: (i, 0))

    def sample_block(self, *shape):
        return pl.BlockSpec((self.ts, *shape), lambda i: (i * self.groups // self.positions,) + (0,) * len(shape))

    def position(self, g=0):
        return self.group(g) % self.positions

    def both(self, width):
        return pl.BlockSpec((self.groups * (self.tp + self.ts), width), lambda i: (i, 0))

    def mod_tables(self, slots):
        first, count = slots
        assert first % count == 0
        return [
            pl.BlockSpec((count, SUBLANES, 1, D_MODEL), lambda i: (first // count, 0, 0, 0),
                         pipeline_mode=pl.Buffered(1)),
            pl.BlockSpec((count, self.n_sample, D_MODEL), lambda i: (first // count, 0, 0),
                         pipeline_mode=pl.Buffered(1)),
        ]

    def mods(self, mp_ref, ms_ref, k, g=0):
        s = self.group(g)
        start = pl.multiple_of((s // self.positions) * self.ts, self.ts)
        return mp_ref[k, s // self.tiles_per_seq], ms_ref[k, pl.ds(start, self.ts), :]


def _resident(shape):
    return pl.BlockSpec(shape, lambda i: (0,) * len(shape), pipeline_mode=pl.Buffered(1))


_HBM = pl.BlockSpec(memory_space=pl.ANY)


def _params():
    return pltpu.CompilerParams(dimension_semantics=("arbitrary",), vmem_limit_bytes=VMEM_LIMIT_BYTES)


def _mix_kernel(xp_ref, xs_ref, mp_ref, ms_ref, gmix_ref, w_hbm, wconv_ref, gv_ref,
                wsg_ref, bsgp_ref, wsgs_ref, bsgs_ref, state_ref,
                a_ref, wada_ref, bada_ref, wg0_ref, wg1_ref, wg2_ref, wg3_ref, wpa_ref, wpb_ref, wout_ref,
                h_ref, ya_ref, yb_ref, xsr_ref, zstp_ref, vstp_ref, zs_ref, vs_ref,
                mpr_ref, msr_ref, wgate_b_ref, wpa_b_ref, wpb_b_ref, wout_b_ref,
                w_ref, stage_ref, sem_ref, carry_ref, zhist_ref, vhist_ref, *, til, steps):
    i = pl.program_id(0)
    tp, ts = til.tp, til.ts
    t = til.position()

    def slot(ref, k):
        start = k * ts if isinstance(k, int) else pl.multiple_of(k * ts, ts)
        return ref[pl.ds(start, ts), :]

    @pl.when(i == 0)
    def _():
        _load_bf16([(w_hbm, 0, w_ref)], stage_ref, sem_ref)
        vhist_ref[...] = jnp.zeros_like(vhist_ref)

    @pl.when(i % til.tiles_per_seq == 0)
    def _():
        carry_ref[...] = jnp.zeros_like(carry_ref)

    gmix = gmix_ref[...]
    scp, scs = til.mods(mp_ref, ms_ref, 0)
    shp, shs = til.mods(mp_ref, ms_ref, 1)
    xs = xs_ref[:, t, :]
    xsr_ref[...] = xs
    h = _rows(_modulated_norm(xp_ref[...], gmix, scp, shp), _modulated_norm(xs, gmix, scs, shs)).astype(BF16)
    h_ref[...] = h

    for k, wg_ref in enumerate((wg0_ref, wg1_ref, wg2_ref, wg3_ref)):
        wgate_b_ref[:, k * SEG:(k + 1) * SEG] = wg_ref[...].astype(BF16)
    wpa_b_ref[...] = wpa_ref[...].astype(BF16)
    wpb_b_ref[...] = wpb_ref[...].astype(BF16)
    wout_b_ref[...] = wout_ref[...].astype(BF16)
    proj = lambda k: _dot(h_ref[...], w_ref[:, k * SEG:(k + 1) * SEG])

    vn = _rms(jax.nn.gelu(proj(4)), gv_ref[...])
    z_all = proj(1) * proj(2)
    vstp_ref[...] = vn[tp - CHUNK:tp]
    vb = vn[:tp].astype(BF16)
    causal = (lax.broadcasted_iota(jnp.int32, (CHUNK, CHUNK), 0)
              >= lax.broadcasted_iota(jnp.int32, (CHUNK, CHUNK), 1))
    wgs = [jnp.where(causal, wsg_ref[g], 0.0).astype(BF16) for g in range(SG_GROUPS)]
    bias = bsgp_ref[...]
    chunks = []
    for c in range(tp // CHUNK):
        rows = slice(c * CHUNK, (c + 1) * CHUNK)
        parts = [_dot(wgs[g], vb[rows, g * SG_HEAD:(g + 1) * SG_HEAD]) for g in range(SG_GROUPS)]
        chunks.append(jnp.concatenate(parts, axis=1) + bias)
    gu = jax.nn.gelu(proj(3))
    b_gate = proj(0)
    yb_ref[:tp] = (gu[:tp] * _rows(*chunks)).astype(BF16)

    vs = vn[tp:]
    for g in range(SG_GROUPS):
        vs_ref[:, t, g, :] = vs[:, g * SG_HEAD:(g + 1) * SG_HEAD]
    vhist_ref[pl.ds(pl.multiple_of(t * ts, ts), ts), :] = vs
    sp = bsgs_ref[t]
    for s in range(steps):
        w_ts = jnp.where(s <= t, wsgs_ref[t * steps + s], 0.0)
        sp = sp + w_ts * slot(vhist_ref, s)
    yb_ref[tp:] = (gu[tp:] * sp).astype(BF16)

    wc = wconv_ref[...]
    z = z_all[:tp]
    carry = carry_ref[...]
    prev2, prev1 = carry[SUBLANES - 2:SUBLANES - 1], carry[SUBLANES - 1:SUBLANES]
    row = lax.broadcasted_iota(jnp.int32, (SUBLANES, CONV_DIM), 0)
    z1 = pltpu.roll(z, 1, 0)
    z2 = pltpu.roll(z, 2, 0)
    z1 = _rows(jnp.where(row == 0, prev1, z1[:SUBLANES]), z1[SUBLANES:])
    z2 = _rows(jnp.where(row == 0, prev2, jnp.where(row == 1, prev1, z2[:SUBLANES])), z2[SUBLANES:])
    conv = wc[0:1] * z2 + wc[1:2] * z1 + wc[2:3] * z
    ya_ref[:tp] = (b_gate[:tp] * conv).astype(BF16)
    carry_ref[...] = z[tp - SUBLANES:]
    zstp_ref[...] = z[tp - SUBLANES:]

    zs = z_all[tp:]
    for k in range(CONV_WIDTH - 1):
        zhist_ref[k * ts:(k + 1) * ts, :] = state_ref[:, k, :]
    zhist_ref[pl.ds(pl.multiple_of((t + CONV_WIDTH - 1) * ts, ts), ts), :] = zs
    conv_s = wc[0:1] * slot(zhist_ref, t) + wc[1:2] * slot(zhist_ref, t + 1) + wc[2:3] * zs
    ya_ref[tp:] = (b_gate[tp:] * conv_s).astype(BF16)
    zs_ref[:, jnp.maximum(t - (steps - (CONV_WIDTH - 1)), 0), :] = zs

    _ada_block(a_ref[...], wada_ref, bada_ref, mpr_ref, msr_ref)


def _row_slab(rows_total, n_steps, width):
    assert rows_total % (16 * n_steps) == 0
    return pl.BlockSpec((rows_total // n_steps, width), lambda i: (i, 0))


def _mix(xp, xs, mod_p, mod_s, g_mix, w_in, w_conv, g_v, w_sg, bias_p, w_sg_s, bias_s, state,
         a, w_ada, b_ada, w_pa, w_pb, w_out, til, steps):
    n_sample = til.n_sample
    n_seq = til.steps // til.tiles_per_seq
    per_seq = lambda rows, width: pl.BlockSpec((None, rows, width), lambda i: (i // til.tiles_per_seq, 0, 0))
    stage_rows = 64
    assert steps >= CONV_WIDTH - 1
    n_side = SIDE_SLOTS[1]
    assert til.steps * SIDE_TN == n_side * D_MODEL, "one modulation-table column block per grid step"
    slab = lambda k, width: _row_slab(k, til.steps, width)
    return pl.pallas_call(
        functools.partial(_mix_kernel, til=til, steps=steps),
        grid=(til.steps,),
        in_specs=[
            til.prompt(D_MODEL), til.sample_block(steps, D_MODEL), *til.mod_tables(MIX_MODS),
            _resident((1, D_MODEL)),
            _HBM,
            _resident((CONV_WIDTH, CONV_DIM)),
            _resident((1, SG_DIM)),
            _resident((SG_GROUPS, CHUNK, CHUNK)),
            _resident((CHUNK, SG_DIM)),
            _resident((steps * steps, 1, SG_DIM)),
            _resident((steps, 1, SG_DIM)),
            til.sample_block(CONV_WIDTH - 1, CONV_DIM),
            _resident(a.shape),
            pl.BlockSpec((D_MODEL, SIDE_TN), _ada_columns(SIDE_SLOTS[0], SIDE_TN)),
            pl.BlockSpec((1, SIDE_TN), _ada_columns(SIDE_SLOTS[0], SIDE_TN)),
            *[pl.BlockSpec((D_MODEL // til.steps, SEG), lambda i, k=k: (i, MIX_COLS // SEG + k))
              for k in range(2 * D_MODEL // SEG)],
            slab(CONV_DIM, D_MODEL), slab(SG_DIM, D_MODEL), slab(D_MODEL, D_MODEL),
        ],
        out_specs=[
            til.both(D_MODEL), til.both(CONV_DIM), til.both(SG_DIM), til.sample_rows(D_MODEL),
            per_seq(SUBLANES, CONV_DIM), per_seq(CHUNK, SG_DIM),
            til.sample_block(CONV_WIDTH - 1, CONV_DIM), til.sample_block(steps, SG_GROUPS, SG_HEAD),
            *_ada_table_specs(n_sample, SIDE_TN),
            slab(D_MODEL, 2 * D_MODEL), slab(CONV_DIM, D_MODEL), slab(SG_DIM, D_MODEL), slab(D_MODEL, D_MODEL),
        ],
        out_shape=[
            jax.ShapeDtypeStruct((til.rows, D_MODEL), BF16),
            jax.ShapeDtypeStruct((til.rows, CONV_DIM), BF16),
            jax.ShapeDtypeStruct((til.rows, SG_DIM), BF16),
            jax.ShapeDtypeStruct((til.steps * til.ts, D_MODEL), F32),
            jax.ShapeDtypeStruct((n_seq, SUBLANES, CONV_DIM), F32),
            jax.ShapeDtypeStruct((n_seq, CHUNK, SG_DIM), F32),
            jax.ShapeDtypeStruct((n_sample, CONV_WIDTH - 1, CONV_DIM), F32),
            jax.ShapeDtypeStruct((n_sample, steps, SG_GROUPS, SG_HEAD), F32),
            jax.ShapeDtypeStruct((n_side, SUBLANES, D_MODEL), F32),
            jax.ShapeDtypeStruct((n_side, n_sample, D_MODEL), F32),
            jax.ShapeDtypeStruct((D_MODEL, 2 * D_MODEL), BF16),
            jax.ShapeDtypeStruct((CONV_DIM, D_MODEL), BF16),
            jax.ShapeDtypeStruct((SG_DIM, D_MODEL), BF16),
            jax.ShapeDtypeStruct((D_MODEL, D_MODEL), BF16),
        ],
        scratch_shapes=[
            pltpu.VMEM((D_MODEL, MIX_COLS), BF16),
            pltpu.VMEM((STAGE_SLOTS, stage_rows, MIX_COLS), F32),
            pltpu.SemaphoreType.DMA((STAGE_SLOTS,)),
            pltpu.VMEM((SUBLANES, CONV_DIM), F32),
            pltpu.VMEM(((CONV_WIDTH - 1 + steps) * til.ts, CONV_DIM), F32),
            pltpu.VMEM((steps * til.ts, SG_DIM), F32),
        ],
        compiler_params=_params(),
        name="mix",
    )(xp, xs, mod_p, mod_s, g_mix, w_in, w_conv, g_v, w_sg, bias_p, w_sg_s, bias_s, state,
      a, w_ada, b_ada, w_in, w_in, w_in, w_in, w_pa, w_pb, w_out)


def _merge_kernel(xp_ref, xs_ref, h_ref, ya_ref, yb_ref, mp_ref, ms_ref, gffn_ref,
                  wgate_ref, wpa_ref, wpb_ref, wout_ref, x1_ref, h2_ref, *, til):
    tp = til.tp
    h = h_ref[...]
    ya = ya_ref[...]
    yb = yb_ref[...]
    halves = []
    for k in range(D_MODEL // SEG):
        cols = slice(k * SEG, (k + 1) * SEG)
        gate_b_cols = slice(D_MODEL + k * SEG, D_MODEL + (k + 1) * SEG)
        branch_a = jax.nn.sigmoid(_dot(h, wgate_ref[:, cols])) * _dot(ya, wpa_ref[:, cols])
        branch_b = jax.nn.sigmoid(_dot(h, wgate_ref[:, gate_b_cols])) * _dot(yb, wpb_ref[:, cols])
        halves.append((branch_a + branch_b).astype(BF16))
    merged = jnp.concatenate(halves, axis=1)

    gffn = gffn_ref[...]
    gtp, gts = til.mods(mp_ref, ms_ref, 0)
    scp, scs = til.mods(mp_ref, ms_ref, 1)
    shp, shs = til.mods(mp_ref, ms_ref, 2)

    def finish(rows, x, out, gate, scale, shift):
        x1 = x + gate * out
        x1_ref[rows] = x1
        h2_ref[rows] = _modulated_norm(x1, gffn, scale, shift).astype(BF16)

    out = _dot(merged, wout_ref[...])
    finish(slice(0, tp), xp_ref[...], out[:tp], gtp, scp, shp)
    finish(slice(tp, tp + til.ts), xs_ref[...], out[tp:], gts, scs, shs)


def _merge(xp, xs, h, ya, yb, mod_p, mod_s, g_ffn, w_gate, w_pa, w_pb, w_out, til):
    return pl.pallas_call(
        functools.partial(_merge_kernel, til=til),
        grid=(til.steps,),
        in_specs=[
            til.prompt(D_MODEL), til.sample_rows(D_MODEL),
            til.both(D_MODEL), til.both(CONV_DIM), til.both(SG_DIM),
            *til.mod_tables(MERGE_MODS),
            _resident((1, D_MODEL)),
            _resident(w_gate.shape), _resident(w_pa.shape), _resident(w_pb.shape), _resident(w_out.shape),
        ],
        out_specs=[til.both(D_MODEL), til.both(D_MODEL)],
        out_shape=[
            jax.ShapeDtypeStruct((til.rows, D_MODEL), F32),
            jax.ShapeDtypeStruct((til.rows, D_MODEL), BF16),
        ],
        compiler_params=_params(),
        name="merge",
    )(xp, xs, h, ya, yb, mod_p, mod_s, g_ffn, w_gate, w_pa, w_pb, w_out)


def _ffn_in_kernel(h_ref, wg_ref, wu_ref, wdown_ref, o_ref, wdown_b_ref):
    h = h_ref[...]
    for c in range(0, wg_ref.shape[1], FFN_CHUNK):
        cols = slice(c, c + FFN_CHUNK)
        act = jax.nn.silu(_dot(h, wg_ref[:, cols].astype(BF16))) * _dot(h, wu_ref[:, cols].astype(BF16))
        o_ref[:, cols] = act.astype(BF16)
    wdown_b_ref[...] = wdown_ref[...].astype(BF16)


def _ffn_in(h2, w_ffn_in, w_ffn_out, tm, tn):
    rows = h2.shape[0]
    n_blocks = D_FF // tn
    m_steps = rows // tm
    slab_rows = D_FF // (n_blocks * m_steps)
    assert rows % tm == 0 and D_FF % tn == 0 and D_FF % (16 * n_blocks * m_steps) == 0
    slab = pl.BlockSpec((slab_rows, D_MODEL), lambda j, i: (j * m_steps + i, 0))
    return pl.pallas_call(
        _ffn_in_kernel,
        grid=(n_blocks, m_steps),
        in_specs=[
            pl.BlockSpec((tm, D_MODEL), lambda j, i: (i, 0)),
            pl.BlockSpec((D_MODEL, tn), lambda j, i: (0, j)),
            pl.BlockSpec((D_MODEL, tn), lambda j, i: (0, j + n_blocks)),
            slab,
        ],
        out_specs=[pl.BlockSpec((tm, tn), lambda j, i: (i, j)), slab],
        out_shape=[
            jax.ShapeDtypeStruct((rows, D_FF), BF16),
            jax.ShapeDtypeStruct((D_FF, D_MODEL), BF16),
        ],
        compiler_params=pltpu.CompilerParams(
            dimension_semantics=("arbitrary", "arbitrary"), vmem_limit_bytes=VMEM_LIMIT_BYTES),
        name="ffn_in",
    )(h2, w_ffn_in, w_ffn_in, w_ffn_out)


def _ffn_out_kernel(a_ref, x1_ref, mp_ref, ms_ref, gfin_ref, w_ref, yp_ref, ys_ref, *, til):
    tp, rows = til.tp, til.tp + til.ts
    gfin = gfin_ref[...]
    for g in range(til.groups):
        gtp, gts = til.mods(mp_ref, ms_ref, 0, g)
        r0 = g * rows
        out = _dot(a_ref[r0:r0 + rows], w_ref[...])
        yp_ref[g * tp:(g + 1) * tp] = _rms(x1_ref[r0:r0 + tp] + gtp * out[:tp], gfin)
        ys_ref[:, til.position(g), :] = _rms(x1_ref[r0 + tp:r0 + rows] + gts * out[tp:], gfin)


def _ffn_out(act, x1, mod_p, mod_s, g_final, w_ffn_out, til, steps):
    return pl.pallas_call(
        functools.partial(_ffn_out_kernel, til=til),
        grid=(til.steps,),
        in_specs=[
            til.both(D_FF), til.both(D_MODEL),
            *til.mod_tables(FFN_OUT_MODS),
            _resident((1, D_MODEL)),
            _resident(w_ffn_out.shape),
        ],
        out_specs=[til.prompt(D_MODEL), til.sample_block(steps, D_MODEL)],
        out_shape=[
            jax.ShapeDtypeStruct((til.steps * til.groups * til.tp, D_MODEL), F32),
            jax.ShapeDtypeStruct((til.n_sample, steps, D_MODEL), F32),
        ],
        compiler_params=_params(),
        name="ffn_out",
    )(act, x1, mod_p, mod_s, g_final, w_ffn_out)


def _lane_expand(per_group):
    return jnp.repeat(per_group, SG_HEAD, axis=1)


def kernel(x_prompt, x_sample, state_conv, c_prompt, c_sample, g_mix, g_ffn, w_ada, b_ada, w_in, w_conv, g_v,
           w_sg, b_sg, w_pa, w_pb, w_out, w_ffn_in, w_ffn_out, g_final):
    assert g_mix.shape[0] == 1, "one layer"
    batch, seq, _ = x_prompt.shape
    n_sample, steps, _ = x_sample.shape
    assert batch <= SUBLANES and seq % CHUNK == 0
    row = lambda v: v.reshape(1, -1)

    pad = lambda n: jnp.zeros((n, D_MODEL), F32)
    c_all = jnp.concatenate([c_prompt, pad(SUBLANES - batch), c_sample, pad(-(SUBLANES + n_sample) % 16)], axis=0)
    table = lambda m: m.reshape(m.shape[0], SUBLANES, 1, D_MODEL)
    a, mod_p, mod_s = _ada(c_all, w_ada[0], row(b_ada[0]), n_sample, ADA_SLOTS)
    mod_p = table(mod_p)

    xp = x_prompt.reshape(batch * seq, D_MODEL)
    xs = x_sample
    state = state_conv[0]
    til = _Tiling(batch * seq, steps * n_sample, seq, n_sample, steps, tp=256)

    bias_p = _lane_expand(b_sg[0][:, :CHUNK].T)
    w_sg_s = _lane_expand(w_sg[0][:, :steps, :steps].transpose(1, 2, 0).reshape(steps * steps, SG_GROUPS))
    w_sg_s = w_sg_s.reshape(steps * steps, 1, SG_DIM)
    bias_s = _lane_expand(b_sg[0][:, :steps].T).reshape(steps, 1, SG_DIM)
    h, ya, yb, xs_rows, zst_p, vst_p, z_s, vn_s, mod2_p, mod2_s, w_gate_b, w_pa_b, w_pb_b, w_out_b = _mix(
        xp, xs, mod_p, mod_s, row(g_mix[0]), w_in[0], w_conv[0], row(g_v[0]), w_sg[0], bias_p, w_sg_s, bias_s,
        state, a, w_ada[0], row(b_ada[0]), w_pa[0], w_pb[0], w_out[0], til, steps)
    mod2_p = table(mod2_p)

    x1, h2 = _merge(xp, xs_rows, h, ya, yb, mod2_p, mod2_s, row(g_ffn[0]), w_gate_b, w_pa_b, w_pb_b, w_out_b, til)
    act, w_down_b = _ffn_in(h2, w_ffn_in[0], w_ffn_out[0], tm=8 * (til.tp + til.ts), tn=512)
    y_p, y_s = _ffn_out(act, x1, mod2_p, mod2_s, row(g_final), w_down_b, til.regrouped(2), steps)

    y_prompt = y_p.reshape(batch, seq, D_MODEL)
    conv_prompt = zst_p[:, SUBLANES - (CONV_WIDTH - 1):, :][None]
    sgv_prompt = vst_p.reshape(1, batch, CHUNK, SG_GROUPS, SG_HEAD)
    return (y_prompt, y_s, conv_prompt, z_s[None], sgv_prompt, vn_s[None])
```

```python
import functools

import jax
import jax.numpy as jnp
from jax import lax
from jax.experimental import pallas as pl
from jax.experimental.pallas import tpu as pltpu

D_MODEL = 2048
CONV_DIM = D_MODEL // 2
CONV_WIDTH = 3
SG_DIM = D_MODEL // 2
SG_GROUPS = 8
SG_HEAD = SG_DIM // SG_GROUPS
CHUNK = 128
D_FF = 5632
N_MOD = 6
EPS = 1e-6
SEG = 1024
MIX_COLS = 3 * CONV_DIM + 2 * SG_DIM
SUBLANES = 8
FFN_CHUNK = 256
STAGE_SLOTS = 4
VMEM_LIMIT_BYTES = 60 * 1024 * 1024

MOD_ORDER = (1, 0, 2, 4, 3, 5)
ADA_SLOTS, SIDE_SLOTS = (0, 2), (2, 4)
MIX_MODS, MERGE_MODS, FFN_OUT_MODS = (0, 2), (0, 3), (3, 1)
SIDE_TN = 256

F32 = jnp.float32
BF16 = jnp.bfloat16


def _dot(a, b):
    return jnp.dot(a, b, preferred_element_type=F32)


def _rms(x, gain):
    return x * lax.rsqrt(jnp.mean(x * x, axis=-1, keepdims=True) + EPS) * gain


def _modulated_norm(x, gain, scale, shift):
    return _rms(x, gain) * (1 + scale) + shift


def _rows(*parts):
    return jnp.concatenate(parts, axis=0)


def _load_bf16(jobs, stage_ref, sem_ref):
    n_slots, rows, n = stage_ref.shape
    chunks = [(src, col0, dst, r0) for src, col0, dst in jobs for r0 in range(0, dst.shape[0], rows)]
    assert all(dst.shape[1] == n and dst.shape[0] % rows == 0 for _, _, dst in jobs)

    def chunk_copy(idx):
        src, col0, _, r0 = chunks[idx]
        slot = idx % n_slots
        return pltpu.make_async_copy(src.at[pl.ds(r0, rows), pl.ds(col0, n)], stage_ref.at[slot], sem_ref.at[slot])

    ahead = n_slots - 1
    for idx in range(min(ahead, len(chunks))):
        chunk_copy(idx).start()
    for idx, (_, _, dst, r0) in enumerate(chunks):
        if idx + ahead < len(chunks):
            chunk_copy(idx + ahead).start()
        chunk_copy(idx).wait()
        dst[r0:r0 + rows, :] = stage_ref[idx % n_slots].astype(BF16)


def _ada_columns(first_slot, tn):
    per = D_MODEL // tn

    def index_map(j):
        comp = 0
        for slot, c in enumerate(MOD_ORDER):
            comp = jnp.where(first_slot + j // per == slot, c, comp)
        return (0, comp * per + j % per)

    return index_map


def _ada_table_specs(n_sample, tn):
    per = D_MODEL // tn
    return [
        pl.BlockSpec((None, SUBLANES, tn), lambda j: (j // per, 0, j % per)),
        pl.BlockSpec((None, n_sample, tn), lambda j: (j // per, 0, j % per)),
    ]


def _ada_block(a, w_ref, b_ref, mp_ref, ms_ref):
    r = _dot(a, w_ref[...].astype(BF16)) + b_ref[...]
    mp_ref[...] = r[:SUBLANES]
    ms_ref[...] = r[SUBLANES:SUBLANES + ms_ref.shape[0]]


def _ada_kernel(c_ref, w_ref, b_ref, a_ref, mp_ref, ms_ref):
    a = jax.nn.silu(c_ref[...]).astype(BF16)
    a_ref[...] = a
    _ada_block(a, w_ref, b_ref, mp_ref, ms_ref)


def _ada(c_all, w_ada, b_ada, n_sample, slots):
    tn = 1024
    first, count = slots
    rows = c_all.shape[0]
    return pl.pallas_call(
        _ada_kernel,
        grid=(count * (D_MODEL // tn),),
        in_specs=[
            pl.BlockSpec((rows, D_MODEL), lambda j: (0, 0)),
            pl.BlockSpec((D_MODEL, tn), _ada_columns(first, tn)),
            pl.BlockSpec((1, tn), _ada_columns(first, tn)),
        ],
        out_specs=[pl.BlockSpec((rows, D_MODEL), lambda j: (0, 0)), *_ada_table_specs(n_sample, tn)],
        out_shape=[
            jax.ShapeDtypeStruct((rows, D_MODEL), BF16),
            jax.ShapeDtypeStruct((count, SUBLANES, D_MODEL), F32),
            jax.ShapeDtypeStruct((count, n_sample, D_MODEL), F32),
        ],
        compiler_params=pltpu.CompilerParams(
            dimension_semantics=("arbitrary",), vmem_limit_bytes=VMEM_LIMIT_BYTES),
        name="ada",
    )(c_all, w_ada, b_ada)


class _Tiling:
    def __init__(self, rows_p, rows_s, seq, n_sample, positions, tp, groups=1):
        n_groups = rows_p // tp
        self.tp = tp
        self.groups = groups
        self.steps = n_groups // groups
        self.ts = rows_s // n_groups
        self.rows = rows_p + rows_s
        self.tiles_per_seq = seq // tp
        self.positions = positions
        self.n_sample = n_sample
        self._args = (rows_p, rows_s, seq, n_sample, positions, tp)
        assert rows_p % tp == 0 and seq % tp == 0 and rows_s % n_groups == 0
        assert self.ts % 16 == 0 and n_sample % self.ts == 0 and rows_s == n_sample * positions
        assert positions % groups == 0 and self.tiles_per_seq % groups == 0

    def regrouped(self, groups):
        return _Tiling(*self._args, groups=groups)

    def group(self, g=0):
        return pl.program_id(0) * self.groups + g

    def prompt(self, width):
        return pl.BlockSpec((self.groups * self.tp, width), lambda i: (i, 0))

    def sample_rows(self, width):
        return pl.BlockSpec((self.groups * self.ts, width), lambda i: (i, 0))

    def sample_block(self, *shape):
        return pl.BlockSpec((self.ts, *shape), lambda i: (i * self.groups // self.positions,) + (0,) * len(shape))

    def position(self, g=0):
        return self.group(g) % self.positions

    def both(self, width):
        return pl.BlockSpec((self.groups * (self.tp + self.ts), width), lambda i: (i, 0))

    def mod_tables(self, slots):
        first, count = slots
        assert first % count == 0
        return [
            pl.BlockSpec((count, SUBLANES, 1, D_MODEL), lambda i: (first // count, 0, 0, 0),
                         pipeline_mode=pl.Buffered(1)),
            pl.BlockSpec((count, self.n_sample, D_MODEL), lambda i: (first // count, 0, 0),
                         pipeline_mode=pl.Buffered(1)),
        ]

    def mods(self, mp_ref, ms_ref, k, g=0):
        s = self.group(g)
        start = pl.multiple_of((s // self.positions) * self.ts, self.ts)
        return mp_ref[k, s // self.tiles_per_seq], ms_ref[k, pl.ds(start, self.ts), :]


def _resident(shape):
    return pl.BlockSpec(shape, lambda i: (0,) * len(shape), pipeline_mode=pl.Buffered(1))


_HBM = pl.BlockSpec(memory_space=pl.ANY)


def _params():
    return pltpu.CompilerParams(dimension_semantics=("arbitrary",), vmem_limit_bytes=VMEM_LIMIT_BYTES)


def _mix_kernel(xp_ref, xs_ref, mp_ref, ms_ref, gmix_ref, w_hbm, wconv_ref, gv_ref,
                wsg_ref, bsgp_ref, wsgs_ref, bsgs_ref, state_ref,
                a_ref, wada_ref, bada_ref, wg0_ref, wg1_ref, wg2_ref, wg3_ref, wpa_ref, wpb_ref, wout_ref,
                h_ref, ya_ref, yb_ref, xsr_ref, zstp_ref, vstp_ref, zs_ref, vs_ref,
                mpr_ref, msr_ref, wgate_b_ref, wpa_b_ref, wpb_b_ref, wout_b_ref,
                w_ref, stage_ref, sem_ref, carry_ref, zhist_ref, vhist_ref, *, til, steps):
    i = pl.program_id(0)
    tp, ts = til.tp, til.ts
    t = til.position()

    def slot(ref, k):
        start = k * ts if isinstance(k, int) else pl.multiple_of(k * ts, ts)
        return ref[pl.ds(start, ts), :]

    @pl.when(i == 0)
    def _():
        _load_bf16([(w_hbm, 0, w_ref)], stage_ref, sem_ref)
        vhist_ref[...] = jnp.zeros_like(vhist_ref)

    @pl.when(i % til.tiles_per_seq == 0)
    def _():
        carry_ref[...] = jnp.zeros_like(carry_ref)

    gmix = gmix_ref[...]
    scp, scs = til.mods(mp_ref, ms_ref, 0)
    shp, shs = til.mods(mp_ref, ms_ref, 1)
    xs = xs_ref[:, t, :]
    xsr_ref[...] = xs
    h = _rows(_modulated_norm(xp_ref[...], gmix, scp, shp), _modulated_norm(xs, gmix, scs, shs)).astype(BF16)
    h_ref[...] = h

    for k, wg_ref in enumerate((wg0_ref, wg1_ref, wg2_ref, wg3_ref)):
        wgate_b_ref[:, k * SEG:(k + 1) * SEG] = wg_ref[...].astype(BF16)
    wpa_b_ref[...] = wpa_ref[...].astype(BF16)
    wpb_b_ref[...] = wpb_ref[...].astype(BF16)
    wout_b_ref[...] = wout_ref[...].astype(BF16)
    proj = lambda k: _dot(h_ref[...], w_ref[:, k * SEG:(k + 1) * SEG])

    vn = _rms(jax.nn.gelu(proj(4)), gv_ref[...])
    z_all = proj(1) * proj(2)
    vstp_ref[...] = vn[tp - CHUNK:tp]
    vb = vn[:tp].astype(BF16)
    causal = (lax.broadcasted_iota(jnp.int32, (CHUNK, CHUNK), 0)
              >= lax.broadcasted_iota(jnp.int32, (CHUNK, CHUNK), 1))
    wgs = [jnp.where(causal, wsg_ref[g], 0.0).astype(BF16) for g in range(SG_GROUPS)]
    bias = bsgp_ref[...]
    chunks = []
    for c in range(tp // CHUNK):
        rows = slice(c * CHUNK, (c + 1) * CHUNK)
        parts = [_dot(wgs[g], vb[rows, g * SG_HEAD:(g + 1) * SG_HEAD]) for g in range(SG_GROUPS)]
        chunks.append(jnp.concatenate(parts, axis=1) + bias)
    gu = jax.nn.gelu(proj(3))
    b_gate = proj(0)
    yb_ref[:tp] = (gu[:tp] * _rows(*chunks)).astype(BF16)

    vs = vn[tp:]
    for g in range(SG_GROUPS):
        vs_ref[:, t, g, :] = vs[:, g * SG_HEAD:(g + 1) * SG_HEAD]
    vhist_ref[pl.ds(pl.multiple_of(t * ts, ts), ts), :] = vs
    sp = bsgs_ref[t]
    for s in range(steps):
        w_ts = jnp.where(s <= t, wsgs_ref[t * steps + s], 0.0)
        sp = sp + w_ts * slot(vhist_ref, s)
    yb_ref[tp:] = (gu[tp:] * sp).astype(BF16)

    wc = wconv_ref[...]
    z = z_all[:tp]
    carry = carry_ref[...]
    prev2, prev1 = carry[SUBLANES - 2:SUBLANES - 1], carry[SUBLANES - 1:SUBLANES]
    row = lax.broadcasted_iota(jnp.int32, (SUBLANES, CONV_DIM), 0)
    z1 = pltpu.roll(z, 1, 0)
    z2 = pltpu.roll(z, 2, 0)
    z1 = _rows(jnp.where(row == 0, prev1, z1[:SUBLANES]), z1[SUBLANES:])
    z2 = _rows(jnp.where(row == 0, prev2, jnp.where(row == 1, prev1, z2[:SUBLANES])), z2[SUBLANES:])
    conv = wc[0:1] * z2 + wc[1:2] * z1 + wc[2:3] * z
    ya_ref[:tp] = (b_gate[:tp] * conv).astype(BF16)
    carry_ref[...] = z[tp - SUBLANES:]
    zstp_ref[...] = z[tp - SUBLANES:]

    zs = z_all[tp:]
    for k in range(CONV_WIDTH - 1):
        zhist_ref[k * ts:(k + 1) * ts, :] = state_ref[:, k, :]
    zhist_ref[pl.ds(pl.multiple_of((t + CONV_WIDTH - 1) * ts, ts), ts), :] = zs
    conv_s = wc[0:1] * slot(zhist_ref, t) + wc[1:2] * slot(zhist_ref, t + 1) + wc[2:3] * zs
    ya_ref[tp:] = (b_gate[tp:] * conv_s).astype(BF16)
    zs_ref[:, jnp.maximum(t - (steps - (CONV_WIDTH - 1)), 0), :] = zs

    _ada_block(a_ref[...], wada_ref, bada_ref, mpr_ref, msr_ref)


def _row_slab(rows_total, n_steps, width):
    assert rows_total % (16 * n_steps) == 0
    return pl.BlockSpec((rows_total // n_steps, width), lambda i: (i, 0))


def _mix(xp, xs, mod_p, mod_s, g_mix, w_in, w_conv, g_v, w_sg, bias_p, w_sg_s, bias_s, state,
         a, w_ada, b_ada, w_pa, w_pb, w_out, til, steps):
    n_sample = til.n_sample
    n_seq = til.steps // til.tiles_per_seq
    per_seq = lambda rows, width: pl.BlockSpec((None, rows, width), lambda i: (i // til.tiles_per_seq, 0, 0))
    stage_rows = 64
    assert steps >= CONV_WIDTH - 1
    n_side = SIDE_SLOTS[1]
    assert til.steps * SIDE_TN == n_side * D_MODEL, "one modulation-table column block per grid step"
    slab = lambda k, width: _row_slab(k, til.steps, width)
    return pl.pallas_call(
        functools.partial(_mix_kernel, til=til, steps=steps),
        grid=(til.steps,),
        in_specs=[
            til.prompt(D_MODEL), til.sample_block(steps, D_MODEL), *til.mod_tables(MIX_MODS),
            _resident((1, D_MODEL)),
            _HBM,
            _resident((CONV_WIDTH, CONV_DIM)),
            _resident((1, SG_DIM)),
            _resident((SG_GROUPS, CHUNK, CHUNK)),
            _resident((CHUNK, SG_DIM)),
            _resident((steps * steps, 1, SG_DIM)),
            _resident((steps, 1, SG_DIM)),
            til.sample_block(CONV_WIDTH - 1, CONV_DIM),
            _resident(a.shape),
            pl.BlockSpec((D_MODEL, SIDE_TN), _ada_columns(SIDE_SLOTS[0], SIDE_TN)),
            pl.BlockSpec((1, SIDE_TN), _ada_columns(SIDE_SLOTS[0], SIDE_TN)),
            *[pl.BlockSpec((D_MODEL // til.steps, SEG), lambda i, k=k: (i, MIX_COLS // SEG + k))
              for k in range(2 * D_MODEL // SEG)],
            slab(CONV_DIM, D_MODEL), slab(SG_DIM, D_MODEL), slab(D_MODEL, D_MODEL),
        ],
        out_specs=[
            til.both(D_MODEL), til.both(CONV_DIM), til.both(SG_DIM), til.sample_rows(D_MODEL),
            per_seq(SUBLANES, CONV_DIM), per_seq(CHUNK, SG_DIM),
            til.sample_block(CONV_WIDTH - 1, CONV_DIM), til.sample_block(steps, SG_GROUPS, SG_HEAD),
            *_ada_table_specs(n_sample, SIDE_TN),
            slab(D_MODEL, 2 * D_MODEL), slab(CONV_DIM, D_MODEL), slab(SG_DIM, D_MODEL), slab(D_MODEL, D_MODEL),
        ],
        out_shape=[
            jax.ShapeDtypeStruct((til.rows, D_MODEL), BF16),
            jax.ShapeDtypeStruct((til.rows, CONV_DIM), BF16),
            jax.ShapeDtypeStruct((til.rows, SG_DIM), BF16),
            jax.ShapeDtypeStruct((til.steps * til.ts, D_MODEL), F32),
            jax.ShapeDtypeStruct((n_seq, SUBLANES, CONV_DIM), F32),
            jax.ShapeDtypeStruct((n_seq, CHUNK, SG_DIM), F32),
            jax.ShapeDtypeStruct((n_sample, CONV_WIDTH - 1, CONV_DIM), F32),
            jax.ShapeDtypeStruct((n_sample, steps, SG_GROUPS, SG_HEAD), F32),
            jax.ShapeDtypeStruct((n_side, SUBLANES, D_MODEL), F32),
            jax.ShapeDtypeStruct((n_side, n_sample, D_MODEL), F32),
            jax.ShapeDtypeStruct((D_MODEL, 2 * D_MODEL), BF16),
            jax.ShapeDtypeStruct((CONV_DIM, D_MODEL), BF16),
            jax.ShapeDtypeStruct((SG_DIM, D_MODEL), BF16),
            jax.ShapeDtypeStruct((D_MODEL, D_MODEL), BF16),
        ],
        scratch_shapes=[
            pltpu.VMEM((D_MODEL, MIX_COLS), BF16),
            pltpu.VMEM((STAGE_SLOTS, stage_rows, MIX_COLS), F32),
            pltpu.SemaphoreType.DMA((STAGE_SLOTS,)),
            pltpu.VMEM((SUBLANES, CONV_DIM), F32),
            pltpu.VMEM(((CONV_WIDTH - 1 + steps) * til.ts, CONV_DIM), F32),
            pltpu.VMEM((steps * til.ts, SG_DIM), F32),
        ],
        compiler_params=_params(),
        name="mix",
    )(xp, xs, mod_p, mod_s, g_mix, w_in, w_conv, g_v, w_sg, bias_p, w_sg_s, bias_s, state,
      a, w_ada, b_ada, w_in, w_in, w_in, w_in, w_pa, w_pb, w_out)


def _merge_kernel(xp_ref, xs_ref, h_ref, ya_ref, yb_ref, mp_ref, ms_ref, gffn_ref,
                  wgate_ref, wpa_ref, wpb_ref, wout_ref, x1_ref, h2_ref, *, til):
    tp = til.tp
    h = h_ref[...]
    ya = ya_ref[...]
    yb = yb_ref[...]
    halves = []
    for k in range(D_MODEL // SEG):
        cols = slice(k * SEG, (k + 1) * SEG)
        gate_b_cols = slice(D_MODEL + k * SEG, D_MODEL + (k + 1) * SEG)
        branch_a = jax.nn.sigmoid(_dot(h, wgate_ref[:, cols])) * _dot(ya, wpa_ref[:, cols])
        branch_b = jax.nn.sigmoid(_dot(h, wgate_ref[:, gate_b_cols])) * _dot(yb, wpb_ref[:, cols])
        halves.append((branch_a + branch_b).astype(BF16))
    merged = jnp.concatenate(halves, axis=1)

    gffn = gffn_ref[...]
    gtp, gts = til.mods(mp_ref, ms_ref, 0)
    scp, scs = til.mods(mp_ref, ms_ref, 1)
    shp, shs = til.mods(mp_ref, ms_ref, 2)

    def finish(rows, x, out, gate, scale, shift):
        x1 = x + gate * out
        x1_ref[rows] = x1
        h2_ref[rows] = _modulated_norm(x1, gffn, scale, shift).astype(BF16)

    out = _dot(merged, wout_ref[...])
    finish(slice(0, tp), xp_ref[...], out[:tp], gtp, scp, shp)
    finish(slice(tp, tp + til.ts), xs_ref[...], out[tp:], gts, scs, shs)


def _merge(xp, xs, h, ya, yb, mod_p, mod_s, g_ffn, w_gate, w_pa, w_pb, w_out, til):
    return pl.pallas_call(
        functools.partial(_merge_kernel, til=til),
        grid=(til.steps,),
        in_specs=[
            til.prompt(D_MODEL), til.sample_rows(D_MODEL),
            til.both(D_MODEL), til.both(CONV_DIM), til.both(SG_DIM),
            *til.mod_tables(MERGE_MODS),
            _resident((1, D_MODEL)),
            _resident(w_gate.shape), _resident(w_pa.shape), _resident(w_pb.shape), _resident(w_out.shape),
        ],
        out_specs=[til.both(D_MODEL), til.both(D_MODEL)],
        out_shape=[
            jax.ShapeDtypeStruct((til.rows, D_MODEL), F32),
            jax.ShapeDtypeStruct((til.rows, D_MODEL), BF16),
        ],
        compiler_params=_params(),
        name="merge",
    )(xp, xs, h, ya, yb, mod_p, mod_s, g_ffn, w_gate, w_pa, w_pb, w_out)


def _ffn_in_kernel(h_ref, wg_ref, wu_ref, wdown_ref, o_ref, wdown_b_ref):
    h = h_ref[...]
    for c in range(0, wg_ref.shape[1], FFN_CHUNK):
        cols = slice(c, c + FFN_CHUNK)
        act = jax.nn.silu(_dot(h, wg_ref[:, cols].astype(BF16))) * _dot(h, wu_ref[:, cols].astype(BF16))
        o_ref[:, cols] = act.astype(BF16)
    wdown_b_ref[...] = wdown_ref[...].astype(BF16)


def _ffn_in(h2, w_ffn_in, w_ffn_out, tm, tn):
    rows = h2.shape[0]
    n_blocks = D_FF // tn
    m_steps = rows // tm
    slab_rows = D_FF // (n_blocks * m_steps)
    assert rows % tm == 0 and D_FF % tn == 0 and D_FF % (16 * n_blocks * m_steps) == 0
    slab = pl.BlockSpec((slab_rows, D_MODEL), lambda j, i: (j * m_steps + i, 0))
    return pl.pallas_call(
        _ffn_in_kernel,
        grid=(n_blocks, m_steps),
        in_specs=[
            pl.BlockSpec((tm, D_MODEL), lambda j, i: (i, 0)),
            pl.BlockSpec((D_MODEL, tn), lambda j, i: (0, j)),
            pl.BlockSpec((D_MODEL, tn), lambda j, i: (0, j + n_blocks)),
            slab,
        ],
        out_specs=[pl.BlockSpec((tm, tn), lambda j, i: (i, j)), slab],
        out_shape=[
            jax.ShapeDtypeStruct((rows, D_FF), BF16),
            jax.ShapeDtypeStruct((D_FF, D_MODEL), BF16),
        ],
        compiler_params=pltpu.CompilerParams(
            dimension_semantics=("arbitrary", "arbitrary"), vmem_limit_bytes=VMEM_LIMIT_BYTES),
        name="ffn_in",
    )(h2, w_ffn_in, w_ffn_in, w_ffn_out)


def _ffn_out_kernel(a_ref, x1_ref, mp_ref, ms_ref, gfin_ref, w_ref, yp_ref, ys_ref, *, til):
    tp, rows = til.tp, til.tp + til.ts
    gfin = gfin_ref[...]
    out_all = _dot(a_ref[...], w_ref[...])
    for g in range(til.groups):
        gtp, gts = til.mods(mp_ref, ms_ref, 0, g)
        r0 = g * rows
        out = out_all[r0:r0 + rows]
        yp_ref[g * tp:(g + 1) * tp] = _rms(x1_ref[r0:r0 + tp] + gtp * out[:tp], gfin)
        ys_ref[:, til.position(g), :] = _rms(x1_ref[r0 + tp:r0 + rows] + gts * out[tp:], gfin)


def _ffn_out(act, x1, mod_p, mod_s, g_final, w_ffn_out, til, steps):
    return pl.pallas_call(
        functools.partial(_ffn_out_kernel, til=til),
        grid=(til.steps,),
        in_specs=[
            til.both(D_FF), til.both(D_MODEL),
            *til.mod_tables(FFN_OUT_MODS),
            _resident((1, D_MODEL)),
            _resident(w_ffn_out.shape),
        ],
        out_specs=[til.prompt(D_MODEL), til.sample_block(steps, D_MODEL)],
        out_shape=[
            jax.ShapeDtypeStruct((til.steps * til.groups * til.tp, D_MODEL), F32),
            jax.ShapeDtypeStruct((til.n_sample, steps, D_MODEL), F32),
        ],
        compiler_params=_params(),
        name="ffn_out",
    )(act, x1, mod_p, mod_s, g_final, w_ffn_out)


def _lane_expand(per_group):
    return jnp.repeat(per_group, SG_HEAD, axis=1)


def kernel(x_prompt, x_sample, state_conv, c_prompt, c_sample, g_mix, g_ffn, w_ada, b_ada, w_in, w_conv, g_v,
           w_sg, b_sg, w_pa, w_pb, w_out, w_ffn_in, w_ffn_out, g_final):
    assert g_mix.shape[0] == 1, "one layer"
    batch, seq, _ = x_prompt.shape
    n_sample, steps, _ = x_sample.shape
    assert batch <= SUBLANES and seq % CHUNK == 0
    row = lambda v: v.reshape(1, -1)

    pad = lambda n: jnp.zeros((n, D_MODEL), F32)
    c_all = jnp.concatenate([c_prompt, pad(SUBLANES - batch), c_sample, pad(-(SUBLANES + n_sample) % 16)], axis=0)
    table = lambda m: m.reshape(m.shape[0], SUBLANES, 1, D_MODEL)
    a, mod_p, mod_s = _ada(c_all, w_ada[0], row(b_ada[0]), n_sample, ADA_SLOTS)
    mod_p = table(mod_p)

    xp = x_prompt.reshape(batch * seq, D_MODEL)
    xs = x_sample
    state = state_conv[0]
    til = _Tiling(batch * seq, steps * n_sample, seq, n_sample, steps, tp=256)

    bias_p = _lane_expand(b_sg[0][:, :CHUNK].T)
    w_sg_s = _lane_expand(w_sg[0][:, :steps, :steps].transpose(1, 2, 0).reshape(steps * steps, SG_GROUPS))
    w_sg_s = w_sg_s.reshape(steps * steps, 1, SG_DIM)
    bias_s = _lane_expand(b_sg[0][:, :steps].T).reshape(steps, 1, SG_DIM)
    h, ya, yb, xs_rows, zst_p, vst_p, z_s, vn_s, mod2_p, mod2_s, w_gate_b, w_pa_b, w_pb_b, w_out_b = _mix(
        xp, xs, mod_p, mod_s, row(g_mix[0]), w_in[0], w_conv[0], row(g_v[0]), w_sg[0], bias_p, w_sg_s, bias_s,
        state, a, w_ada[0], row(b_ada[0]), w_pa[0], w_pb[0], w_out[0], til, steps)
    mod2_p = table(mod2_p)

    x1, h2 = _merge(xp, xs_rows, h, ya, yb, mod2_p, mod2_s, row(g_ffn[0]), w_gate_b, w_pa_b, w_pb_b, w_out_b, til)
    act, w_down_b = _ffn_in(h2, w_ffn_in[0], w_ffn_out[0], tm=8 * (til.tp + til.ts), tn=512)
    y_p, y_s = _ffn_out(act, x1, mod2_p, mod2_s, row(g_final), w_down_b, til.regrouped(2), steps)

    y_prompt = y_p.reshape(batch, seq, D_MODEL)
    conv_prompt = zst_p[:, SUBLANES - (CONV_WIDTH - 1):, :][None]
    sgv_prompt = vst_p.reshape(1, batch, CHUNK, SG_GROUPS, SG_HEAD)
    return (y_prompt, y_s, conv_prompt, z_s[None], sgv_prompt, vn_s[None])
```

```python
import functools

import jax
import jax.numpy as jnp
from jax import lax
from jax.experimental import pallas as pl
from jax.experimental.pallas import tpu as pltpu

D_MODEL = 2048
CONV_DIM = D_MODEL // 2
CONV_WIDTH = 3
SG_DIM = D_MODEL // 2
SG_GROUPS = 8
SG_HEAD = SG_DIM // SG_GROUPS
CHUNK = 128
D_FF = 5632
EPS = 1e-6
SEG = 1024
MIX_COLS = 3 * CONV_DIM + 2 * SG_DIM
SUBLANES = 8
FFN_CHUNK = 256
STAGE_SLOTS = 4
VMEM_LIMIT_BYTES = 60 * 1024 * 1024

MOD_ORDER = (1, 0, 2, 4, 3, 5)
ADA_SLOTS, SIDE_SLOTS = (0, 2), (2, 4)
MIX_MODS, MERGE_MODS, FFN_OUT_MODS = (0, 2), (0, 3), (3, 1)
SIDE_TN = 256

F32 = jnp.float32
BF16 = jnp.bfloat16


def _dot(a, b):
    return jnp.dot(a, b, preferred_element_type=F32)


def _rms(x, gain):
    return x * lax.rsqrt(jnp.mean(x * x, axis=-1, keepdims=True) + EPS) * gain


def _modulated_norm(x, gain, scale, shift):
    return _rms(x, gain * (1 + scale)) + shift


def _rows(*parts):
    return jnp.concatenate(parts, axis=0)


def _load_bf16(jobs, stage_ref, sem_ref):
    n_slots, rows, n = stage_ref.shape
    chunks = [(src, col0, dst, r0) for src, col0, dst in jobs for r0 in range(0, dst.shape[0], rows)]
    assert all(dst.shape[1] == n and dst.shape[0] % rows == 0 for _, _, dst in jobs)

    def chunk_copy(idx):
        src, col0, _, r0 = chunks[idx]
        slot = idx % n_slots
        return pltpu.make_async_copy(src.at[pl.ds(r0, rows), pl.ds(col0, n)], stage_ref.at[slot], sem_ref.at[slot])

    ahead = n_slots - 1
    for idx in range(min(ahead, len(chunks))):
        chunk_copy(idx).start()
    for idx, (_, _, dst, r0) in enumerate(chunks):
        if idx + ahead < len(chunks):
            chunk_copy(idx + ahead).start()
        chunk_copy(idx).wait()
        dst[r0:r0 + rows, :] = stage_ref[idx % n_slots].astype(BF16)


def _ada_columns(first_slot, tn):
    per = D_MODEL // tn

    def index_map(j):
        comp = 0
        for slot, c in enumerate(MOD_ORDER):
            comp = jnp.where(first_slot + j // per == slot, c, comp)
        return (0, comp * per + j % per)

    return index_map


def _ada_table_specs(n_sample, tn):
    per = D_MODEL // tn
    return [
        pl.BlockSpec((None, SUBLANES, 1, tn), lambda j: (j // per, 0, 0, j % per)),
        pl.BlockSpec((None, n_sample, tn), lambda j: (j // per, 0, j % per)),
    ]


def _ada_block(a, w_ref, b_ref, mp_ref, ms_ref):
    r = _dot(a, w_ref[...].astype(BF16)) + b_ref[...]
    for n in range(SUBLANES):
        mp_ref[n] = r[n:n + 1]
    ms_ref[...] = r[SUBLANES:SUBLANES + ms_ref.shape[0]]


def _ada_kernel(cp_ref, cs_ref, w_ref, b_ref, a_ref, mp_ref, ms_ref, rows_ref):
    batch, n_sample = cp_ref.shape[0], cs_ref.shape[0]
    rows_ref[...] = jnp.zeros_like(rows_ref)
    rows_ref[0:batch] = jax.nn.silu(cp_ref[...])
    rows_ref[SUBLANES:SUBLANES + n_sample] = jax.nn.silu(cs_ref[...])
    a = rows_ref[...].astype(BF16)
    a_ref[...] = a
    _ada_block(a, w_ref, b_ref, mp_ref, ms_ref)


def _ada(c_prompt, c_sample, w_ada, b_ada, slots):
    tn = 1024
    first, count = slots
    batch, n_sample = c_prompt.shape[0], c_sample.shape[0]
    rows = -(-(SUBLANES + n_sample) // 16) * 16
    assert batch <= SUBLANES
    return pl.pallas_call(
        _ada_kernel,
        grid=(count * (D_MODEL // tn),),
        in_specs=[
            pl.BlockSpec((batch, D_MODEL), lambda j: (0, 0)),
            pl.BlockSpec((n_sample, D_MODEL), lambda j: (0, 0)),
            pl.BlockSpec((D_MODEL, tn), _ada_columns(first, tn)),
            pl.BlockSpec((1, tn), _ada_columns(first, tn)),
        ],
        out_specs=[pl.BlockSpec((rows, D_MODEL), lambda j: (0, 0)), *_ada_table_specs(n_sample, tn)],
        out_shape=[
            jax.ShapeDtypeStruct((rows, D_MODEL), BF16),
            jax.ShapeDtypeStruct((count, SUBLANES, 1, D_MODEL), F32),
            jax.ShapeDtypeStruct((count, n_sample, D_MODEL), F32),
        ],
        scratch_shapes=[pltpu.VMEM((rows, D_MODEL), F32)],
        compiler_params=pltpu.CompilerParams(
            dimension_semantics=("arbitrary",), vmem_limit_bytes=VMEM_LIMIT_BYTES),
        name="ada",
    )(c_prompt, c_sample, w_ada, b_ada)


class _Tiling:
    def __init__(self, rows_p, rows_s, seq, n_sample, positions, tp, groups=1):
        n_groups = rows_p // tp
        self.tp = tp
        self.groups = groups
        self.steps = n_groups // groups
        self.ts = rows_s // n_groups
        self.rows = rows_p + rows_s
        self.tiles_per_seq = seq // tp
        self.positions = positions
        self.n_sample = n_sample
        self._args = (rows_p, rows_s, seq, n_sample, positions, tp)
        assert rows_p % tp == 0 and seq % tp == 0 and rows_s % n_groups == 0
        assert self.ts % 16 == 0 and n_sample % self.ts == 0 and rows_s == n_sample * positions
        assert positions % groups == 0 and self.tiles_per_seq % groups == 0

    def regrouped(self, groups):
        return _Tiling(*self._args, groups=groups)

    def group(self, g=0):
        return pl.program_id(0) * self.groups + g

    def prompt(self, width):
        return pl.BlockSpec((self.groups * self.tp, width), lambda i: (i, 0))

    def sample_rows(self, width):
        return pl.BlockSpec((self.groups * self.ts, width), lambda i: (i, 0))

    def sample_block(self, *shape):
        return pl.BlockSpec((self.ts, *shape), lambda i: (i * self.groups // self.positions,) + (0,) * len(shape))

    def position(self, g=0):
        return self.group(g) % self.positions

    def both(self, width):
        return pl.BlockSpec((self.groups * (self.tp + self.ts), width), lambda i: (i, 0))

    def mod_tables(self, slots):
        first, count = slots
        assert first % count == 0
        return [
            pl.BlockSpec((count, SUBLANES, 1, D_MODEL), lambda i: (first // count, 0, 0, 0),
                         pipeline_mode=pl.Buffered(1)),
            pl.BlockSpec((count, self.n_sample, D_MODEL), lambda i: (first // count, 0, 0),
                         pipeline_mode=pl.Buffered(1)),
        ]

    def mods(self, mp_ref, ms_ref, k, g=0):
        s = self.group(g)
        start = pl.multiple_of((s // self.positions) * self.ts, self.ts)
        return mp_ref[k, s // self.tiles_per_seq], ms_ref[k, pl.ds(start, self.ts), :]


def _resident(shape):
    return pl.BlockSpec(shape, lambda i: (0,) * len(shape), pipeline_mode=pl.Buffered(1))


_HBM = pl.BlockSpec(memory_space=pl.ANY)


def _params():
    return pltpu.CompilerParams(dimension_semantics=("arbitrary",), vmem_limit_bytes=VMEM_LIMIT_BYTES)


def _mix_kernel(xp_ref, xs_ref, mp_ref, ms_ref, gmix_ref, w_hbm, wconv_ref, gv_ref,
                wsg_ref, bsgp_ref, wsgs_ref, bsgs_ref, state_ref,
                a_ref, wada_ref, bada_ref, wg0_ref, wg1_ref, wg2_ref, wg3_ref, wpa_ref, wpb_ref, wout_ref,
                h_ref, ya_ref, yb_ref, xsr_ref, zstp_ref, vstp_ref, zs_ref, vs_ref,
                mpr_ref, msr_ref, wgate_b_ref, wpa_b_ref, wpb_b_ref, wout_b_ref,
                w_ref, stage_ref, sem_ref, carry_ref, zhist_ref, vhist_ref, *, til, steps):
    assert CONV_WIDTH == 3, "the conv mixer below is written for three taps"
    i = pl.program_id(0)
    tp, ts = til.tp, til.ts
    t = til.position()

    def slot(ref, k):
        start = k * ts if isinstance(k, int) else pl.multiple_of(k * ts, ts)
        return ref[pl.ds(start, ts), :]

    @pl.when(i == 0)
    def _():
        _load_bf16([(w_hbm, 0, w_ref)], stage_ref, sem_ref)
        vhist_ref[...] = jnp.zeros_like(vhist_ref)

    @pl.when(i % til.tiles_per_seq == 0)
    def _():
        carry_ref[...] = jnp.zeros_like(carry_ref)

    gmix = gmix_ref[...]
    scp, scs = til.mods(mp_ref, ms_ref, 0)
    shp, shs = til.mods(mp_ref, ms_ref, 1)
    xs = xs_ref[:, t, :]
    xsr_ref[...] = xs
    h = _rows(_modulated_norm(xp_ref[...], gmix, scp, shp), _modulated_norm(xs, gmix, scs, shs)).astype(BF16)
    h_ref[...] = h

    for k, wg_ref in enumerate((wg0_ref, wg1_ref, wg2_ref, wg3_ref)):
        wgate_b_ref[:, k * SEG:(k + 1) * SEG] = wg_ref[...].astype(BF16)
    wpa_b_ref[...] = wpa_ref[...].astype(BF16)
    wpb_b_ref[...] = wpb_ref[...].astype(BF16)
    wout_b_ref[...] = wout_ref[...].astype(BF16)
    proj = lambda k: _dot(h_ref[...], w_ref[:, k * SEG:(k + 1) * SEG])

    vn = _rms(jax.nn.gelu(proj(4)), gv_ref[...])
    z_all = proj(1) * proj(2)
    vstp_ref[...] = vn[tp - CHUNK:tp]
    vb = vn[:tp].astype(BF16)
    causal = (lax.broadcasted_iota(jnp.int32, (CHUNK, CHUNK), 0)
              >= lax.broadcasted_iota(jnp.int32, (CHUNK, CHUNK), 1))
    wgs = [jnp.where(causal, wsg_ref[g], 0.0).astype(BF16) for g in range(SG_GROUPS)]
    bias = bsgp_ref[...]
    chunks = []
    for c in range(tp // CHUNK):
        rows = slice(c * CHUNK, (c + 1) * CHUNK)
        parts = [_dot(wgs[g], vb[rows, g * SG_HEAD:(g + 1) * SG_HEAD]) for g in range(SG_GROUPS)]
        chunks.append(jnp.concatenate(parts, axis=1) + bias)
    gu = jax.nn.gelu(proj(3))
    b_gate = proj(0)
    yb_ref[:tp] = (gu[:tp] * _rows(*chunks)).astype(BF16)

    vs = vn[tp:]
    for g in range(SG_GROUPS):
        vs_ref[:, t, g, :] = vs[:, g * SG_HEAD:(g + 1) * SG_HEAD]
    vhist_ref[pl.ds(pl.multiple_of(t * ts, ts), ts), :] = vs
    sp = bsgs_ref[t]
    for s in range(steps):
        w_ts = jnp.where(s <= t, wsgs_ref[t * steps + s], 0.0)
        sp = sp + w_ts * slot(vhist_ref, s)
    yb_ref[tp:] = (gu[tp:] * sp).astype(BF16)

    wc = wconv_ref[...]
    z = z_all[:tp]
    carry = carry_ref[...]
    prev2, prev1 = carry[SUBLANES - 2:SUBLANES - 1], carry[SUBLANES - 1:SUBLANES]
    row = lax.broadcasted_iota(jnp.int32, (SUBLANES, CONV_DIM), 0)
    z1 = pltpu.roll(z, 1, 0)
    z2 = pltpu.roll(z, 2, 0)
    z1 = _rows(jnp.where(row == 0, prev1, z1[:SUBLANES]), z1[SUBLANES:])
    z2 = _rows(jnp.where(row == 0, prev2, jnp.where(row == 1, prev1, z2[:SUBLANES])), z2[SUBLANES:])
    conv = wc[0:1] * z2 + wc[1:2] * z1 + wc[2:3] * z
    ya_ref[:tp] = (b_gate[:tp] * conv).astype(BF16)
    carry_ref[...] = z[tp - SUBLANES:]
    zstp_ref[...] = z[tp - (CONV_WIDTH - 1):]

    zs = z_all[tp:]
    for k in range(CONV_WIDTH - 1):
        zhist_ref[k * ts:(k + 1) * ts, :] = state_ref[:, k, :]
    zhist_ref[pl.ds(pl.multiple_of((t + CONV_WIDTH - 1) * ts, ts), ts), :] = zs
    conv_s = wc[0:1] * slot(zhist_ref, t) + wc[1:2] * slot(zhist_ref, t + 1) + wc[2:3] * zs
    ya_ref[tp:] = (b_gate[tp:] * conv_s).astype(BF16)
    zs_ref[:, jnp.maximum(t - (steps - (CONV_WIDTH - 1)), 0), :] = zs

    _ada_block(a_ref[...], wada_ref, bada_ref, mpr_ref, msr_ref)


def _row_slab(rows_total, n_steps, width):
    assert rows_total % (16 * n_steps) == 0
    return pl.BlockSpec((rows_total // n_steps, width), lambda i: (i, 0))


def _mix(xp, xs, mod_p, mod_s, g_mix, w_in, w_conv, g_v, w_sg, bias_p, w_sg_s, bias_s, state,
         a, w_ada, b_ada, w_pa, w_pb, w_out, til, steps):
    n_sample = til.n_sample
    n_seq = til.steps // til.tiles_per_seq
    per_seq = lambda rows, width: pl.BlockSpec((None, rows, width), lambda i: (i // til.tiles_per_seq, 0, 0))
    stage_rows = 64
    assert steps >= CONV_WIDTH - 1
    n_side = SIDE_SLOTS[1]
    assert til.steps * SIDE_TN == n_side * D_MODEL, "one modulation-table column block per grid step"
    slab = lambda k, width: _row_slab(k, til.steps, width)
    return pl.pallas_call(
        functools.partial(_mix_kernel, til=til, steps=steps),
        grid=(til.steps,),
        in_specs=[
            til.prompt(D_MODEL), til.sample_block(steps, D_MODEL), *til.mod_tables(MIX_MODS),
            _resident((1, D_MODEL)),
            _HBM,
            _resident((CONV_WIDTH, CONV_DIM)),
            _resident((1, SG_DIM)),
            _resident((SG_GROUPS, CHUNK, CHUNK)),
            _resident((CHUNK, SG_DIM)),
            _resident((steps * steps, 1, SG_DIM)),
            _resident((steps, 1, SG_DIM)),
            til.sample_block(CONV_WIDTH - 1, CONV_DIM),
            _resident(a.shape),
            pl.BlockSpec((D_MODEL, SIDE_TN), _ada_columns(SIDE_SLOTS[0], SIDE_TN)),
            pl.BlockSpec((1, SIDE_TN), _ada_columns(SIDE_SLOTS[0], SIDE_TN)),
            *[pl.BlockSpec((D_MODEL // til.steps, SEG), lambda i, k=k: (i, MIX_COLS // SEG + k))
              for k in range(2 * D_MODEL // SEG)],
            slab(CONV_DIM, D_MODEL), slab(SG_DIM, D_MODEL), slab(D_MODEL, D_MODEL),
        ],
        out_specs=[
            til.both(D_MODEL), til.both(CONV_DIM), til.both(SG_DIM), til.sample_rows(D_MODEL),
            per_seq(CONV_WIDTH - 1, CONV_DIM), per_seq(CHUNK, SG_DIM),
            til.sample_block(CONV_WIDTH - 1, CONV_DIM), til.sample_block(steps, SG_GROUPS, SG_HEAD),
            *_ada_table_specs(n_sample, SIDE_TN),
            slab(D_MODEL, 2 * D_MODEL), slab(CONV_DIM, D_MODEL), slab(SG_DIM, D_MODEL), slab(D_MODEL, D_MODEL),
        ],
        out_shape=[
            jax.ShapeDtypeStruct((til.rows, D_MODEL), BF16),
            jax.ShapeDtypeStruct((til.rows, CONV_DIM), BF16),
            jax.ShapeDtypeStruct((til.rows, SG_DIM), BF16),
            jax.ShapeDtypeStruct((til.steps * til.ts, D_MODEL), F32),
            jax.ShapeDtypeStruct((n_seq, CONV_WIDTH - 1, CONV_DIM), F32),
            jax.ShapeDtypeStruct((n_seq, CHUNK, SG_DIM), F32),
            jax.ShapeDtypeStruct((n_sample, CONV_WIDTH - 1, CONV_DIM), F32),
            jax.ShapeDtypeStruct((n_sample, steps, SG_GROUPS, SG_HEAD), F32),
            jax.ShapeDtypeStruct((n_side, SUBLANES, 1, D_MODEL), F32),
            jax.ShapeDtypeStruct((n_side, n_sample, D_MODEL), F32),
            jax.ShapeDtypeStruct((D_MODEL, 2 * D_MODEL), BF16),
            jax.ShapeDtypeStruct((CONV_DIM, D_MODEL), BF16),
            jax.ShapeDtypeStruct((SG_DIM, D_MODEL), BF16),
            jax.ShapeDtypeStruct((D_MODEL, D_MODEL), BF16),
        ],
        scratch_shapes=[
            pltpu.VMEM((D_MODEL, MIX_COLS), BF16),
            pltpu.VMEM((STAGE_SLOTS, stage_rows, MIX_COLS), F32),
            pltpu.SemaphoreType.DMA((STAGE_SLOTS,)),
            pltpu.VMEM((SUBLANES, CONV_DIM), F32),
            pltpu.VMEM(((CONV_WIDTH - 1 + steps) * til.ts, CONV_DIM), F32),
            pltpu.VMEM((steps * til.ts, SG_DIM), F32),
        ],
        compiler_params=_params(),
        name="mix",
    )(xp, xs, mod_p, mod_s, g_mix, w_in, w_conv, g_v, w_sg, bias_p, w_sg_s, bias_s, state,
      a, w_ada, b_ada, w_in, w_in, w_in, w_in, w_pa, w_pb, w_out)


def _merge_kernel(xp_ref, xs_ref, h_ref, ya_ref, yb_ref, mp_ref, ms_ref, gffn_ref,
                  wgate_ref, wpa_ref, wpb_ref, wout_ref, x1_ref, h2_ref, *, til):
    tp = til.tp
    h = h_ref[...]
    ya = ya_ref[...]
    yb = yb_ref[...]
    halves = []
    for k in range(D_MODEL // SEG):
        cols = slice(k * SEG, (k + 1) * SEG)
        gate_b_cols = slice(D_MODEL + k * SEG, D_MODEL + (k + 1) * SEG)
        gate_a = jax.nn.sigmoid(_dot(h, wgate_ref[:, cols]))
        gate_b = jax.nn.sigmoid(_dot(h, wgate_ref[:, gate_b_cols]))
        halves.append((gate_a * _dot(ya, wpa_ref[:, cols]) + gate_b * _dot(yb, wpb_ref[:, cols])).astype(BF16))
    merged = jnp.concatenate(halves, axis=1)

    gffn = gffn_ref[...]
    gtp, gts = til.mods(mp_ref, ms_ref, 0)
    scp, scs = til.mods(mp_ref, ms_ref, 1)
    shp, shs = til.mods(mp_ref, ms_ref, 2)

    def finish(rows, x, out, gate, scale, shift):
        x1 = x + gate * out
        x1_ref[rows] = x1
        h2_ref[rows] = _modulated_norm(x1, gffn, scale, shift).astype(BF16)

    out = _dot(merged, wout_ref[...])
    finish(slice(0, tp), xp_ref[...], out[:tp], gtp, scp, shp)
    finish(slice(tp, tp + til.ts), xs_ref[...], out[tp:], gts, scs, shs)


def _merge(xp, xs, h, ya, yb, mod_p, mod_s, g_ffn, w_gate, w_pa, w_pb, w_out, til):
    return pl.pallas_call(
        functools.partial(_merge_kernel, til=til),
        grid=(til.steps,),
        in_specs=[
            til.prompt(D_MODEL), til.sample_rows(D_MODEL),
            til.both(D_MODEL), til.both(CONV_DIM), til.both(SG_DIM),
            *til.mod_tables(MERGE_MODS),
            _resident((1, D_MODEL)),
            _resident(w_gate.shape), _resident(w_pa.shape), _resident(w_pb.shape), _resident(w_out.shape),
        ],
        out_specs=[til.both(D_MODEL), til.both(D_MODEL)],
        out_shape=[
            jax.ShapeDtypeStruct((til.rows, D_MODEL), F32),
            jax.ShapeDtypeStruct((til.rows, D_MODEL), BF16),
        ],
        compiler_params=_params(),
        name="merge",
    )(xp, xs, h, ya, yb, mod_p, mod_s, g_ffn, w_gate, w_pa, w_pb, w_out)


def _ffn_in_kernel(h_ref, wg_ref, wu_ref, wdown_ref, o_ref, wdown_b_ref):
    h = h_ref[...]
    for c in range(0, wg_ref.shape[1], FFN_CHUNK):
        cols = slice(c, c + FFN_CHUNK)
        act = jax.nn.silu(_dot(h, wg_ref[:, cols].astype(BF16))) * _dot(h, wu_ref[:, cols].astype(BF16))
        o_ref[:, cols] = act.astype(BF16)
    wdown_b_ref[...] = wdown_ref[...].astype(BF16)


def _ffn_in(h2, w_ffn_in, w_ffn_out, tm, tn):
    rows = h2.shape[0]
    n_blocks = D_FF // tn
    m_steps = rows // tm
    slab_rows = D_FF // (n_blocks * m_steps)
    assert rows % tm == 0 and D_FF % tn == 0 and D_FF % (16 * n_blocks * m_steps) == 0
    slab = pl.BlockSpec((slab_rows, D_MODEL), lambda j, i: (j * m_steps + i, 0))
    return pl.pallas_call(
        _ffn_in_kernel,
        grid=(n_blocks, m_steps),
        in_specs=[
            pl.BlockSpec((tm, D_MODEL), lambda j, i: (i, 0)),
            pl.BlockSpec((D_MODEL, tn), lambda j, i: (0, j)),
            pl.BlockSpec((D_MODEL, tn), lambda j, i: (0, j + n_blocks)),
            slab,
        ],
        out_specs=[pl.BlockSpec((tm, tn), lambda j, i: (i, j)), slab],
        out_shape=[
            jax.ShapeDtypeStruct((rows, D_FF), BF16),
            jax.ShapeDtypeStruct((D_FF, D_MODEL), BF16),
        ],
        compiler_params=pltpu.CompilerParams(
            dimension_semantics=("arbitrary", "arbitrary"), vmem_limit_bytes=VMEM_LIMIT_BYTES),
        name="ffn_in",
    )(h2, w_ffn_in, w_ffn_in, w_ffn_out)


def _ffn_out_kernel(a_ref, x1_ref, mp_ref, ms_ref, gfin_ref, w_ref, yp_ref, ys_ref, *, til):
    tp, rows = til.tp, til.tp + til.ts
    gfin = gfin_ref[...]
    out_all = _dot(a_ref[...], w_ref[...])
    for g in range(til.groups):
        gtp, gts = til.mods(mp_ref, ms_ref, 0, g)
        r0 = g * rows
        out = out_all[r0:r0 + rows]
        yp_ref[g * tp:(g + 1) * tp] = _rms(x1_ref[r0:r0 + tp] + gtp * out[:tp], gfin)
        ys_ref[:, til.position(g), :] = _rms(x1_ref[r0 + tp:r0 + rows] + gts * out[tp:], gfin)


def _ffn_out(act, x1, mod_p, mod_s, g_final, w_ffn_out, til, steps):
    return pl.pallas_call(
        functools.partial(_ffn_out_kernel, til=til),
        grid=(til.steps,),
        in_specs=[
            til.both(D_FF), til.both(D_MODEL),
            *til.mod_tables(FFN_OUT_MODS),
            _resident((1, D_MODEL)),
            _resident(w_ffn_out.shape),
        ],
        out_specs=[til.prompt(D_MODEL), til.sample_block(steps, D_MODEL)],
        out_shape=[
            jax.ShapeDtypeStruct((til.steps * til.groups * til.tp, D_MODEL), F32),
            jax.ShapeDtypeStruct((til.n_sample, steps, D_MODEL), F32),
        ],
        compiler_params=_params(),
        name="ffn_out",
    )(act, x1, mod_p, mod_s, g_final, w_ffn_out)


def _lane_expand(per_group):
    return jnp.repeat(per_group, SG_HEAD, axis=1)


def kernel(x_prompt, x_sample, state_conv, c_prompt, c_sample, g_mix, g_ffn, w_ada, b_ada, w_in, w_conv, g_v,
           w_sg, b_sg, w_pa, w_pb, w_out, w_ffn_in, w_ffn_out, g_final):
    assert g_mix.shape[0] == 1, "one layer"
    batch, seq, _ = x_prompt.shape
    n_sample, steps, _ = x_sample.shape
    assert batch <= SUBLANES and seq % CHUNK == 0
    row = lambda v: v.reshape(1, -1)

    a, mod_p, mod_s = _ada(c_prompt, c_sample, w_ada[0], row(b_ada[0]), ADA_SLOTS)

    xp = x_prompt.reshape(batch * seq, D_MODEL)
    xs = x_sample
    state = state_conv[0]
    til = _Tiling(batch * seq, steps * n_sample, seq, n_sample, steps, tp=256)

    bias_p = _lane_expand(b_sg[0][:, :CHUNK].T)
    w_sg_s = _lane_expand(w_sg[0][:, :steps, :steps].transpose(1, 2, 0).reshape(steps * steps, SG_GROUPS))
    w_sg_s = w_sg_s.reshape(steps * steps, 1, SG_DIM)
    bias_s = _lane_expand(b_sg[0][:, :steps].T).reshape(steps, 1, SG_DIM)
    h, ya, yb, xs_rows, zst_p, vst_p, z_s, vn_s, mod2_p, mod2_s, w_gate_b, w_pa_b, w_pb_b, w_out_b = _mix(
        xp, xs, mod_p, mod_s, row(g_mix[0]), w_in[0], w_conv[0], row(g_v[0]), w_sg[0], bias_p, w_sg_s, bias_s,
        state, a, w_ada[0], row(b_ada[0]), w_pa[0], w_pb[0], w_out[0], til, steps)

    x1, h2 = _merge(xp, xs_rows, h, ya, yb, mod2_p, mod2_s, row(g_ffn[0]), w_gate_b, w_pa_b, w_pb_b, w_out_b, til)
    act, w_down_b = _ffn_in(h2, w_ffn_in[0], w_ffn_out[0], tm=8 * (til.tp + til.ts), tn=512)
    y_p, y_s = _ffn_out(act, x1, mod2_p, mod2_s, row(g_final), w_down_b, til.regrouped(2), steps)

    y_prompt = y_p.reshape(batch, seq, D_MODEL)
    conv_prompt = zst_p[None]
    sgv_prompt = vst_p.reshape(1, batch, CHUNK, SG_GROUPS, SG_HEAD)
    return (y_prompt, y_s, conv_prompt, z_s[None], sgv_prompt, vn_s[None])
```

```python
import functools

import jax
import jax.numpy as jnp
from jax import lax
from jax.experimental import pallas as pl
from jax.experimental.pallas import tpu as pltpu

D_MODEL = 2048
CONV_DIM = D_MODEL // 2
CONV_WIDTH = 3
SG_DIM = D_MODEL // 2
SG_GROUPS = 8
SG_HEAD = SG_DIM // SG_GROUPS
CHUNK = 128
D_FF = 5632
EPS = 1e-6
SEG = 1024
MIX_COLS = 3 * CONV_DIM + 2 * SG_DIM
SUBLANES = 8
FFN_CHUNK = 256
STAGE_SLOTS = 4
VMEM_LIMIT_BYTES = 60 * 1024 * 1024

MOD_ORDER = (1, 0, 2, 4, 3, 5)
ADA_SLOTS, SIDE_SLOTS = (0, 2), (2, 4)
MIX_MODS, MERGE_MODS, FFN_OUT_MODS = (0, 2), (0, 3), (3, 1)
SIDE_TN = 256

F32 = jnp.float32
BF16 = jnp.bfloat16


def _dot(a, b):
    return jnp.dot(a, b, preferred_element_type=F32)


def _rms(x, gain):
    return x * lax.rsqrt(jnp.mean(x * x, axis=-1, keepdims=True) + EPS) * gain


def _modulated_norm(x, gain, scale, shift):
    return _rms(x, gain * (1 + scale)) + shift


def _rows(*parts):
    return jnp.concatenate(parts, axis=0)


def _load_bf16(jobs, stage_ref, sem_ref):
    n_slots, rows, n = stage_ref.shape
    chunks = [(src, col0, dst, r0) for src, col0, dst in jobs for r0 in range(0, dst.shape[0], rows)]
    assert all(dst.shape[1] == n and dst.shape[0] % rows == 0 for _, _, dst in jobs)

    def chunk_copy(idx):
        src, col0, _, r0 = chunks[idx]
        slot = idx % n_slots
        return pltpu.make_async_copy(src.at[pl.ds(r0, rows), pl.ds(col0, n)], stage_ref.at[slot], sem_ref.at[slot])

    ahead = n_slots - 1
    for idx in range(min(ahead, len(chunks))):
        chunk_copy(idx).start()
    for idx, (_, _, dst, r0) in enumerate(chunks):
        if idx + ahead < len(chunks):
            chunk_copy(idx + ahead).start()
        chunk_copy(idx).wait()
        dst[r0:r0 + rows, :] = stage_ref[idx % n_slots].astype(BF16)


def _ada_columns(first_slot, tn):
    per = D_MODEL // tn

    def index_map(j):
        comp = 0
        for slot, c in enumerate(MOD_ORDER):
            comp = jnp.where(first_slot + j // per == slot, c, comp)
        return (0, comp * per + j % per)

    return index_map


def _ada_table_specs(n_sample, tn):
    per = D_MODEL // tn
    return [
        pl.BlockSpec((None, SUBLANES, 1, tn), lambda j: (j // per, 0, 0, j % per)),
        pl.BlockSpec((None, n_sample, tn), lambda j: (j // per, 0, j % per)),
    ]


def _ada_block(a, w_ref, b_ref, mp_ref, ms_ref):
    r = _dot(a, w_ref[...].astype(BF16)) + b_ref[...]
    for n in range(SUBLANES):
        mp_ref[n] = r[n:n + 1]
    ms_ref[...] = r[SUBLANES:SUBLANES + ms_ref.shape[0]]


def _ada_kernel(cp_ref, cs_ref, w_ref, b_ref, a_ref, mp_ref, ms_ref, rows_ref):
    batch, n_sample = cp_ref.shape[0], cs_ref.shape[0]
    rows_ref[...] = jnp.zeros_like(rows_ref)
    rows_ref[0:batch] = jax.nn.silu(cp_ref[...])
    rows_ref[SUBLANES:SUBLANES + n_sample] = jax.nn.silu(cs_ref[...])
    a = rows_ref[...].astype(BF16)
    a_ref[...] = a
    _ada_block(a, w_ref, b_ref, mp_ref, ms_ref)


def _ada(c_prompt, c_sample, w_ada, b_ada, slots):
    tn = 1024
    first, count = slots
    batch, n_sample = c_prompt.shape[0], c_sample.shape[0]
    rows = -(-(SUBLANES + n_sample) // 16) * 16
    assert batch <= SUBLANES
    return pl.pallas_call(
        _ada_kernel,
        grid=(count * (D_MODEL // tn),),
        in_specs=[
            pl.BlockSpec((batch, D_MODEL), lambda j: (0, 0)),
            pl.BlockSpec((n_sample, D_MODEL), lambda j: (0, 0)),
            pl.BlockSpec((D_MODEL, tn), _ada_columns(first, tn)),
            pl.BlockSpec((1, tn), _ada_columns(first, tn)),
        ],
        out_specs=[pl.BlockSpec((rows, D_MODEL), lambda j: (0, 0)), *_ada_table_specs(n_sample, tn)],
        out_shape=[
            jax.ShapeDtypeStruct((rows, D_MODEL), BF16),
            jax.ShapeDtypeStruct((count, SUBLANES, 1, D_MODEL), F32),
            jax.ShapeDtypeStruct((count, n_sample, D_MODEL), F32),
        ],
        scratch_shapes=[pltpu.VMEM((rows, D_MODEL), F32)],
        compiler_params=pltpu.CompilerParams(
            dimension_semantics=("arbitrary",), vmem_limit_bytes=VMEM_LIMIT_BYTES),
        name="ada",
    )(c_prompt, c_sample, w_ada, b_ada)


class _Tiling:
    def __init__(self, rows_p, rows_s, seq, n_sample, positions, tp, groups=1):
        n_groups = rows_p // tp
        self.tp = tp
        self.groups = groups
        self.steps = n_groups // groups
        self.ts = rows_s // n_groups
        self.rows = rows_p + rows_s
        self.tiles_per_seq = seq // tp
        self.positions = positions
        self.n_sample = n_sample
        self._args = (rows_p, rows_s, seq, n_sample, positions, tp)
        assert rows_p % tp == 0 and seq % tp == 0 and rows_s % n_groups == 0
        assert self.ts % 16 == 0 and n_sample % self.ts == 0 and rows_s == n_sample * positions
        assert positions % groups == 0 and self.tiles_per_seq % groups == 0

    def regrouped(self, groups):
        return _Tiling(*self._args, groups=groups)

    def group(self, g=0):
        return pl.program_id(0) * self.groups + g

    def prompt(self, width):
        return pl.BlockSpec((self.groups * self.tp, width), lambda i: (i, 0))

    def sample_rows(self, width):
        return pl.BlockSpec((self.groups * self.ts, width), lambda i: (i, 0))

    def sample_block(self, *shape):
        return pl.BlockSpec((self.ts, *shape), lambda i: (i * self.groups // self.positions,) + (0,) * len(shape))

    def position(self, g=0):
        return self.group(g) % self.positions

    def both(self, width):
        return pl.BlockSpec((self.groups * (self.tp + self.ts), width), lambda i: (i, 0))

    def mod_tables(self, slots):
        first, count = slots
        assert first % count == 0
        return [
            pl.BlockSpec((count, SUBLANES, 1, D_MODEL), lambda i: (first // count, 0, 0, 0),
                         pipeline_mode=pl.Buffered(1)),
            pl.BlockSpec((count, self.n_sample, D_MODEL), lambda i: (first // count, 0, 0),
                         pipeline_mode=pl.Buffered(1)),
        ]

    def mods(self, mp_ref, ms_ref, k, g=0):
        s = self.group(g)
        start = pl.multiple_of((s // self.positions) * self.ts, self.ts)
        return mp_ref[k, s // self.tiles_per_seq], ms_ref[k, pl.ds(start, self.ts), :]


def _resident(shape):
    return pl.BlockSpec(shape, lambda i: (0,) * len(shape), pipeline_mode=pl.Buffered(1))


_HBM = pl.BlockSpec(memory_space=pl.ANY)


def _params():
    return pltpu.CompilerParams(dimension_semantics=("arbitrary",), vmem_limit_bytes=VMEM_LIMIT_BYTES)


def _mix_kernel(xp_ref, xs_ref, mp_ref, ms_ref, gmix_ref, w_hbm, wconv_ref, gv_ref,
                wsg_ref, bsgp_ref, wsgs_ref, bsgs_ref, state_ref,
                a_ref, wada_ref, bada_ref, wg0_ref, wg1_ref, wg2_ref, wg3_ref, wpa_ref, wpb_ref, wout_ref,
                h_ref, ya_ref, yb_ref, xsr_ref, zstp_ref, vstp_ref, zs_ref, vs_ref,
                mpr_ref, msr_ref, wgate_b_ref, wpa_b_ref, wpb_b_ref, wout_b_ref,
                w_ref, stage_ref, sem_ref, carry_ref, zhist_ref, vhist_ref, *, til, steps):
    assert CONV_WIDTH == 3, "the conv mixer below is written for three taps"
    i = pl.program_id(0)
    tp, ts = til.tp, til.ts
    t = til.position()

    def slot(ref, k):
        start = k * ts if isinstance(k, int) else pl.multiple_of(k * ts, ts)
        return ref[pl.ds(start, ts), :]

    @pl.when(i == 0)
    def _():
        _load_bf16([(w_hbm, 0, w_ref)], stage_ref, sem_ref)
        vhist_ref[...] = jnp.zeros_like(vhist_ref)

    @pl.when(i % til.tiles_per_seq == 0)
    def _():
        carry_ref[...] = jnp.zeros_like(carry_ref)

    gmix = gmix_ref[...]
    scp, scs = til.mods(mp_ref, ms_ref, 0)
    shp, shs = til.mods(mp_ref, ms_ref, 1)
    xs = xs_ref[:, t, :]
    xsr_ref[...] = xs
    h = _rows(_modulated_norm(xp_ref[...], gmix, scp, shp), _modulated_norm(xs, gmix, scs, shs)).astype(BF16)
    h_ref[...] = h

    for k, wg_ref in enumerate((wg0_ref, wg1_ref, wg2_ref, wg3_ref)):
        wgate_b_ref[:, k * SEG:(k + 1) * SEG] = wg_ref[...].astype(BF16)
    wpa_b_ref[...] = wpa_ref[...].astype(BF16)
    wpb_b_ref[...] = wpb_ref[...].astype(BF16)
    wout_b_ref[...] = wout_ref[...].astype(BF16)
    proj = lambda k: _dot(h_ref[...], w_ref[:, k * SEG:(k + 1) * SEG])

    vn = _rms(jax.nn.gelu(proj(4)), gv_ref[...])
    z_all = proj(1) * proj(2)
    vstp_ref[...] = vn[tp - CHUNK:tp]
    vb = vn[:tp].astype(BF16)
    causal = (lax.broadcasted_iota(jnp.int32, (CHUNK, CHUNK), 0)
              >= lax.broadcasted_iota(jnp.int32, (CHUNK, CHUNK), 1))
    wgs = [jnp.where(causal, wsg_ref[g], 0.0).astype(BF16) for g in range(SG_GROUPS)]
    bias = bsgp_ref[...]
    chunks = []
    for c in range(tp // CHUNK):
        rows = slice(c * CHUNK, (c + 1) * CHUNK)
        parts = [_dot(wgs[g], vb[rows, g * SG_HEAD:(g + 1) * SG_HEAD]) for g in range(SG_GROUPS)]
        chunks.append(jnp.concatenate(parts, axis=1) + bias)
    gu = jax.nn.gelu(proj(3))
    b_gate = proj(0)
    yb_ref[:tp] = (gu[:tp] * _rows(*chunks)).astype(BF16)

    vs = vn[tp:]
    for g in range(SG_GROUPS):
        vs_ref[:, t, g, :] = vs[:, g * SG_HEAD:(g + 1) * SG_HEAD]
    vhist_ref[pl.ds(pl.multiple_of(t * ts, ts), ts), :] = vs
    sp = bsgs_ref[t]
    for s in range(steps):
        w_ts = jnp.where(s <= t, wsgs_ref[t * steps + s], 0.0)
        sp = sp + w_ts * slot(vhist_ref, s)
    yb_ref[tp:] = (gu[tp:] * sp).astype(BF16)

    wc = wconv_ref[...]
    z = z_all[:tp]
    carry = carry_ref[...]
    prev2, prev1 = carry[SUBLANES - 2:SUBLANES - 1], carry[SUBLANES - 1:SUBLANES]
    row = lax.broadcasted_iota(jnp.int32, (SUBLANES, CONV_DIM), 0)
    z1 = pltpu.roll(z, 1, 0)
    z2 = pltpu.roll(z, 2, 0)
    z1 = _rows(jnp.where(row == 0, prev1, z1[:SUBLANES]), z1[SUBLANES:])
    z2 = _rows(jnp.where(row == 0, prev2, jnp.where(row == 1, prev1, z2[:SUBLANES])), z2[SUBLANES:])
    conv = wc[0:1] * z2 + wc[1:2] * z1 + wc[2:3] * z
    ya_ref[:tp] = (b_gate[:tp] * conv).astype(BF16)
    carry_ref[...] = z[tp - SUBLANES:]
    zstp_ref[...] = z[tp - (CONV_WIDTH - 1):]

    zs = z_all[tp:]
    for k in range(CONV_WIDTH - 1):
        zhist_ref[k * ts:(k + 1) * ts, :] = state_ref[:, k, :]
    zhist_ref[pl.ds(pl.multiple_of((t + CONV_WIDTH - 1) * ts, ts), ts), :] = zs
    conv_s = wc[0:1] * slot(zhist_ref, t) + wc[1:2] * slot(zhist_ref, t + 1) + wc[2:3] * zs
    ya_ref[tp:] = (b_gate[tp:] * conv_s).astype(BF16)
    zs_ref[:, jnp.maximum(t - (steps - (CONV_WIDTH - 1)), 0), :] = zs

    _ada_block(a_ref[...], wada_ref, bada_ref, mpr_ref, msr_ref)


def _row_slab(rows_total, n_steps, width):
    assert rows_total % (16 * n_steps) == 0
    return pl.BlockSpec((rows_total // n_steps, width), lambda i: (i, 0))


def _mix(xp, xs, mod_p, mod_s, g_mix, w_in, w_conv, g_v, w_sg, bias_p, w_sg_s, bias_s, state,
         a, w_ada, b_ada, w_pa, w_pb, w_out, til, steps):
    n_sample = til.n_sample
    n_seq = til.steps // til.tiles_per_seq
    per_seq = lambda rows, width: pl.BlockSpec((None, rows, width), lambda i: (i // til.tiles_per_seq, 0, 0))
    stage_rows = 64
    assert steps >= CONV_WIDTH - 1
    n_side = SIDE_SLOTS[1]
    assert til.steps * SIDE_TN == n_side * D_MODEL, "one modulation-table column block per grid step"
    slab = lambda k, width: _row_slab(k, til.steps, width)
    return pl.pallas_call(
        functools.partial(_mix_kernel, til=til, steps=steps),
        grid=(til.steps,),
        in_specs=[
            til.prompt(D_MODEL), til.sample_block(steps, D_MODEL), *til.mod_tables(MIX_MODS),
            _resident((1, D_MODEL)),
            _HBM,
            _resident((CONV_WIDTH, CONV_DIM)),
            _resident((1, SG_DIM)),
            _resident((SG_GROUPS, CHUNK, CHUNK)),
            _resident((CHUNK, SG_DIM)),
            _resident((steps * steps, 1, SG_DIM)),
            _resident((steps, 1, SG_DIM)),
            til.sample_block(CONV_WIDTH - 1, CONV_DIM),
            _resident(a.shape),
            pl.BlockSpec((D_MODEL, SIDE_TN), _ada_columns(SIDE_SLOTS[0], SIDE_TN)),
            pl.BlockSpec((1, SIDE_TN), _ada_columns(SIDE_SLOTS[0], SIDE_TN)),
            *[pl.BlockSpec((D_MODEL // til.steps, SEG), lambda i, k=k: (i, MIX_COLS // SEG + k))
              for k in range(2 * D_MODEL // SEG)],
            slab(CONV_DIM, D_MODEL), slab(SG_DIM, D_MODEL), slab(D_MODEL, D_MODEL),
        ],
        out_specs=[
            til.both(D_MODEL), til.both(CONV_DIM), til.both(SG_DIM), til.sample_rows(D_MODEL),
            per_seq(CONV_WIDTH - 1, CONV_DIM), per_seq(CHUNK, SG_DIM),
            til.sample_block(CONV_WIDTH - 1, CONV_DIM), til.sample_block(steps, SG_GROUPS, SG_HEAD),
            *_ada_table_specs(n_sample, SIDE_TN),
            slab(D_MODEL, 2 * D_MODEL), slab(CONV_DIM, D_MODEL), slab(SG_DIM, D_MODEL), slab(D_MODEL, D_MODEL),
        ],
        out_shape=[
            jax.ShapeDtypeStruct((til.rows, D_MODEL), BF16),
            jax.ShapeDtypeStruct((til.rows, CONV_DIM), BF16),
            jax.ShapeDtypeStruct((til.rows, SG_DIM), BF16),
            jax.ShapeDtypeStruct((til.steps * til.ts, D_MODEL), F32),
            jax.ShapeDtypeStruct((n_seq, CONV_WIDTH - 1, CONV_DIM), F32),
            jax.ShapeDtypeStruct((n_seq, CHUNK, SG_DIM), F32),
            jax.ShapeDtypeStruct((n_sample, CONV_WIDTH - 1, CONV_DIM), F32),
            jax.ShapeDtypeStruct((n_sample, steps, SG_GROUPS, SG_HEAD), F32),
            jax.ShapeDtypeStruct((n_side, SUBLANES, 1, D_MODEL), F32),
            jax.ShapeDtypeStruct((n_side, n_sample, D_MODEL), F32),
            jax.ShapeDtypeStruct((D_MODEL, 2 * D_MODEL), BF16),
            jax.ShapeDtypeStruct((CONV_DIM, D_MODEL), BF16),
            jax.ShapeDtypeStruct((SG_DIM, D_MODEL), BF16),
            jax.ShapeDtypeStruct((D_MODEL, D_MODEL), BF16),
        ],
        scratch_shapes=[
            pltpu.VMEM((D_MODEL, MIX_COLS), BF16),
            pltpu.VMEM((STAGE_SLOTS, stage_rows, MIX_COLS), F32),
            pltpu.SemaphoreType.DMA((STAGE_SLOTS,)),
            pltpu.VMEM((SUBLANES, CONV_DIM), F32),
            pltpu.VMEM(((CONV_WIDTH - 1 + steps) * til.ts, CONV_DIM), F32),
            pltpu.VMEM((steps * til.ts, SG_DIM), F32),
        ],
        compiler_params=_params(),
        name="mix",
    )(xp, xs, mod_p, mod_s, g_mix, w_in, w_conv, g_v, w_sg, bias_p, w_sg_s, bias_s, state,
      a, w_ada, b_ada, w_in, w_in, w_in, w_in, w_pa, w_pb, w_out)


def _merge_kernel(h_ref, ya_ref, yb_ref, wgate_ref, wpa_ref, wpb_ref, merged_ref):
    h = h_ref[...]
    ya = ya_ref[...]
    yb = yb_ref[...]
    for k in range(D_MODEL // SEG):
        cols = slice(k * SEG, (k + 1) * SEG)
        gate_b_cols = slice(D_MODEL + k * SEG, D_MODEL + (k + 1) * SEG)
        gate_a = jax.nn.sigmoid(_dot(h, wgate_ref[:, cols]))
        gate_b = jax.nn.sigmoid(_dot(h, wgate_ref[:, gate_b_cols]))
        merged_ref[:, cols] = (gate_a * _dot(ya, wpa_ref[:, cols]) + gate_b * _dot(yb, wpb_ref[:, cols])).astype(BF16)


def _merge(h, ya, yb, w_gate, w_pa, w_pb, til):
    return pl.pallas_call(
        _merge_kernel,
        grid=(til.steps,),
        in_specs=[
            til.both(D_MODEL), til.both(CONV_DIM), til.both(SG_DIM),
            _resident(w_gate.shape), _resident(w_pa.shape), _resident(w_pb.shape),
        ],
        out_specs=til.both(D_MODEL),
        out_shape=jax.ShapeDtypeStruct((til.rows, D_MODEL), BF16),
        compiler_params=_params(),
        name="merge",
    )(h, ya, yb, w_gate, w_pa, w_pb)


def _proj_out_kernel(xp_ref, xs_ref, merged_ref, mp_ref, ms_ref, gffn_ref, wout_ref, x1_ref, h2_ref, *, til):
    tp, ts, rows = til.tp, til.ts, til.tp + til.ts
    gffn = gffn_ref[...]
    out_all = _dot(merged_ref[...], wout_ref[...])

    def finish(where, x, out, gate, scale, shift):
        x1 = x + gate * out
        x1_ref[where] = x1
        h2_ref[where] = _modulated_norm(x1, gffn, scale, shift).astype(BF16)

    for g in range(til.groups):
        gtp, gts = til.mods(mp_ref, ms_ref, 0, g)
        scp, scs = til.mods(mp_ref, ms_ref, 1, g)
        shp, shs = til.mods(mp_ref, ms_ref, 2, g)
        r0 = g * rows
        finish(slice(r0, r0 + tp), xp_ref[g * tp:(g + 1) * tp], out_all[r0:r0 + tp], gtp, scp, shp)
        finish(slice(r0 + tp, r0 + rows), xs_ref[g * ts:(g + 1) * ts], out_all[r0 + tp:r0 + rows], gts, scs, shs)


def _proj_out(xp, xs, merged, mod_p, mod_s, g_ffn, w_out, til):
    return pl.pallas_call(
        functools.partial(_proj_out_kernel, til=til),
        grid=(til.steps,),
        in_specs=[
            til.prompt(D_MODEL), til.sample_rows(D_MODEL), til.both(D_MODEL),
            *til.mod_tables(MERGE_MODS),
            _resident((1, D_MODEL)),
            _resident(w_out.shape),
        ],
        out_specs=[til.both(D_MODEL), til.both(D_MODEL)],
        out_shape=[
            jax.ShapeDtypeStruct((til.rows, D_MODEL), F32),
            jax.ShapeDtypeStruct((til.rows, D_MODEL), BF16),
        ],
        compiler_params=_params(),
        name="proj_out",
    )(xp, xs, merged, mod_p, mod_s, g_ffn, w_out)


def _ffn_in_kernel(h_ref, wg_ref, wu_ref, wdown_ref, o_ref, wdown_b_ref):
    h = h_ref[...]
    for c in range(0, wg_ref.shape[1], FFN_CHUNK):
        cols = slice(c, c + FFN_CHUNK)
        act = jax.nn.silu(_dot(h, wg_ref[:, cols].astype(BF16))) * _dot(h, wu_ref[:, cols].astype(BF16))
        o_ref[:, cols] = act.astype(BF16)
    wdown_b_ref[...] = wdown_ref[...].astype(BF16)


def _ffn_in(h2, w_ffn_in, w_ffn_out, tm, tn):
    rows = h2.shape[0]
    n_blocks = D_FF // tn
    m_steps = rows // tm
    slab_rows = D_FF // (n_blocks * m_steps)
    assert rows % tm == 0 and D_FF % tn == 0 and D_FF % (16 * n_blocks * m_steps) == 0
    slab = pl.BlockSpec((slab_rows, D_MODEL), lambda j, i: (j * m_steps + i, 0))
    return pl.pallas_call(
        _ffn_in_kernel,
        grid=(n_blocks, m_steps),
        in_specs=[
            pl.BlockSpec((tm, D_MODEL), lambda j, i: (i, 0)),
            pl.BlockSpec((D_MODEL, tn), lambda j, i: (0, j)),
            pl.BlockSpec((D_MODEL, tn), lambda j, i: (0, j + n_blocks)),
            slab,
        ],
        out_specs=[pl.BlockSpec((tm, tn), lambda j, i: (i, j)), slab],
        out_shape=[
            jax.ShapeDtypeStruct((rows, D_FF), BF16),
            jax.ShapeDtypeStruct((D_FF, D_MODEL), BF16),
        ],
        compiler_params=pltpu.CompilerParams(
            dimension_semantics=("arbitrary", "arbitrary"), vmem_limit_bytes=VMEM_LIMIT_BYTES),
        name="ffn_in",
    )(h2, w_ffn_in, w_ffn_in, w_ffn_out)


def _ffn_out_kernel(a_ref, x1_ref, mp_ref, ms_ref, gfin_ref, w_ref, yp_ref, ys_ref, *, til):
    tp, rows = til.tp, til.tp + til.ts
    gfin = gfin_ref[...]
    out_all = _dot(a_ref[...], w_ref[...])
    for g in range(til.groups):
        gtp, gts = til.mods(mp_ref, ms_ref, 0, g)
        r0 = g * rows
        out = out_all[r0:r0 + rows]
        yp_ref[g * tp:(g + 1) * tp] = _rms(x1_ref[r0:r0 + tp] + gtp * out[:tp], gfin)
        ys_ref[:, til.position(g), :] = _rms(x1_ref[r0 + tp:r0 + rows] + gts * out[tp:], gfin)


def _ffn_out(act, x1, mod_p, mod_s, g_final, w_ffn_out, til, steps):
    return pl.pallas_call(
        functools.partial(_ffn_out_kernel, til=til),
        grid=(til.steps,),
        in_specs=[
            til.both(D_FF), til.both(D_MODEL),
            *til.mod_tables(FFN_OUT_MODS),
            _resident((1, D_MODEL)),
            _resident(w_ffn_out.shape),
        ],
        out_specs=[til.prompt(D_MODEL), til.sample_block(steps, D_MODEL)],
        out_shape=[
            jax.ShapeDtypeStruct((til.steps * til.groups * til.tp, D_MODEL), F32),
            jax.ShapeDtypeStruct((til.n_sample, steps, D_MODEL), F32),
        ],
        compiler_params=_params(),
        name="ffn_out",
    )(act, x1, mod_p, mod_s, g_final, w_ffn_out)


def _lane_expand(per_group):
    return jnp.repeat(per_group, SG_HEAD, axis=1)


def kernel(x_prompt, x_sample, state_conv, c_prompt, c_sample, g_mix, g_ffn, w_ada, b_ada, w_in, w_conv, g_v,
           w_sg, b_sg, w_pa, w_pb, w_out, w_ffn_in, w_ffn_out, g_final):
    assert g_mix.shape[0] == 1, "one layer"
    batch, seq, _ = x_prompt.shape
    n_sample, steps, _ = x_sample.shape
    assert batch <= SUBLANES and seq % CHUNK == 0
    row = lambda v: v.reshape(1, -1)

    a, mod_p, mod_s = _ada(c_prompt, c_sample, w_ada[0], row(b_ada[0]), ADA_SLOTS)

    xp = x_prompt.reshape(batch * seq, D_MODEL)
    xs = x_sample
    state = state_conv[0]
    til = _Tiling(batch * seq, steps * n_sample, seq, n_sample, steps, tp=256)

    bias_p = _lane_expand(b_sg[0][:, :CHUNK].T)
    w_sg_s = _lane_expand(w_sg[0][:, :steps, :steps].transpose(1, 2, 0).reshape(steps * steps, SG_GROUPS))
    w_sg_s = w_sg_s.reshape(steps * steps, 1, SG_DIM)
    bias_s = _lane_expand(b_sg[0][:, :steps].T).reshape(steps, 1, SG_DIM)
    h, ya, yb, xs_rows, zst_p, vst_p, z_s, vn_s, mod2_p, mod2_s, w_gate_b, w_pa_b, w_pb_b, w_out_b = _mix(
        xp, xs, mod_p, mod_s, row(g_mix[0]), w_in[0], w_conv[0], row(g_v[0]), w_sg[0], bias_p, w_sg_s, bias_s,
        state, a, w_ada[0], row(b_ada[0]), w_pa[0], w_pb[0], w_out[0], til, steps)

    merged = _merge(h, ya, yb, w_gate_b, w_pa_b, w_pb_b, til.regrouped(2))
    x1, h2 = _proj_out(xp, xs_rows, merged, mod2_p, mod2_s, row(g_ffn[0]), w_out_b, til.regrouped(2))
    act, w_down_b = _ffn_in(h2, w_ffn_in[0], w_ffn_out[0], tm=8 * (til.tp + til.ts), tn=512)
    y_p, y_s = _ffn_out(act, x1, mod2_p, mod2_s, row(g_final), w_down_b, til.regrouped(2), steps)

    y_prompt = y_p.reshape(batch, seq, D_MODEL)
    conv_prompt = zst_p[None]
    sgv_prompt = vst_p.reshape(1, batch, CHUNK, SG_GROUPS, SG_HEAD)
    return (y_prompt, y_s, conv_prompt, z_s[None], sgv_prompt, vn_s[None])
```

```python
import functools

import jax
import jax.numpy as jnp
from jax import lax
from jax.experimental import pallas as pl
from jax.experimental.pallas import tpu as pltpu

D_MODEL = 2048
CONV_DIM = D_MODEL // 2
CONV_WIDTH = 3
SG_DIM = D_MODEL // 2
SG_GROUPS = 8
SG_HEAD = SG_DIM // SG_GROUPS
CHUNK = 128
D_FF = 5632
EPS = 1e-6
SEG = 1024
MIX_COLS = 3 * CONV_DIM + 2 * SG_DIM
SUBLANES = 8
FFN_CHUNK = 256
STAGE_SLOTS = 4
VMEM_LIMIT_BYTES = 60 * 1024 * 1024

MOD_ORDER = (1, 0, 2, 4, 3, 5)
ADA_SLOTS, SIDE_SLOTS = (0, 2), (2, 4)
MIX_MODS, MERGE_MODS, FFN_OUT_MODS = (0, 2), (0, 3), (3, 1)
SIDE_TN = 256

F32 = jnp.float32
BF16 = jnp.bfloat16


def _dot(a, b):
    return jnp.dot(a, b, preferred_element_type=F32)


def _rms(x, gain):
    return x * lax.rsqrt(jnp.mean(x * x, axis=-1, keepdims=True) + EPS) * gain


def _modulated_norm(x, gain, scale, shift):
    return _rms(x, gain * (1 + scale)) + shift


def _rows(*parts):
    return jnp.concatenate(parts, axis=0)


def _load_bf16(jobs, stage_ref, sem_ref):
    n_slots, rows, n = stage_ref.shape
    chunks = [(src, col0, dst, r0) for src, col0, dst in jobs for r0 in range(0, dst.shape[0], rows)]
    assert all(dst.shape[1] == n and dst.shape[0] % rows == 0 for _, _, dst in jobs)

    def chunk_copy(idx):
        src, col0, _, r0 = chunks[idx]
        slot = idx % n_slots
        return pltpu.make_async_copy(src.at[pl.ds(r0, rows), pl.ds(col0, n)], stage_ref.at[slot], sem_ref.at[slot])

    ahead = n_slots - 1
    for idx in range(min(ahead, len(chunks))):
        chunk_copy(idx).start()
    for idx, (_, _, dst, r0) in enumerate(chunks):
        if idx + ahead < len(chunks):
            chunk_copy(idx + ahead).start()
        chunk_copy(idx).wait()
        dst[r0:r0 + rows, :] = stage_ref[idx % n_slots].astype(BF16)


def _ada_columns(first_slot, tn):
    per = D_MODEL // tn

    def index_map(j):
        comp = 0
        for slot, c in enumerate(MOD_ORDER):
            comp = jnp.where(first_slot + j // per == slot, c, comp)
        return (0, comp * per + j % per)

    return index_map


def _ada_table_specs(n_sample, tn):
    per = D_MODEL // tn
    return [
        pl.BlockSpec((None, SUBLANES, 1, tn), lambda j: (j // per, 0, 0, j % per)),
        pl.BlockSpec((None, n_sample, tn), lambda j: (j // per, 0, j % per)),
    ]


def _ada_block(a, w_ref, b_ref, mp_ref, ms_ref):
    r = _dot(a, w_ref[...].astype(BF16)) + b_ref[...]
    for n in range(SUBLANES):
        mp_ref[n] = r[n:n + 1]
    ms_ref[...] = r[SUBLANES:SUBLANES + ms_ref.shape[0]]


def _ada_kernel(cp_ref, cs_ref, w_ref, b_ref, a_ref, mp_ref, ms_ref, rows_ref):
    batch, n_sample = cp_ref.shape[0], cs_ref.shape[0]
    rows_ref[...] = jnp.zeros_like(rows_ref)
    rows_ref[0:batch] = jax.nn.silu(cp_ref[...])
    rows_ref[SUBLANES:SUBLANES + n_sample] = jax.nn.silu(cs_ref[...])
    a = rows_ref[...].astype(BF16)
    a_ref[...] = a
    _ada_block(a, w_ref, b_ref, mp_ref, ms_ref)


def _ada(c_prompt, c_sample, w_ada, b_ada, slots):
    tn = 1024
    first, count = slots
    batch, n_sample = c_prompt.shape[0], c_sample.shape[0]
    rows = -(-(SUBLANES + n_sample) // 16) * 16
    assert batch <= SUBLANES
    return pl.pallas_call(
        _ada_kernel,
        grid=(count * (D_MODEL // tn),),
        in_specs=[
            pl.BlockSpec((batch, D_MODEL), lambda j: (0, 0)),
            pl.BlockSpec((n_sample, D_MODEL), lambda j: (0, 0)),
            pl.BlockSpec((D_MODEL, tn), _ada_columns(first, tn)),
            pl.BlockSpec((1, tn), _ada_columns(first, tn)),
        ],
        out_specs=[pl.BlockSpec((rows, D_MODEL), lambda j: (0, 0)), *_ada_table_specs(n_sample, tn)],
        out_shape=[
            jax.ShapeDtypeStruct((rows, D_MODEL), BF16),
            jax.ShapeDtypeStruct((count, SUBLANES, 1, D_MODEL), F32),
            jax.ShapeDtypeStruct((count, n_sample, D_MODEL), F32),
        ],
        scratch_shapes=[pltpu.VMEM((rows, D_MODEL), F32)],
        compiler_params=pltpu.CompilerParams(
            dimension_semantics=("arbitrary",), vmem_limit_bytes=VMEM_LIMIT_BYTES),
        name="ada",
    )(c_prompt, c_sample, w_ada, b_ada)


class _Tiling:
    def __init__(self, rows_p, rows_s, seq, n_sample, positions, tp, groups=1):
        n_groups = rows_p // tp
        self.tp = tp
        self.groups = groups
        self.steps = n_groups // groups
        self.ts = rows_s // n_groups
        self.rows = rows_p + rows_s
        self.tiles_per_seq = seq // tp
        self.positions = positions
        self.n_sample = n_sample
        self._args = (rows_p, rows_s, seq, n_sample, positions, tp)
        assert rows_p % tp == 0 and seq % tp == 0 and rows_s % n_groups == 0
        assert self.ts % 16 == 0 and n_sample % self.ts == 0 and rows_s == n_sample * positions
        assert positions % groups == 0 and self.tiles_per_seq % groups == 0

    def regrouped(self, groups):
        return _Tiling(*self._args, groups=groups)

    def group(self, g=0):
        return pl.program_id(0) * self.groups + g

    def prompt(self, width):
        return pl.BlockSpec((self.groups * self.tp, width), lambda i: (i, 0))

    def sample_rows(self, width):
        return pl.BlockSpec((self.groups * self.ts, width), lambda i: (i, 0))

    def sample_block(self, *shape):
        return pl.BlockSpec((self.ts, *shape), lambda i: (i * self.groups // self.positions,) + (0,) * len(shape))

    def position(self, g=0):
        return self.group(g) % self.positions

    def both(self, width):
        return pl.BlockSpec((self.groups * (self.tp + self.ts), width), lambda i: (i, 0))

    def mod_tables(self, slots):
        first, count = slots
        assert first % count == 0
        return [
            pl.BlockSpec((count, SUBLANES, 1, D_MODEL), lambda i: (first // count, 0, 0, 0),
                         pipeline_mode=pl.Buffered(1)),
            pl.BlockSpec((count, self.n_sample, D_MODEL), lambda i: (first // count, 0, 0),
                         pipeline_mode=pl.Buffered(1)),
        ]

    def mods(self, mp_ref, ms_ref, k, g=0):
        s = self.group(g)
        start = pl.multiple_of((s // self.positions) * self.ts, self.ts)
        return mp_ref[k, s // self.tiles_per_seq], ms_ref[k, pl.ds(start, self.ts), :]


def _resident(shape):
    return pl.BlockSpec(shape, lambda i: (0,) * len(shape), pipeline_mode=pl.Buffered(1))


_HBM = pl.BlockSpec(memory_space=pl.ANY)


def _params():
    return pltpu.CompilerParams(dimension_semantics=("arbitrary",), vmem_limit_bytes=VMEM_LIMIT_BYTES)


def _mix_kernel(xp_ref, xs_ref, mp_ref, ms_ref, gmix_ref, w_hbm, wconv_ref, gv_ref,
                wsg_ref, bsgp_ref, wsgs_ref, bsgs_ref, state_ref,
                a_ref, wada_ref, bada_ref, wg0_ref, wg1_ref, wg2_ref, wg3_ref, wpa_ref, wpb_ref, wout_ref,
                h_ref, ya_ref, yb_ref, xsr_ref, zstp_ref, vstp_ref, zs_ref, vs_ref,
                mpr_ref, msr_ref, wgate_b_ref, wpa_b_ref, wpb_b_ref, wout_b_ref,
                w_ref, stage_ref, sem_ref, carry_ref, zhist_ref, vhist_ref, *, til, steps):
    assert CONV_WIDTH == 3, "the conv mixer below is written for three taps"
    i = pl.program_id(0)
    tp, ts = til.tp, til.ts
    t = til.position()

    def slot(ref, k):
        start = k * ts if isinstance(k, int) else pl.multiple_of(k * ts, ts)
        return ref[pl.ds(start, ts), :]

    @pl.when(i == 0)
    def _():
        _load_bf16([(w_hbm, 0, w_ref)], stage_ref, sem_ref)
        vhist_ref[...] = jnp.zeros_like(vhist_ref)

    @pl.when(i % til.tiles_per_seq == 0)
    def _():
        carry_ref[...] = jnp.zeros_like(carry_ref)

    gmix = gmix_ref[...]
    scp, scs = til.mods(mp_ref, ms_ref, 0)
    shp, shs = til.mods(mp_ref, ms_ref, 1)
    xs = xs_ref[:, t, :]
    xsr_ref[...] = xs
    h = _rows(_modulated_norm(xp_ref[...], gmix, scp, shp), _modulated_norm(xs, gmix, scs, shs)).astype(BF16)
    h_ref[...] = h

    for k, wg_ref in enumerate((wg0_ref, wg1_ref, wg2_ref, wg3_ref)):
        wgate_b_ref[:, k * SEG:(k + 1) * SEG] = wg_ref[...].astype(BF16)
    wpa_b_ref[...] = wpa_ref[...].astype(BF16)
    wpb_b_ref[...] = wpb_ref[...].astype(BF16)
    wout_b_ref[...] = wout_ref[...].astype(BF16)
    proj = lambda k: _dot(h_ref[...], w_ref[:, k * SEG:(k + 1) * SEG])

    vn = _rms(jax.nn.gelu(proj(4)), gv_ref[...])
    z_all = proj(1) * proj(2)
    vstp_ref[...] = vn[tp - CHUNK:tp]
    vb = vn[:tp].astype(BF16)
    causal = (lax.broadcasted_iota(jnp.int32, (CHUNK, CHUNK), 0)
              >= lax.broadcasted_iota(jnp.int32, (CHUNK, CHUNK), 1))
    wgs = [jnp.where(causal, wsg_ref[g], 0.0).astype(BF16) for g in range(SG_GROUPS)]
    bias = bsgp_ref[...]
    chunks = []
    for c in range(tp // CHUNK):
        rows = slice(c * CHUNK, (c + 1) * CHUNK)
        parts = [_dot(wgs[g], vb[rows, g * SG_HEAD:(g + 1) * SG_HEAD]) for g in range(SG_GROUPS)]
        chunks.append(jnp.concatenate(parts, axis=1) + bias)
    gu = jax.nn.gelu(proj(3))
    b_gate = proj(0)
    yb_ref[:tp] = (gu[:tp] * _rows(*chunks)).astype(BF16)

    vs = vn[tp:]
    for g in range(SG_GROUPS):
        vs_ref[:, t, g, :] = vs[:, g * SG_HEAD:(g + 1) * SG_HEAD]
    vhist_ref[pl.ds(pl.multiple_of(t * ts, ts), ts), :] = vs
    sp = bsgs_ref[t]
    for s in range(steps):
        w_ts = jnp.where(s <= t, wsgs_ref[t * steps + s], 0.0)
        sp = sp + w_ts * slot(vhist_ref, s)
    yb_ref[tp:] = (gu[tp:] * sp).astype(BF16)

    wc = wconv_ref[...]
    z = z_all[:tp]
    carry = carry_ref[...]
    prev2, prev1 = carry[SUBLANES - 2:SUBLANES - 1], carry[SUBLANES - 1:SUBLANES]
    row = lax.broadcasted_iota(jnp.int32, (SUBLANES, CONV_DIM), 0)
    z1 = pltpu.roll(z, 1, 0)
    z2 = pltpu.roll(z, 2, 0)
    z1 = _rows(jnp.where(row == 0, prev1, z1[:SUBLANES]), z1[SUBLANES:])
    z2 = _rows(jnp.where(row == 0, prev2, jnp.where(row == 1, prev1, z2[:SUBLANES])), z2[SUBLANES:])
    conv = wc[0:1] * z2 + wc[1:2] * z1 + wc[2:3] * z
    ya_ref[:tp] = (b_gate[:tp] * conv).astype(BF16)
    carry_ref[...] = z[tp - SUBLANES:]
    zstp_ref[...] = z[tp - (CONV_WIDTH - 1):]

    zs = z_all[tp:]
    for k in range(CONV_WIDTH - 1):
        zhist_ref[k * ts:(k + 1) * ts, :] = state_ref[:, k, :]
    zhist_ref[pl.ds(pl.multiple_of((t + CONV_WIDTH - 1) * ts, ts), ts), :] = zs
    conv_s = wc[0:1] * slot(zhist_ref, t) + wc[1:2] * slot(zhist_ref, t + 1) + wc[2:3] * zs
    ya_ref[tp:] = (b_gate[tp:] * conv_s).astype(BF16)
    zs_ref[:, jnp.maximum(t - (steps - (CONV_WIDTH - 1)), 0), :] = zs

    _ada_block(a_ref[...], wada_ref, bada_ref, mpr_ref, msr_ref)


def _row_slab(rows_total, n_steps, width):
    assert rows_total % (16 * n_steps) == 0
    return pl.BlockSpec((rows_total // n_steps, width), lambda i: (i, 0))


def _mix(xp, xs, mod_p, mod_s, g_mix, w_in, w_conv, g_v, w_sg, bias_p, w_sg_s, bias_s, state,
         a, w_ada, b_ada, w_pa, w_pb, w_out, til, steps):
    n_sample = til.n_sample
    n_seq = til.steps // til.tiles_per_seq
    per_seq = lambda rows, width: pl.BlockSpec((None, rows, width), lambda i: (i // til.tiles_per_seq, 0, 0))
    stage_rows = 64
    assert steps >= CONV_WIDTH - 1
    n_side = SIDE_SLOTS[1]
    assert til.steps * SIDE_TN == n_side * D_MODEL, "one modulation-table column block per grid step"
    slab = lambda k, width: _row_slab(k, til.steps, width)
    return pl.pallas_call(
        functools.partial(_mix_kernel, til=til, steps=steps),
        grid=(til.steps,),
        in_specs=[
            til.prompt(D_MODEL), til.sample_block(steps, D_MODEL), *til.mod_tables(MIX_MODS),
            _resident((1, D_MODEL)),
            _HBM,
            _resident((CONV_WIDTH, CONV_DIM)),
            _resident((1, SG_DIM)),
            _resident((SG_GROUPS, CHUNK, CHUNK)),
            _resident((CHUNK, SG_DIM)),
            _resident((steps * steps, 1, SG_DIM)),
            _resident((steps, 1, SG_DIM)),
            til.sample_block(CONV_WIDTH - 1, CONV_DIM),
            _resident(a.shape),
            pl.BlockSpec((D_MODEL, SIDE_TN), _ada_columns(SIDE_SLOTS[0], SIDE_TN)),
            pl.BlockSpec((1, SIDE_TN), _ada_columns(SIDE_SLOTS[0], SIDE_TN)),
            *[pl.BlockSpec((D_MODEL // til.steps, SEG), lambda i, k=k: (i, MIX_COLS // SEG + k))
              for k in range(2 * D_MODEL // SEG)],
            slab(CONV_DIM, D_MODEL), slab(SG_DIM, D_MODEL), slab(D_MODEL, D_MODEL),
        ],
        out_specs=[
            til.both(D_MODEL), til.both(CONV_DIM), til.both(SG_DIM), til.sample_rows(D_MODEL),
            per_seq(CONV_WIDTH - 1, CONV_DIM), per_seq(CHUNK, SG_DIM),
            til.sample_block(CONV_WIDTH - 1, CONV_DIM), til.sample_block(steps, SG_GROUPS, SG_HEAD),
            *_ada_table_specs(n_sample, SIDE_TN),
            slab(D_MODEL, 2 * D_MODEL), slab(CONV_DIM, D_MODEL), slab(SG_DIM, D_MODEL), slab(D_MODEL, D_MODEL),
        ],
        out_shape=[
            jax.ShapeDtypeStruct((til.rows, D_MODEL), BF16),
            jax.ShapeDtypeStruct((til.rows, CONV_DIM), BF16),
            jax.ShapeDtypeStruct((til.rows, SG_DIM), BF16),
            jax.ShapeDtypeStruct((til.steps * til.ts, D_MODEL), F32),
            jax.ShapeDtypeStruct((n_seq, CONV_WIDTH - 1, CONV_DIM), F32),
            jax.ShapeDtypeStruct((n_seq, CHUNK, SG_DIM), F32),
            jax.ShapeDtypeStruct((n_sample, CONV_WIDTH - 1, CONV_DIM), F32),
            jax.ShapeDtypeStruct((n_sample, steps, SG_GROUPS, SG_HEAD), F32),
            jax.ShapeDtypeStruct((n_side, SUBLANES, 1, D_MODEL), F32),
            jax.ShapeDtypeStruct((n_side, n_sample, D_MODEL), F32),
            jax.ShapeDtypeStruct((D_MODEL, 2 * D_MODEL), BF16),
            jax.ShapeDtypeStruct((CONV_DIM, D_MODEL), BF16),
            jax.ShapeDtypeStruct((SG_DIM, D_MODEL), BF16),
            jax.ShapeDtypeStruct((D_MODEL, D_MODEL), BF16),
        ],
        scratch_shapes=[
            pltpu.VMEM((D_MODEL, MIX_COLS), BF16),
            pltpu.VMEM((STAGE_SLOTS, stage_rows, MIX_COLS), F32),
            pltpu.SemaphoreType.DMA((STAGE_SLOTS,)),
            pltpu.VMEM((SUBLANES, CONV_DIM), F32),
            pltpu.VMEM(((CONV_WIDTH - 1 + steps) * til.ts, CONV_DIM), F32),
            pltpu.VMEM((steps * til.ts, SG_DIM), F32),
        ],
        compiler_params=_params(),
        name="mix",
    )(xp, xs, mod_p, mod_s, g_mix, w_in, w_conv, g_v, w_sg, bias_p, w_sg_s, bias_s, state,
      a, w_ada, b_ada, w_in, w_in, w_in, w_in, w_pa, w_pb, w_out)


def _merge_kernel(xp_ref, xs_ref, h_ref, ya_ref, yb_ref, mp_ref, ms_ref, gffn_ref,
                  wgate_ref, wpa_ref, wpb_ref, wout_ref, x1_ref, h2_ref, *, til):
    tp = til.tp
    h = h_ref[...]
    ya = ya_ref[...]
    yb = yb_ref[...]
    halves = []
    for k in range(D_MODEL // SEG):
        cols = slice(k * SEG, (k + 1) * SEG)
        gate_b_cols = slice(D_MODEL + k * SEG, D_MODEL + (k + 1) * SEG)
        gate_a = jax.nn.sigmoid(_dot(h, wgate_ref[:, cols]))
        gate_b = jax.nn.sigmoid(_dot(h, wgate_ref[:, gate_b_cols]))
        halves.append((gate_a * _dot(ya, wpa_ref[:, cols]) + gate_b * _dot(yb, wpb_ref[:, cols])).astype(BF16))
    merged = jnp.concatenate(halves, axis=1)

    gffn = gffn_ref[...]
    gtp, gts = til.mods(mp_ref, ms_ref, 0)
    scp, scs = til.mods(mp_ref, ms_ref, 1)
    shp, shs = til.mods(mp_ref, ms_ref, 2)

    def finish(rows, x, out, gate, scale, shift):
        x1 = x + gate * out
        x1_ref[rows] = x1
        h2_ref[rows] = _modulated_norm(x1, gffn, scale, shift).astype(BF16)

    out = _dot(merged, wout_ref[...])
    finish(slice(0, tp), xp_ref[...], out[:tp], gtp, scp, shp)
    finish(slice(tp, tp + til.ts), xs_ref[...], out[tp:], gts, scs, shs)


def _merge(xp, xs, h, ya, yb, mod_p, mod_s, g_ffn, w_gate, w_pa, w_pb, w_out, til):
    return pl.pallas_call(
        functools.partial(_merge_kernel, til=til),
        grid=(til.steps,),
        in_specs=[
            til.prompt(D_MODEL), til.sample_rows(D_MODEL),
            til.both(D_MODEL), til.both(CONV_DIM), til.both(SG_DIM),
            *til.mod_tables(MERGE_MODS),
            _resident((1, D_MODEL)),
            _resident(w_gate.shape), _resident(w_pa.shape), _resident(w_pb.shape), _resident(w_out.shape),
        ],
        out_specs=[til.both(D_MODEL), til.both(D_MODEL)],
        out_shape=[
            jax.ShapeDtypeStruct((til.rows, D_MODEL), F32),
            jax.ShapeDtypeStruct((til.rows, D_MODEL), BF16),
        ],
        compiler_params=_params(),
        name="merge",
    )(xp, xs, h, ya, yb, mod_p, mod_s, g_ffn, w_gate, w_pa, w_pb, w_out)


def _ffn_in_kernel(h_ref, wg_ref, wu_ref, wdown_ref, o_ref, wdown_b_ref):
    h = h_ref[...]
    for c in range(0, wg_ref.shape[1], FFN_CHUNK):
        cols = slice(c, c + FFN_CHUNK)
        act = jax.nn.silu(_dot(h, wg_ref[:, cols].astype(BF16))) * _dot(h, wu_ref[:, cols].astype(BF16))
        o_ref[:, cols] = act.astype(BF16)
    wdown_b_ref[...] = wdown_ref[...].astype(BF16)


def _ffn_in(h2, w_ffn_in, w_ffn_out, tm, tn):
    rows = h2.shape[0]
    n_blocks = D_FF // tn
    m_steps = rows // tm
    slab_rows = D_FF // (n_blocks * m_steps)
    assert rows % tm == 0 and D_FF % tn == 0 and D_FF % (16 * n_blocks * m_steps) == 0
    slab = pl.BlockSpec((slab_rows, D_MODEL), lambda j, i: (j * m_steps + i, 0))
    pipeline = pltpu.emit_pipeline(
        _ffn_in_kernel,
        grid=(n_blocks, m_steps),
        in_specs=[
            pl.BlockSpec((tm, D_MODEL), lambda j, i: (i, 0), pipeline_mode=pl.Buffered(3)),
            pl.BlockSpec((D_MODEL, tn), lambda j, i: (0, j)),
            pl.BlockSpec((D_MODEL, tn), lambda j, i: (0, j + n_blocks)),
            slab,
        ],
        out_specs=[pl.BlockSpec((tm, tn), lambda j, i: (i, j)), slab],
    )

    def streamed(h_hbm, wg_hbm, wu_hbm, wdown_hbm, o_hbm, wdown_b_hbm):
        pipeline(h_hbm, wg_hbm, wu_hbm, wdown_hbm, o_hbm, wdown_b_hbm)

    return pl.pallas_call(
        streamed,
        in_specs=[_HBM] * 4,
        out_specs=[_HBM] * 2,
        out_shape=[
            jax.ShapeDtypeStruct((rows, D_FF), BF16),
            jax.ShapeDtypeStruct((D_FF, D_MODEL), BF16),
        ],
        compiler_params=pltpu.CompilerParams(vmem_limit_bytes=VMEM_LIMIT_BYTES),
        name="ffn_in",
    )(h2, w_ffn_in, w_ffn_in, w_ffn_out)


def _ffn_out_kernel(a_ref, x1_ref, mp_ref, ms_ref, gfin_ref, w_ref, yp_ref, ys_ref, *, til):
    tp, rows = til.tp, til.tp + til.ts
    gfin = gfin_ref[...]
    out_all = _dot(a_ref[...], w_ref[...])
    for g in range(til.groups):
        gtp, gts = til.mods(mp_ref, ms_ref, 0, g)
        r0 = g * rows
        out = out_all[r0:r0 + rows]
        yp_ref[g * tp:(g + 1) * tp] = _rms(x1_ref[r0:r0 + tp] + gtp * out[:tp], gfin)
        ys_ref[:, til.position(g), :] = _rms(x1_ref[r0 + tp:r0 + rows] + gts * out[tp:], gfin)


def _ffn_out(act, x1, mod_p, mod_s, g_final, w_ffn_out, til, steps):
    return pl.pallas_call(
        functools.partial(_ffn_out_kernel, til=til),
        grid=(til.steps,),
        in_specs=[
            til.both(D_FF), til.both(D_MODEL),
            *til.mod_tables(FFN_OUT_MODS),
            _resident((1, D_MODEL)),
            _resident(w_ffn_out.shape),
        ],
        out_specs=[til.prompt(D_MODEL), til.sample_block(steps, D_MODEL)],
        out_shape=[
            jax.ShapeDtypeStruct((til.steps * til.groups * til.tp, D_MODEL), F32),
            jax.ShapeDtypeStruct((til.n_sample, steps, D_MODEL), F32),
        ],
        compiler_params=_params(),
        name="ffn_out",
    )(act, x1, mod_p, mod_s, g_final, w_ffn_out)


def _lane_expand(per_group):
    return jnp.repeat(per_group, SG_HEAD, axis=1)


def kernel(x_prompt, x_sample, state_conv, c_prompt, c_sample, g_mix, g_ffn, w_ada, b_ada, w_in, w_conv, g_v,
           w_sg, b_sg, w_pa, w_pb, w_out, w_ffn_in, w_ffn_out, g_final):
    assert g_mix.shape[0] == 1, "one layer"
    batch, seq, _ = x_prompt.shape
    n_sample, steps, _ = x_sample.shape
    assert batch <= SUBLANES and seq % CHUNK == 0
    row = lambda v: v.reshape(1, -1)

    a, mod_p, mod_s = _ada(c_prompt, c_sample, w_ada[0], row(b_ada[0]), ADA_SLOTS)

    xp = x_prompt.reshape(batch * seq, D_MODEL)
    xs = x_sample
    state = state_conv[0]
    til = _Tiling(batch * seq, steps * n_sample, seq, n_sample, steps, tp=256)

    bias_p = _lane_expand(b_sg[0][:, :CHUNK].T)
    w_sg_s = _lane_expand(w_sg[0][:, :steps, :steps].transpose(1, 2, 0).reshape(steps * steps, SG_GROUPS))
    w_sg_s = w_sg_s.reshape(steps * steps, 1, SG_DIM)
    bias_s = _lane_expand(b_sg[0][:, :steps].T).reshape(steps, 1, SG_DIM)
    h, ya, yb, xs_rows, zst_p, vst_p, z_s, vn_s, mod2_p, mod2_s, w_gate_b, w_pa_b, w_pb_b, w_out_b = _mix(
        xp, xs, mod_p, mod_s, row(g_mix[0]), w_in[0], w_conv[0], row(g_v[0]), w_sg[0], bias_p, w_sg_s, bias_s,
        state, a, w_ada[0], row(b_ada[0]), w_pa[0], w_pb[0], w_out[0], til, steps)

    x1, h2 = _merge(xp, xs_rows, h, ya, yb, mod2_p, mod2_s, row(g_ffn[0]), w_gate_b, w_pa_b, w_pb_b, w_out_b, til)
    act, w_down_b = _ffn_in(h2, w_ffn_in[0], w_ffn_out[0], tm=4 * (til.tp + til.ts), tn=512)
    y_p, y_s = _ffn_out(act, x1, mod2_p, mod2_s, row(g_final), w_down_b, til.regrouped(2), steps)

    y_prompt = y_p.reshape(batch, seq, D_MODEL)
    conv_prompt = zst_p[None]
    sgv_prompt = vst_p.reshape(1, batch, CHUNK, SG_GROUPS, SG_HEAD)
    return (y_prompt, y_s, conv_prompt, z_s[None], sgv_prompt, vn_s[None])
```

```python
import functools

import jax
import jax.numpy as jnp
from jax import lax
from jax.experimental import pallas as pl
from jax.experimental.pallas import tpu as pltpu

D_MODEL = 2048
CONV_DIM = D_MODEL // 2
CONV_WIDTH = 3
SG_DIM = D_MODEL // 2
SG_GROUPS = 8
SG_HEAD = SG_DIM // SG_GROUPS
CHUNK = 128
D_FF = 5632
EPS = 1e-6
SEG = 1024
MIX_COLS = 3 * CONV_DIM + 2 * SG_DIM
SUBLANES = 8
FFN_CHUNK = 256
STAGE_SLOTS = 4
VMEM_LIMIT_BYTES = 60 * 1024 * 1024

MOD_ORDER = (1, 0, 2, 4, 3, 5)
ADA_SLOTS, SIDE_SLOTS = (0, 2), (2, 4)
MIX_MODS, MERGE_MODS, FFN_OUT_MODS = (0, 2), (0, 3), (3, 1)
SIDE_TN = 256

F32 = jnp.float32
BF16 = jnp.bfloat16


def _dot(a, b):
    return jnp.dot(a, b, preferred_element_type=F32)


def _rms(x, gain):
    return x * lax.rsqrt(jnp.mean(x * x, axis=-1, keepdims=True) + EPS) * gain


def _modulated_norm(x, gain, scale, shift):
    return _rms(x, gain * (1 + scale)) + shift


def _rows(*parts):
    return jnp.concatenate(parts, axis=0)


def _load_bf16(jobs, stage_ref, sem_ref):
    n_slots, rows, n = stage_ref.shape
    chunks = [(src, col0, dst, r0) for src, col0, dst in jobs for r0 in range(0, dst.shape[0], rows)]
    assert all(dst.shape[1] == n and dst.shape[0] % rows == 0 for _, _, dst in jobs)

    def chunk_copy(idx):
        src, col0, _, r0 = chunks[idx]
        slot = idx % n_slots
        return pltpu.make_async_copy(src.at[pl.ds(r0, rows), pl.ds(col0, n)], stage_ref.at[slot], sem_ref.at[slot])

    ahead = n_slots - 1
    for idx in range(min(ahead, len(chunks))):
        chunk_copy(idx).start()
    for idx, (_, _, dst, r0) in enumerate(chunks):
        if idx + ahead < len(chunks):
            chunk_copy(idx + ahead).start()
        chunk_copy(idx).wait()
        dst[r0:r0 + rows, :] = stage_ref[idx % n_slots].astype(BF16)


def _ada_columns(first_slot, tn):
    per = D_MODEL // tn

    def index_map(j):
        comp = 0
        for slot, c in enumerate(MOD_ORDER):
            comp = jnp.where(first_slot + j // per == slot, c, comp)
        return (0, comp * per + j % per)

    return index_map


def _ada_table_specs(n_sample, tn):
    per = D_MODEL // tn
    return [
        pl.BlockSpec((None, SUBLANES, 1, tn), lambda j: (j // per, 0, 0, j % per)),
        pl.BlockSpec((None, n_sample, tn), lambda j: (j // per, 0, j % per)),
    ]


def _ada_block(a, w_ref, b_ref, mp_ref, ms_ref):
    r = _dot(a, w_ref[...].astype(BF16)) + b_ref[...]
    for n in range(SUBLANES):
        mp_ref[n] = r[n:n + 1]
    ms_ref[...] = r[SUBLANES:SUBLANES + ms_ref.shape[0]]


def _ada_kernel(cp_ref, cs_ref, w_ref, b_ref, a_ref, mp_ref, ms_ref, rows_ref):
    batch, n_sample = cp_ref.shape[0], cs_ref.shape[0]
    rows_ref[...] = jnp.zeros_like(rows_ref)
    rows_ref[0:batch] = jax.nn.silu(cp_ref[...])
    rows_ref[SUBLANES:SUBLANES + n_sample] = jax.nn.silu(cs_ref[...])
    a = rows_ref[...].astype(BF16)
    a_ref[...] = a
    _ada_block(a, w_ref, b_ref, mp_ref, ms_ref)


def _ada(c_prompt, c_sample, w_ada, b_ada, slots):
    tn = 1024
    first, count = slots
    batch, n_sample = c_prompt.shape[0], c_sample.shape[0]
    rows = -(-(SUBLANES + n_sample) // 16) * 16
    assert batch <= SUBLANES
    return pl.pallas_call(
        _ada_kernel,
        grid=(count * (D_MODEL // tn),),
        in_specs=[
            pl.BlockSpec((batch, D_MODEL), lambda j: (0, 0)),
            pl.BlockSpec((n_sample, D_MODEL), lambda j: (0, 0)),
            pl.BlockSpec((D_MODEL, tn), _ada_columns(first, tn)),
            pl.BlockSpec((1, tn), _ada_columns(first, tn)),
        ],
        out_specs=[pl.BlockSpec((rows, D_MODEL), lambda j: (0, 0)), *_ada_table_specs(n_sample, tn)],
        out_shape=[
            jax.ShapeDtypeStruct((rows, D_MODEL), BF16),
            jax.ShapeDtypeStruct((count, SUBLANES, 1, D_MODEL), F32),
            jax.ShapeDtypeStruct((count, n_sample, D_MODEL), F32),
        ],
        scratch_shapes=[pltpu.VMEM((rows, D_MODEL), F32)],
        compiler_params=pltpu.CompilerParams(
            dimension_semantics=("arbitrary",), vmem_limit_bytes=VMEM_LIMIT_BYTES),
        name="ada",
    )(c_prompt, c_sample, w_ada, b_ada)


class _Tiling:
    def __init__(self, rows_p, rows_s, seq, n_sample, positions, tp, groups=1):
        n_groups = rows_p // tp
        self.tp = tp
        self.groups = groups
        self.steps = n_groups // groups
        self.ts = rows_s // n_groups
        self.rows = rows_p + rows_s
        self.tiles_per_seq = seq // tp
        self.positions = positions
        self.n_sample = n_sample
        self._args = (rows_p, rows_s, seq, n_sample, positions, tp)
        assert rows_p % tp == 0 and seq % tp == 0 and rows_s % n_groups == 0
        assert self.ts % 16 == 0 and n_sample % self.ts == 0 and rows_s == n_sample * positions
        assert positions % groups == 0 and self.tiles_per_seq % groups == 0

    def regrouped(self, groups):
        return _Tiling(*self._args, groups=groups)

    def group(self, g=0):
        return pl.program_id(0) * self.groups + g

    def prompt(self, width):
        return pl.BlockSpec((self.groups * self.tp, width), lambda i: (i, 0))

    def sample_rows(self, width):
        return pl.BlockSpec((self.groups * self.ts, width), lambda i: (i, 0))

    def sample_block(self, *shape):
        return pl.BlockSpec((self.ts, *shape), lambda i: (i * self.groups // self.positions,) + (0,) * len(shape))

    def position(self, g=0):
        return self.group(g) % self.positions

    def both(self, width):
        return pl.BlockSpec((self.groups * (self.tp + self.ts), width), lambda i: (i, 0))

    def mod_tables(self, slots):
        first, count = slots
        assert first % count == 0
        return [
            pl.BlockSpec((count, SUBLANES, 1, D_MODEL), lambda i: (first // count, 0, 0, 0),
                         pipeline_mode=pl.Buffered(1)),
            pl.BlockSpec((count, self.n_sample, D_MODEL), lambda i: (first // count, 0, 0),
                         pipeline_mode=pl.Buffered(1)),
        ]

    def mods(self, mp_ref, ms_ref, k, g=0):
        s = self.group(g)
        start = pl.multiple_of((s // self.positions) * self.ts, self.ts)
        return mp_ref[k, s // self.tiles_per_seq], ms_ref[k, pl.ds(start, self.ts), :]


def _resident(shape):
    return pl.BlockSpec(shape, lambda i: (0,) * len(shape), pipeline_mode=pl.Buffered(1))


_HBM = pl.BlockSpec(memory_space=pl.ANY)


def _params():
    return pltpu.CompilerParams(dimension_semantics=("arbitrary",), vmem_limit_bytes=VMEM_LIMIT_BYTES)


def _mix_kernel(xp_ref, xs_ref, mp_ref, ms_ref, gmix_ref, w_hbm, wconv_ref, gv_ref,
                wsg_ref, bsgp_ref, wsgs_ref, bsgs_ref, state_ref,
                a_ref, wada_ref, bada_ref, wg0_ref, wg1_ref, wg2_ref, wg3_ref, wpa_ref, wpb_ref, wout_ref,
                h_ref, ya_ref, yb_ref, xsr_ref, zstp_ref, vstp_ref, zs_ref, vs_ref,
                mpr_ref, msr_ref, wgate_b_ref, wpa_b_ref, wpb_b_ref, wout_b_ref,
                w_ref, stage_ref, sem_ref, carry_ref, zhist_ref, vhist_ref, *, til, steps):
    assert CONV_WIDTH == 3, "the conv mixer below is written for three taps"
    i = pl.program_id(0)
    tp, ts = til.tp, til.ts
    t = til.position()

    def slot(ref, k):
        start = k * ts if isinstance(k, int) else pl.multiple_of(k * ts, ts)
        return ref[pl.ds(start, ts), :]

    @pl.when(i == 0)
    def _():
        _load_bf16([(w_hbm, 0, w_ref)], stage_ref, sem_ref)
        vhist_ref[...] = jnp.zeros_like(vhist_ref)

    @pl.when(i % til.tiles_per_seq == 0)
    def _():
        carry_ref[...] = jnp.zeros_like(carry_ref)

    gmix = gmix_ref[...]
    scp, scs = til.mods(mp_ref, ms_ref, 0)
    shp, shs = til.mods(mp_ref, ms_ref, 1)
    xs = xs_ref[:, t, :]
    xsr_ref[...] = xs
    h = _rows(_modulated_norm(xp_ref[...], gmix, scp, shp), _modulated_norm(xs, gmix, scs, shs)).astype(BF16)
    h_ref[...] = h

    for k, wg_ref in enumerate((wg0_ref, wg1_ref, wg2_ref, wg3_ref)):
        wgate_b_ref[:, k * SEG:(k + 1) * SEG] = wg_ref[...].astype(BF16)
    wpa_b_ref[...] = wpa_ref[...].astype(BF16)
    wpb_b_ref[...] = wpb_ref[...].astype(BF16)
    wout_b_ref[...] = wout_ref[...].astype(BF16)
    proj = lambda k: _dot(h_ref[...], w_ref[:, k * SEG:(k + 1) * SEG])

    vn = _rms(jax.nn.gelu(proj(4)), gv_ref[...])
    z_all = proj(1) * proj(2)
    vstp_ref[...] = vn[tp - CHUNK:tp]
    vb = vn[:tp].astype(BF16)
    causal = (lax.broadcasted_iota(jnp.int32, (CHUNK, CHUNK), 0)
              >= lax.broadcasted_iota(jnp.int32, (CHUNK, CHUNK), 1))
    wgs = [jnp.where(causal, wsg_ref[g], 0.0).astype(BF16) for g in range(SG_GROUPS)]
    bias = bsgp_ref[...]
    chunks = []
    for c in range(tp // CHUNK):
        rows = slice(c * CHUNK, (c + 1) * CHUNK)
        parts = [_dot(wgs[g], vb[rows, g * SG_HEAD:(g + 1) * SG_HEAD]) for g in range(SG_GROUPS)]
        chunks.append(jnp.concatenate(parts, axis=1) + bias)
    gu = jax.nn.gelu(proj(3))
    b_gate = proj(0)
    yb_ref[:tp] = (gu[:tp] * _rows(*chunks)).astype(BF16)

    vs = vn[tp:]
    for g in range(SG_GROUPS):
        vs_ref[:, t, g, :] = vs[:, g * SG_HEAD:(g + 1) * SG_HEAD]
    vhist_ref[pl.ds(pl.multiple_of(t * ts, ts), ts), :] = vs
    sp = bsgs_ref[t]
    for s in range(steps):
        w_ts = jnp.where(s <= t, wsgs_ref[t * steps + s], 0.0)
        sp = sp + w_ts * slot(vhist_ref, s)
    yb_ref[tp:] = (gu[tp:] * sp).astype(BF16)

    wc = wconv_ref[...]
    z = z_all[:tp]
    carry = carry_ref[...]
    prev2, prev1 = carry[SUBLANES - 2:SUBLANES - 1], carry[SUBLANES - 1:SUBLANES]
    row = lax.broadcasted_iota(jnp.int32, (SUBLANES, CONV_DIM), 0)
    z1 = pltpu.roll(z, 1, 0)
    z2 = pltpu.roll(z, 2, 0)
    z1 = _rows(jnp.where(row == 0, prev1, z1[:SUBLANES]), z1[SUBLANES:])
    z2 = _rows(jnp.where(row == 0, prev2, jnp.where(row == 1, prev1, z2[:SUBLANES])), z2[SUBLANES:])
    conv = wc[0:1] * z2 + wc[1:2] * z1 + wc[2:3] * z
    ya_ref[:tp] = (b_gate[:tp] * conv).astype(BF16)
    carry_ref[...] = z[tp - SUBLANES:]
    zstp_ref[...] = z[tp - (CONV_WIDTH - 1):]

    zs = z_all[tp:]
    for k in range(CONV_WIDTH - 1):
        zhist_ref[k * ts:(k + 1) * ts, :] = state_ref[:, k, :]
    zhist_ref[pl.ds(pl.multiple_of((t + CONV_WIDTH - 1) * ts, ts), ts), :] = zs
    conv_s = wc[0:1] * slot(zhist_ref, t) + wc[1:2] * slot(zhist_ref, t + 1) + wc[2:3] * zs
    ya_ref[tp:] = (b_gate[tp:] * conv_s).astype(BF16)
    zs_ref[:, jnp.maximum(t - (steps - (CONV_WIDTH - 1)), 0), :] = zs

    _ada_block(a_ref[...], wada_ref, bada_ref, mpr_ref, msr_ref)


def _row_slab(rows_total, n_steps, width):
    assert rows_total % (16 * n_steps) == 0
    return pl.BlockSpec((rows_total // n_steps, width), lambda i: (i, 0))


def _mix(xp, xs, mod_p, mod_s, g_mix, w_in, w_conv, g_v, w_sg, bias_p, w_sg_s, bias_s, state,
         a, w_ada, b_ada, w_pa, w_pb, w_out, til, steps):
    n_sample = til.n_sample
    n_seq = til.steps // til.tiles_per_seq
    per_seq = lambda rows, width: pl.BlockSpec((None, rows, width), lambda i: (i // til.tiles_per_seq, 0, 0))
    stage_rows = 64
    assert steps >= CONV_WIDTH - 1
    n_side = SIDE_SLOTS[1]
    assert til.steps * SIDE_TN == n_side * D_MODEL, "one modulation-table column block per grid step"
    slab = lambda k, width: _row_slab(k, til.steps, width)
    return pl.pallas_call(
        functools.partial(_mix_kernel, til=til, steps=steps),
        grid=(til.steps,),
        in_specs=[
            til.prompt(D_MODEL), til.sample_block(steps, D_MODEL), *til.mod_tables(MIX_MODS),
            _resident((1, D_MODEL)),
            _HBM,
            _resident((CONV_WIDTH, CONV_DIM)),
            _resident((1, SG_DIM)),
            _resident((SG_GROUPS, CHUNK, CHUNK)),
            _resident((CHUNK, SG_DIM)),
            _resident((steps * steps, 1, SG_DIM)),
            _resident((steps, 1, SG_DIM)),
            til.sample_block(CONV_WIDTH - 1, CONV_DIM),
            _resident(a.shape),
            pl.BlockSpec((D_MODEL, SIDE_TN), _ada_columns(SIDE_SLOTS[0], SIDE_TN)),
            pl.BlockSpec((1, SIDE_TN), _ada_columns(SIDE_SLOTS[0], SIDE_TN)),
            *[pl.BlockSpec((D_MODEL // til.steps, SEG), lambda i, k=k: (i, MIX_COLS // SEG + k))
              for k in range(2 * D_MODEL // SEG)],
            slab(CONV_DIM, D_MODEL), slab(SG_DIM, D_MODEL), slab(D_MODEL, D_MODEL),
        ],
        out_specs=[
            til.both(D_MODEL), til.both(CONV_DIM), til.both(SG_DIM), til.sample_rows(D_MODEL),
            per_seq(CONV_WIDTH - 1, CONV_DIM), per_seq(CHUNK, SG_DIM),
            til.sample_block(CONV_WIDTH - 1, CONV_DIM), til.sample_block(steps, SG_GROUPS, SG_HEAD),
            *_ada_table_specs(n_sample, SIDE_TN),
            slab(D_MODEL, 2 * D_MODEL), slab(CONV_DIM, D_MODEL), slab(SG_DIM, D_MODEL), slab(D_MODEL, D_MODEL),
        ],
        out_shape=[
            jax.ShapeDtypeStruct((til.rows, D_MODEL), BF16),
            jax.ShapeDtypeStruct((til.rows, CONV_DIM), BF16),
            jax.ShapeDtypeStruct((til.rows, SG_DIM), BF16),
            jax.ShapeDtypeStruct((til.steps * til.ts, D_MODEL), F32),
            jax.ShapeDtypeStruct((n_seq, CONV_WIDTH - 1, CONV_DIM), F32),
            jax.ShapeDtypeStruct((n_seq, CHUNK, SG_DIM), F32),
            jax.ShapeDtypeStruct((n_sample, CONV_WIDTH - 1, CONV_DIM), F32),
            jax.ShapeDtypeStruct((n_sample, steps, SG_GROUPS, SG_HEAD), F32),
            jax.ShapeDtypeStruct((n_side, SUBLANES, 1, D_MODEL), F32),
            jax.ShapeDtypeStruct((n_side, n_sample, D_MODEL), F32),
            jax.ShapeDtypeStruct((D_MODEL, 2 * D_MODEL), BF16),
            jax.ShapeDtypeStruct((CONV_DIM, D_MODEL), BF16),
            jax.ShapeDtypeStruct((SG_DIM, D_MODEL), BF16),
            jax.ShapeDtypeStruct((D_MODEL, D_MODEL), BF16),
        ],
        scratch_shapes=[
            pltpu.VMEM((D_MODEL, MIX_COLS), BF16),
            pltpu.VMEM((STAGE_SLOTS, stage_rows, MIX_COLS), F32),
            pltpu.SemaphoreType.DMA((STAGE_SLOTS,)),
            pltpu.VMEM((SUBLANES, CONV_DIM), F32),
            pltpu.VMEM(((CONV_WIDTH - 1 + steps) * til.ts, CONV_DIM), F32),
            pltpu.VMEM((steps * til.ts, SG_DIM), F32),
        ],
        compiler_params=_params(),
        name="mix",
    )(xp, xs, mod_p, mod_s, g_mix, w_in, w_conv, g_v, w_sg, bias_p, w_sg_s, bias_s, state,
      a, w_ada, b_ada, w_in, w_in, w_in, w_in, w_pa, w_pb, w_out)


def _merge_kernel(xp_ref, xs_ref, h_ref, ya_ref, yb_ref, mp_ref, ms_ref, gffn_ref,
                  wgate_ref, wpa_ref, wpb_ref, wout_ref, x1_ref, h2_ref, *, til):
    tp = til.tp
    h = h_ref[...]
    ya = ya_ref[...]
    yb = yb_ref[...]
    halves = []
    for k in range(D_MODEL // SEG):
        cols = slice(k * SEG, (k + 1) * SEG)
        gate_b_cols = slice(D_MODEL + k * SEG, D_MODEL + (k + 1) * SEG)
        gate_a = jax.nn.sigmoid(_dot(h, wgate_ref[:, cols]))
        gate_b = jax.nn.sigmoid(_dot(h, wgate_ref[:, gate_b_cols]))
        halves.append((gate_a * _dot(ya, wpa_ref[:, cols]) + gate_b * _dot(yb, wpb_ref[:, cols])).astype(BF16))
    merged = jnp.concatenate(halves, axis=1)

    gffn = gffn_ref[...]
    gtp, gts = til.mods(mp_ref, ms_ref, 0)
    scp, scs = til.mods(mp_ref, ms_ref, 1)
    shp, shs = til.mods(mp_ref, ms_ref, 2)

    def finish(rows, x, out, gate, scale, shift):
        x1 = x + gate * out
        x1_ref[rows] = x1
        h2_ref[rows] = _modulated_norm(x1, gffn, scale, shift).astype(BF16)

    out = _dot(merged, wout_ref[...])
    finish(slice(0, tp), xp_ref[...], out[:tp], gtp, scp, shp)
    finish(slice(tp, tp + til.ts), xs_ref[...], out[tp:], gts, scs, shs)


def _merge(xp, xs, h, ya, yb, mod_p, mod_s, g_ffn, w_gate, w_pa, w_pb, w_out, til):
    return pl.pallas_call(
        functools.partial(_merge_kernel, til=til),
        grid=(til.steps,),
        in_specs=[
            til.prompt(D_MODEL), til.sample_rows(D_MODEL),
            til.both(D_MODEL), til.both(CONV_DIM), til.both(SG_DIM),
            *til.mod_tables(MERGE_MODS),
            _resident((1, D_MODEL)),
            _resident(w_gate.shape), _resident(w_pa.shape), _resident(w_pb.shape), _resident(w_out.shape),
        ],
        out_specs=[til.both(D_MODEL), til.both(D_MODEL)],
        out_shape=[
            jax.ShapeDtypeStruct((til.rows, D_MODEL), F32),
            jax.ShapeDtypeStruct((til.rows, D_MODEL), BF16),
        ],
        compiler_params=_params(),
        name="merge",
    )(xp, xs, h, ya, yb, mod_p, mod_s, g_ffn, w_gate, w_pa, w_pb, w_out)


def _ffn_in_kernel(h_ref, wg_ref, wu_ref, wdown_ref, o_ref, wdown_b_ref):
    h = h_ref[...]
    for c in range(0, wg_ref.shape[1], FFN_CHUNK):
        cols = slice(c, c + FFN_CHUNK)
        act = jax.nn.silu(_dot(h, wg_ref[:, cols].astype(BF16))) * _dot(h, wu_ref[:, cols].astype(BF16))
        o_ref[:, cols] = act.astype(BF16)
    wdown_b_ref[...] = wdown_ref[...].astype(BF16)


def _ffn_in(h2, w_ffn_in, w_ffn_out, tm, tn):
    rows = h2.shape[0]
    n_blocks = D_FF // tn
    m_steps = rows // tm
    slab_rows = D_FF // (n_blocks * m_steps)
    assert rows % tm == 0 and D_FF % tn == 0 and D_FF % (16 * n_blocks * m_steps) == 0
    slab = pl.BlockSpec((slab_rows, D_MODEL), lambda i, j: (i * n_blocks + j, 0))
    return pl.pallas_call(
        _ffn_in_kernel,
        grid=(m_steps, n_blocks),
        in_specs=[
            pl.BlockSpec((tm, D_MODEL), lambda i, j: (i, 0)),
            pl.BlockSpec((D_MODEL, tn), lambda i, j: (0, j)),
            pl.BlockSpec((D_MODEL, tn), lambda i, j: (0, j + n_blocks)),
            slab,
        ],
        out_specs=[pl.BlockSpec((tm, tn), lambda i, j: (i, j)), slab],
        out_shape=[
            jax.ShapeDtypeStruct((rows, D_FF), BF16),
            jax.ShapeDtypeStruct((D_FF, D_MODEL), BF16),
        ],
        compiler_params=pltpu.CompilerParams(
            dimension_semantics=("arbitrary", "arbitrary"), vmem_limit_bytes=VMEM_LIMIT_BYTES),
        name="ffn_in",
    )(h2, w_ffn_in, w_ffn_in, w_ffn_out)


def _ffn_out_kernel(a_ref, x1_ref, mp_ref, ms_ref, gfin_ref, w_ref, yp_ref, ys_ref, *, til):
    tp, rows = til.tp, til.tp + til.ts
    gfin = gfin_ref[...]
    out_all = _dot(a_ref[...], w_ref[...])
    for g in range(til.groups):
        gtp, gts = til.mods(mp_ref, ms_ref, 0, g)
        r0 = g * rows
        out = out_all[r0:r0 + rows]
        yp_ref[g * tp:(g + 1) * tp] = _rms(x1_ref[r0:r0 + tp] + gtp * out[:tp], gfin)
        ys_ref[:, til.position(g), :] = _rms(x1_ref[r0 + tp:r0 + rows] + gts * out[tp:], gfin)


def _ffn_out(act, x1, mod_p, mod_s, g_final, w_ffn_out, til, steps):
    return pl.pallas_call(
        functools.partial(_ffn_out_kernel, til=til),
        grid=(til.steps,),
        in_specs=[
            til.both(D_FF), til.both(D_MODEL),
            *til.mod_tables(FFN_OUT_MODS),
            _resident((1, D_MODEL)),
            _resident(w_ffn_out.shape),
        ],
        out_specs=[til.prompt(D_MODEL), til.sample_block(steps, D_MODEL)],
        out_shape=[
            jax.ShapeDtypeStruct((til.steps * til.groups * til.tp, D_MODEL), F32),
            jax.ShapeDtypeStruct((til.n_sample, steps, D_MODEL), F32),
        ],
        compiler_params=_params(),
        name="ffn_out",
    )(act, x1, mod_p, mod_s, g_final, w_ffn_out)


def _lane_expand(per_group):
    return jnp.repeat(per_group, SG_HEAD, axis=1)


def kernel(x_prompt, x_sample, state_conv, c_prompt, c_sample, g_mix, g_ffn, w_ada, b_ada, w_in, w_conv, g_v,
           w_sg, b_sg, w_pa, w_pb, w_out, w_ffn_in, w_ffn_out, g_final):
    assert g_mix.shape[0] == 1, "one layer"
    batch, seq, _ = x_prompt.shape
    n_sample, steps, _ = x_sample.shape
    assert batch <= SUBLANES and seq % CHUNK == 0
    row = lambda v: v.reshape(1, -1)

    a, mod_p, mod_s = _ada(c_prompt, c_sample, w_ada[0], row(b_ada[0]), ADA_SLOTS)

    xp = x_prompt.reshape(batch * seq, D_MODEL)
    xs = x_sample
    state = state_conv[0]
    til = _Tiling(batch * seq, steps * n_sample, seq, n_sample, steps, tp=256)

    bias_p = _lane_expand(b_sg[0][:, :CHUNK].T)
    w_sg_s = _lane_expand(w_sg[0][:, :steps, :steps].transpose(1, 2, 0).reshape(steps * steps, SG_GROUPS))
    w_sg_s = w_sg_s.reshape(steps * steps, 1, SG_DIM)
    bias_s = _lane_expand(b_sg[0][:, :steps].T).reshape(steps, 1, SG_DIM)
    h, ya, yb, xs_rows, zst_p, vst_p, z_s, vn_s, mod2_p, mod2_s, w_gate_b, w_pa_b, w_pb_b, w_out_b = _mix(
        xp, xs, mod_p, mod_s, row(g_mix[0]), w_in[0], w_conv[0], row(g_v[0]), w_sg[0], bias_p, w_sg_s, bias_s,
        state, a, w_ada[0], row(b_ada[0]), w_pa[0], w_pb[0], w_out[0], til, steps)

    x1, h2 = _merge(xp, xs_rows, h, ya, yb, mod2_p, mod2_s, row(g_ffn[0]), w_gate_b, w_pa_b, w_pb_b, w_out_b, til)
    act, w_down_b = _ffn_in(h2, w_ffn_in[0], w_ffn_out[0], tm=8 * (til.tp + til.ts), tn=512)
    y_p, y_s = _ffn_out(act, x1, mod2_p, mod2_s, row(g_final), w_down_b, til.regrouped(2), steps)

    y_prompt = y_p.reshape(batch, seq, D_MODEL)
    conv_prompt = zst_p[None]
    sgv_prompt = vst_p.reshape(1, batch, CHUNK, SG_GROUPS, SG_HEAD)
    return (y_prompt, y_s, conv_prompt, z_s[None], sgv_prompt, vn_s[None])
```
